```python
import math
import jax, jax.numpy as jnp
from jax import lax
import numpy as np

D_MODEL = 1024
BATCH = 2
SEQ = 16384
DEPTH = 1

N_MEM = 256

NSA_WIDTH = D_MODEL // 2
HG_WIDTH = D_MODEL - NSA_WIDTH

NSA_HEAD_DIM = 64
NSA_HEADS = NSA_WIDTH // NSA_HEAD_DIM
NSA_KV_GROUPS = 2
NSA_HPG = NSA_HEADS // NSA_KV_GROUPS
CMP_BLOCK = 32
CMP_STRIDE = 16
CMP_HIDDEN = 256
SEL_BLOCK = 64
SEL_TOP_N = 16
WINDOW = 512
Q_BLOCK = 128
FORCE_BONUS = 1.0e4

HG_KEY_DIM = 128
HG_VAL_DIM = 128
HG_HEADS = HG_WIDTH // HG_VAL_DIM
HG_CHUNK = 64

XA_HEADS = 4
XA_HEAD_DIM = D_MODEL // XA_HEADS

MOE_GROUPS = 4
MOE_EXPERTS_PER_GROUP = 8
MOE_N_EXPERTS = MOE_GROUPS * MOE_EXPERTS_PER_GROUP
MOE_TOP_K = 2
MOE_D_FF = 512
MOE_BLOCK = 128

DEEPNORM_ALPHA = (2.0 * DEPTH) ** 0.25
DEEPNORM_BETA = (8.0 * DEPTH) ** -0.25

LN_EPS = 1e-5
RMS_EPS = 1e-6
NEG_INF = -1e30

NSA_Q_COLS = NSA_HEADS * NSA_HEAD_DIM
NSA_KV_COLS = NSA_KV_GROUPS * NSA_HEAD_DIM
NSA_GATE_COLS = NSA_HEADS * 3
HG_QF_COLS = HG_HEADS * HG_KEY_DIM
HG_IV_COLS = HG_HEADS * HG_VAL_DIM
IN_SPLITS = (NSA_Q_COLS, NSA_KV_COLS, NSA_KV_COLS, NSA_KV_COLS, NSA_KV_COLS, NSA_KV_COLS, NSA_KV_COLS,
             NSA_GATE_COLS, HG_QF_COLS, HG_QF_COLS, HG_IV_COLS, HG_IV_COLS)
VALUE_SPLITS = (False, False, True, False, True, False, True, False, False, False, True, False)
IN_COLS = NSA_Q_COLS + 6 * NSA_KV_COLS + NSA_GATE_COLS + 2 * HG_QF_COLS + 2 * HG_IV_COLS

kernel_name = "hybrid_nsa_hgrn2_hmoe_layer"


def layer_norm(x, g, b):
    xf = x.astype(jnp.float32)
    mu = jnp.mean(xf, axis=-1, keepdims=True)
    var = jnp.mean(jnp.square(xf - mu), axis=-1, keepdims=True)
    return ((xf - mu) * lax.rsqrt(var + LN_EPS) * g + b).astype(x.dtype)


def rms_norm(x, g):
    xf = x.astype(jnp.float32)
    return xf * lax.rsqrt(jnp.mean(xf * xf, axis=-1, keepdims=True) + RMS_EPS) * g


def masked_softmax(s, mask):
    s = jnp.where(mask, s.astype(jnp.float32), NEG_INF)
    p = jax.nn.softmax(s, axis=-1)
    return jnp.where(mask, p, 0.0)


def split_cols(h):
    offs = np.cumsum((0,) + IN_SPLITS)
    return [h[..., int(offs[i]):int(offs[i + 1])] for i in range(len(IN_SPLITS))]


def nsa_compress(k, pe, w1, w2):
    b, s, g, d = k.shape
    n_sub = CMP_BLOCK // CMP_STRIDE
    n_ch = s // CMP_STRIDE
    ncmp = n_ch - n_sub + 1
    ch = k.reshape(b, n_ch, CMP_STRIDE, g, d)
    blocks = jnp.concatenate([ch[:, i:i + ncmp] for i in range(n_sub)], axis=2)
    blocks = blocks + pe[None, None, :, None, :]
    blocks = jnp.moveaxis(blocks, 3, 1).reshape(b, g, ncmp, CMP_BLOCK * d)
    return jax.nn.gelu(blocks @ w1) @ w2


def cmp_to_sel(p, n_sel):
    n_sub = CMP_BLOCK // CMP_STRIDE
    r = SEL_BLOCK // CMP_STRIDE
    ncmp = p.shape[-1]
    pp = jnp.pad(p, [(0, 0)] * (p.ndim - 1) + [(n_sub - 1, r * n_sel - ncmp)])
    return sum(pp[..., m:m + r * n_sel:r] for m in range(r + n_sub - 1))


def nsa_attention(q, kc, vc, ks, vs, kw, vw, gates):
    b, h, s, d = q.shape
    g = NSA_KV_GROUPS
    ncmp = kc.shape[2]
    n_sel = s // SEL_BLOCK
    top_n = min(SEL_TOP_N, n_sel)
    scale = d ** -0.5
    kw_pad = jnp.pad(kw, ((0, 0), (0, 0), (WINDOW, 0), (0, 0)))
    vw_pad = jnp.pad(vw, ((0, 0), (0, 0), (WINDOW, 0), (0, 0)))
    cmp_end = jnp.arange(ncmp) * CMP_STRIDE + CMP_BLOCK - 1
    blk_ids = jnp.arange(n_sel)
    bi = jnp.arange(b)[:, None, None, None]
    gi = jnp.arange(g)[None, :, None, None]
    offs = jnp.arange(SEL_BLOCK)

    def one_block(qb):
        s0 = qb * Q_BLOCK
        t = s0 + jnp.arange(Q_BLOCK)
        qq = lax.dynamic_slice_in_dim(q, s0, Q_BLOCK, axis=2)
        qq = qq.reshape(b, g, NSA_HPG, Q_BLOCK, d) * scale
        sc = jnp.einsum('bgrtd,bgnd->bgrtn', qq, kc)
        pc = masked_softmax(sc, cmp_end[None, :] <= t[:, None])
        o_c = jnp.einsum('bgrtn,bgnd->bgrtd', pc.astype(vc.dtype), vc)
        imp = cmp_to_sel(jnp.sum(pc, axis=2), n_sel)
        cur = t // SEL_BLOCK
        valid = blk_ids[None, :] <= cur[:, None]
        forced = ((blk_ids[None, :] == 0) | (blk_ids[None, :] == cur[:, None])
                  | (blk_ids[None, :] == cur[:, None] - 1))
        score = jnp.where(valid, imp + jnp.where(forced, FORCE_BONUS, 0.0), -1.0)
        _, idx = lax.top_k(score, top_n)
        pos = (idx[..., None] * SEL_BLOCK + offs).reshape(b, g, Q_BLOCK, top_n * SEL_BLOCK)
        k_sel = ks[bi, gi, pos]
        v_sel = vs[bi, gi, pos]
        ss = jnp.einsum('bgrtd,bgtkd->bgrtk', qq, k_sel)
        ps = masked_softmax(ss, (pos <= t[None, None, :, None])[:, :, None])
        o_s = jnp.einsum('bgrtk,bgtkd->bgrtd', ps.astype(v_sel.dtype), v_sel)
        kwb = lax.dynamic_slice_in_dim(kw_pad, s0, WINDOW + Q_BLOCK, axis=2)
        vwb = lax.dynamic_slice_in_dim(vw_pad, s0, WINDOW + Q_BLOCK, axis=2)
        kpos = s0 - WINDOW + jnp.arange(WINDOW + Q_BLOCK)
        m_w = ((kpos[None, :] <= t[:, None]) & (kpos[None, :] > t[:, None] - WINDOW)
               & (kpos[None, :] >= 0))
        sw = jnp.einsum('bgrtd,bgkd->bgrtk', qq, kwb)
        pw = masked_softmax(sw, m_w)
        o_w = jnp.einsum('bgrtk,bgkd->bgrtd', pw.astype(vwb.dtype), vwb)
        gb = lax.dynamic_slice_in_dim(gates, s0, Q_BLOCK, axis=2).reshape(b, g, NSA_HPG, Q_BLOCK, 3)
        o = gb[..., 0:1] * o_c + gb[..., 1:2] * o_s + gb[..., 2:3] * o_w
        return o.reshape(b, h, Q_BLOCK, d)

    out = lax.map(one_block, jnp.arange(s // Q_BLOCK))
    return jnp.transpose(out, (1, 0, 3, 2, 4)).reshape(b, s, h * d)


def hgrn2_recurrence(q, f_logit, i, lb):
    b, s, h, dk = q.shape
    dv = i.shape[-1]
    lb = lb.reshape(h, dk)
    log_f = jnp.logaddexp(jnp.log(lb), jnp.log1p(-lb) + jax.nn.log_sigmoid(f_logit.astype(jnp.float32)))
    k = -jnp.expm1(log_f)
    nc = s // HG_CHUNK

    def to_chunks(a):
        return jnp.transpose(a.astype(jnp.float32).reshape(b, nc, HG_CHUNK, h, a.shape[-1]), (1, 0, 3, 2, 4))

    qc, kc, vc, lc = to_chunks(q), to_chunks(k), to_chunks(i), to_chunks(log_f)
    bc = jnp.cumsum(lc, axis=3)
    causal = jnp.tril(jnp.ones((HG_CHUNK, HG_CHUNK), dtype=bool))

    def step(state, xs):
        qt, kt, vt, bt = xs
        inter = jnp.einsum('bhtk,bhkv->bhtv', qt * jnp.exp(bt), state)
        diff = bt[:, :, :, None, :] - bt[:, :, None, :, :]
        decay = jnp.where(causal[:, :, None], jnp.exp(jnp.minimum(diff, 0.0)), 0.0)
        att = jnp.einsum('bhtk,bhsk,bhtsk->bhts', qt, kt, decay)
        intra = jnp.einsum('bhts,bhsv->bhtv', att, vt)
        b_last = bt[:, :, -1:, :]
        new_state = (jnp.exp(b_last[:, :, 0, :])[..., None] * state
                     + jnp.einsum('bhsk,bhsv->bhkv', kt * jnp.exp(b_last - bt), vt))
        return new_state, inter + intra

    s_init = jnp.zeros((b, h, dk, dv), jnp.float32)
    _, o = lax.scan(step, s_init, (qc, kc, vc, bc))
    return jnp.transpose(o, (1, 0, 3, 2, 4)).reshape(b, s, h, dv)


def memory_cross_attention(x, mem, wq, wk, wv, wo):
    b, s, d = x.shape
    m = mem.shape[1]
    q = (x @ wq).reshape(b, s, XA_HEADS, XA_HEAD_DIM)
    k = (mem @ wk).reshape(b, m, XA_HEADS, XA_HEAD_DIM)
    v = (mem @ wv).reshape(b, m, XA_HEADS, XA_HEAD_DIM)
    sc = jnp.einsum('bshd,bmhd->bhsm', q, k).astype(jnp.float32) * (XA_HEAD_DIM ** -0.5)
    p = jax.nn.softmax(sc, axis=-1).astype(v.dtype)
    o = jnp.einsum('bhsm,bmhd->bshd', p, v).reshape(b, s, d)
    return o @ wo


def hierarchical_moe(x, w_group, b_group, w_expert, b_expert, w_gate, w_up, w_down):
    b, s, d = x.shape
    xt = x.reshape(-1, d)
    n_tok = xt.shape[0]
    g_prob = jax.nn.softmax((xt @ w_group + b_group).astype(jnp.float32), axis=-1)
    g_w, g_idx = lax.top_k(g_prob, 1)
    e_logits = (xt @ w_expert + b_expert).astype(jnp.float32).reshape(n_tok, MOE_GROUPS, MOE_EXPERTS_PER_GROUP)
    e_logits = e_logits[jnp.arange(n_tok), g_idx[:, 0]]
    top_p, top_i = lax.top_k(jax.nn.softmax(e_logits, axis=-1), MOE_TOP_K)
    top_p = top_p / jnp.sum(top_p, axis=-1, keepdims=True)
    weights = (g_w * top_p).reshape(-1)
    experts = (g_idx * MOE_EXPERTS_PER_GROUP + top_i).reshape(-1)
    tokens = jnp.repeat(jnp.arange(n_tok, dtype=jnp.int32), MOE_TOP_K)
    n_assign = n_tok * MOE_TOP_K
    order = jnp.argsort(experts)
    e_sorted, tok_sorted, w_sorted = experts[order], tokens[order], weights[order]
    counts = jnp.bincount(experts, length=MOE_N_EXPERTS)
    padded = ((counts + MOE_BLOCK - 1) // MOE_BLOCK) * MOE_BLOCK
    start = jnp.cumsum(counts) - counts
    pend = jnp.cumsum(padded)
    pstart = pend - padded
    dest = pstart[e_sorted] + (jnp.arange(n_assign) - start[e_sorted])
    cap = n_assign + MOE_N_EXPERTS * MOE_BLOCK
    n_blocks = cap // MOE_BLOCK
    slot_tok = jnp.full((cap,), n_tok, jnp.int32).at[dest].set(tok_sorted)
    slot_w = jnp.zeros((cap,), jnp.float32).at[dest].set(w_sorted)
    block_expert = jnp.minimum(jnp.searchsorted(pend, jnp.arange(n_blocks) * MOE_BLOCK, side='right'),
                               MOE_N_EXPERTS - 1)
    x_pad = jnp.concatenate([xt, jnp.zeros((1, d), xt.dtype)], axis=0)
    xb = x_pad[slot_tok].reshape(n_blocks, MOE_BLOCK, d)

    def run_block(args):
        xblk, e = args
        hid = jax.nn.silu(xblk @ w_gate[e]) * (xblk @ w_up[e])
        return hid @ w_down[e]

    yb = lax.map(run_block, (xb, block_expert)).reshape(cap, d)
    y = yb * slot_w[:, None].astype(yb.dtype)
    out = jax.ops.segment_sum(y, slot_tok, num_segments=n_tok + 1)[:n_tok]
    return out.reshape(b, s, d)


def setup_inputs(seed: int = 0) -> dict:
    key = jax.random.key(seed)
    ks = jax.random.split(key, 32)
    f32 = jnp.float32
    nrm = lambda k, shape, sc: jax.random.normal(k, shape, f32) * sc
    beta = DEEPNORM_BETA
    col_scale = jnp.concatenate([jnp.full((n,), beta if v else 1.0, f32) for n, v in zip(IN_SPLITS, VALUE_SPLITS)])
    ln_g = lambda k: 1.0 + nrm(k, (DEPTH, D_MODEL), 0.02)
    ln_b = lambda k: nrm(k, (DEPTH, D_MODEL), 0.02)
    return {
        "x": nrm(ks[0], (BATCH, SEQ, D_MODEL), 1.0),
        "mem": nrm(ks[1], (BATCH, N_MEM, D_MODEL), 1.0),
        "w_in": nrm(ks[2], (DEPTH, D_MODEL, IN_COLS), D_MODEL ** -0.5) * col_scale,
        "cmp_pe_k": nrm(ks[3], (DEPTH, CMP_BLOCK, NSA_HEAD_DIM), 0.1),
        "cmp_pe_v": nrm(ks[4], (DEPTH, CMP_BLOCK, NSA_HEAD_DIM), 0.1),
        "cmp_w1_k": nrm(ks[5], (DEPTH, CMP_BLOCK * NSA_HEAD_DIM, CMP_HIDDEN), (CMP_BLOCK * NSA_HEAD_DIM) ** -0.5),
        "cmp_w2_k": nrm(ks[6], (DEPTH, CMP_HIDDEN, NSA_HEAD_DIM), CMP_HIDDEN ** -0.5),
        "cmp_w1_v": nrm(ks[7], (DEPTH, CMP_BLOCK * NSA_HEAD_DIM, CMP_HIDDEN), (CMP_BLOCK * NSA_HEAD_DIM) ** -0.5),
        "cmp_w2_v": nrm(ks[8], (DEPTH, CMP_HIDDEN, NSA_HEAD_DIM), CMP_HIDDEN ** -0.5),
        "nsa_norm_g": 1.0 + nrm(ks[9], (DEPTH, NSA_WIDTH), 0.02),
        "hg_lb_logits": nrm(ks[10], (DEPTH + 1, HG_HEADS * HG_KEY_DIM), 0.5),
        "hg_norm_g": 1.0 + nrm(ks[11], (DEPTH, HG_VAL_DIM), 0.02),
        "w_out": nrm(ks[12], (DEPTH, D_MODEL, D_MODEL), D_MODEL ** -0.5 * beta),
        "ln1_g": ln_g(ks[13]),
        "ln1_b": ln_b(ks[14]),
        "xa_wq": nrm(ks[15], (DEPTH, D_MODEL, D_MODEL), D_MODEL ** -0.5),
        "xa_wk": nrm(ks[16], (DEPTH, D_MODEL, D_MODEL), D_MODEL ** -0.5),
        "xa_wv": nrm(ks[17], (DEPTH, D_MODEL, D_MODEL), D_MODEL ** -0.5 * beta),
        "xa_wo": nrm(ks[18], (DEPTH, D_MODEL, D_MODEL), D_MODEL ** -0.5 * beta),
        "ln2_g": ln_g(ks[19]),
        "ln2_b": ln_b(ks[20]),
        "moe_w_group": nrm(ks[21], (DEPTH, D_MODEL, MOE_GROUPS), D_MODEL ** -0.5),
        "moe_b_group": nrm(ks[22], (DEPTH, MOE_GROUPS), 0.01),
        "moe_w_expert": nrm(ks[23], (DEPTH, D_MODEL, MOE_N_EXPERTS), D_MODEL ** -0.5),
        "moe_b_expert": nrm(ks[24], (DEPTH, MOE_N_EXPERTS), 0.01),
        "moe_w_gate": nrm(ks[25], (DEPTH, MOE_N_EXPERTS, D_MODEL, MOE_D_FF), D_MODEL ** -0.5),
        "moe_w_up": nrm(ks[26], (DEPTH, MOE_N_EXPERTS, D_MODEL, MOE_D_FF), D_MODEL ** -0.5 * beta),
        "moe_w_down": nrm(ks[27], (DEPTH, MOE_N_EXPERTS, MOE_D_FF, D_MODEL), MOE_D_FF ** -0.5 * beta),
        "ln3_g": ln_g(ks[28]),
        "ln3_b": ln_b(ks[29]),
    }


def reference(x, mem, w_in, cmp_pe_k, cmp_pe_v, cmp_w1_k, cmp_w2_k, cmp_w1_v, cmp_w2_v, nsa_norm_g,
              hg_lb_logits, hg_norm_g, w_out, ln1_g, ln1_b, xa_wq, xa_wk, xa_wv, xa_wo, ln2_g, ln2_b,
              moe_w_group, moe_b_group, moe_w_expert, moe_b_expert, moe_w_gate, moe_w_up, moe_w_down,
              ln3_g, ln3_b):
    b, s, _ = x.shape
    lb_all = jnp.cumsum(jax.nn.softmax(hg_lb_logits.astype(jnp.float32), axis=0), axis=0)
    for l in range(DEPTH):
        h = x @ w_in[l]
        (nq, nkc, nvc, nks, nvs, nkw, nvw, ngate, hq, hf, hi, hgate) = split_cols(h)
        kv = lambda a: a.reshape(b, s, NSA_KV_GROUPS, NSA_HEAD_DIM)
        kc = nsa_compress(kv(nkc), cmp_pe_k[l], cmp_w1_k[l], cmp_w2_k[l])
        vc = nsa_compress(kv(nvc), cmp_pe_v[l], cmp_w1_v[l], cmp_w2_v[l])
        tr = lambda a: jnp.transpose(kv(a), (0, 2, 1, 3))
        q_nsa = jnp.transpose(nq.reshape(b, s, NSA_HEADS, NSA_HEAD_DIM), (0, 2, 1, 3))
        gates = jnp.transpose(jax.nn.sigmoid(ngate.reshape(b, s, NSA_HEADS, 3)), (0, 2, 1, 3))
        o_nsa = nsa_attention(q_nsa, kc, vc, tr(nks), tr(nvs), tr(nkw), tr(nvw), gates)
        o_nsa = rms_norm(o_nsa, nsa_norm_g[l])
        o_hg = hgrn2_recurrence(hq.reshape(b, s, HG_HEADS, HG_KEY_DIM),
                                hf.reshape(b, s, HG_HEADS, HG_KEY_DIM),
                                hi.reshape(b, s, HG_HEADS, HG_VAL_DIM), lb_all[l])
        o_hg = rms_norm(o_hg, hg_norm_g[l]) * jax.nn.silu(
            hgate.reshape(b, s, HG_HEADS, HG_VAL_DIM).astype(jnp.float32))
        o_hg = o_hg.reshape(b, s, HG_WIDTH)
        mix = jnp.concatenate([o_nsa.astype(x.dtype), o_hg.astype(x.dtype)], axis=-1) @ w_out[l]
        x = layer_norm(DEEPNORM_ALPHA * x + mix, ln1_g[l], ln1_b[l])
        xa = memory_cross_attention(x, mem, xa_wq[l], xa_wk[l], xa_wv[l], xa_wo[l])
        x = layer_norm(DEEPNORM_ALPHA * x + xa, ln2_g[l], ln2_b[l])
        ff = hierarchical_moe(x, moe_w_group[l], moe_b_group[l], moe_w_expert[l], moe_b_expert[l],
                              moe_w_gate[l], moe_w_up[l], moe_w_down[l])
        x = layer_norm(DEEPNORM_ALPHA * x + ff, ln3_g[l], ln3_b[l])
    return x
```

```python
import functools

import jax
import jax.numpy as jnp
from jax import lax
from jax.experimental import pallas as pl
from jax.experimental.pallas import tpu as pltpu

F32 = jnp.float32
BF16 = jnp.bfloat16

NSA_HEAD_DIM = 64
NSA_HEADS = 8
NSA_KV_GROUPS = 2
NSA_HPG = NSA_HEADS // NSA_KV_GROUPS
CMP_BLOCK = 32
CMP_STRIDE = 16
CMP_HIDDEN = 256
SEL_BLOCK = 64
SEL_TOP_N = 16
WINDOW = 512
FORCE_BONUS = 1.0e4
HG_KEY_DIM = 128
HG_VAL_DIM = 128
HG_HEADS = 4
XA_HEADS = 4
MOE_GROUPS = 4
MOE_EXPERTS_PER_GROUP = 8
MOE_N_EXPERTS = MOE_GROUPS * MOE_EXPERTS_PER_GROUP
MOE_TOP_K = 2
MOE_BLOCK = 128
DEPTH = 1
DEEPNORM_ALPHA = (2.0 * DEPTH) ** 0.25
LN_EPS = 1e-5
RMS_EPS = 1e-6
NEG_INF = -1e30

NSA_Q_COLS = NSA_HEADS * NSA_HEAD_DIM
NSA_KV_COLS = NSA_KV_GROUPS * NSA_HEAD_DIM
NSA_GATE_COLS = NSA_HEADS * 3
HG_COLS = HG_HEADS * HG_KEY_DIM

LANES = 128
Q_TILE = 128
NSA_ROWS = NSA_HPG * Q_TILE
SEL_KV_TILE = 512
HG_CHUNK = 128
VMEM_LIMIT = 48 * 1024 * 1024


def _dot(a, b):
    return jnp.dot(a, b, preferred_element_type=F32)


def _dot_nt(a, b):
    return lax.dot_general(a, b, (((1,), (1,)), ((), ())), preferred_element_type=F32)


def _split2(a):
    hi = a.astype(BF16)
    lo = (a - hi.astype(F32)).astype(BF16)
    return hi, lo


def _split3(a):
    p1 = a.astype(BF16)
    r1 = a - p1.astype(F32)
    p2 = r1.astype(BF16)
    p3 = (r1 - p2.astype(F32)).astype(BF16)
    return p1, p2, p3


def _dot3(a, b):
    ah, al = _split2(a)
    bh, bl = _split2(b)
    return _dot(ah, bh) + _dot(ah, bl) + _dot(al, bh)


def _sigmoid(x):
    return 1.0 / (1.0 + jnp.exp(-x))


def _layer_norm(y, g, b):
    mu = jnp.mean(y, axis=-1, keepdims=True)
    d = y - mu
    var = jnp.mean(d * d, axis=-1, keepdims=True)
    return d * lax.rsqrt(var + LN_EPS) * g + b


def _params(sem):
    return pltpu.CompilerParams(dimension_semantics=sem, vmem_limit_bytes=VMEM_LIMIT)


def _matmul_kernel(x_ref, w_ref, o_ref):
    o_ref[...] = _dot(x_ref[...].astype(BF16), w_ref[...]).astype(o_ref.dtype)


def _matmul(x, w, tm):
    m, k = x.shape
    n = w.shape[1]
    return pl.pallas_call(
        _matmul_kernel,
        grid=(m // tm,),
        in_specs=[pl.BlockSpec((tm, k), lambda i: (i, 0)),
                  pl.BlockSpec((k, n), lambda i: (0, 0))],
        out_specs=pl.BlockSpec((tm, n), lambda i: (i, 0)),
        out_shape=jax.ShapeDtypeStruct((m, n), F32),
        compiler_params=_params(("parallel",)),
        name="proj",
    )(x, w)


def _compress_kernel(ch_ref, pe_ref, w1_ref, w2_ref, o_ref):
    ch = ch_ref[0, 0, 0]
    half = ch.shape[1]
    nch = ch.shape[0]
    pe = pe_ref[0]
    w1 = w1_ref[0]
    top = _dot3(ch + pe[:, :half], w1[:half])
    bot = _dot3(ch + pe[:, half:], w1[half:])
    hid = top + pltpu.roll(bot, nch - 1, 0)
    c = 0.7978845608028654
    act = 0.5 * hid * (1.0 + jnp.tanh(c * (hid + 0.044715 * hid * hid * hid)))
    out = _dot3(act, w2_ref[0])
    row = lax.broadcasted_iota(jnp.int32, out.shape, 0)
    o_ref[0, 0, 0] = jnp.where(row < nch - 1, out, 0.0)


def _compress(ch, pe, w1, w2):
    _, b, g, nch, width = ch.shape
    hidden = w1.shape[-1]
    d = w2.shape[-1]
    return pl.pallas_call(
        _compress_kernel,
        grid=(2, b, g),
        in_specs=[pl.BlockSpec((1, 1, 1, nch, width), lambda a, i, j: (a, i, j, 0, 0)),
                  pl.BlockSpec((1, 1, 2 * width), lambda a, i, j: (a, 0, 0)),
                  pl.BlockSpec((1, 2 * width, hidden), lambda a, i, j: (a, 0, 0)),
                  pl.BlockSpec((1, hidden, d), lambda a, i, j: (a, 0, 0))],
        out_specs=pl.BlockSpec((1, 1, 1, nch, d), lambda a, i, j: (a, i, j, 0, 0)),
        out_shape=jax.ShapeDtypeStruct((2, b, g, nch, d), F32),
        compiler_params=_params(("parallel", "parallel", "parallel")),
        name="compress",
    )(ch, pe, w1, w2)


def _nsa_kernel(qT_ref, kc_ref, vcT_ref, msel_ref, ks_ref, vsT_ref, kw_ref, vwT_ref, gT_ref,
                o_ref, sel_ref, *, seq):
    s0 = pl.program_id(2) * Q_TILE
    nsel = seq // SEL_BLOCK
    ncmp_pad = seq // CMP_STRIDE
    scale = NSA_HEAD_DIM ** -0.5

    q = qT_ref[0, 0, 0] * scale
    q_bf = q.astype(BF16)
    t_q = s0 + lax.broadcasted_iota(jnp.int32, (1, Q_TILE), 1)
    t_all = jnp.concatenate([t_q] * NSA_HPG, axis=1)

    sc = _dot3(kc_ref[0, 0], q)
    n_idx = lax.broadcasted_iota(jnp.int32, (ncmp_pad, 1), 0)
    ok_c = (n_idx * CMP_STRIDE + (CMP_BLOCK - 1)) <= t_all
    sc = jnp.where(ok_c, sc, NEG_INF)
    m_c = jnp.max(sc, axis=0, keepdims=True)
    e_c = jnp.where(ok_c, jnp.exp(sc - m_c), 0.0)
    l_c = jnp.sum(e_c, axis=0, keepdims=True)
    p_c = e_c * (1.0 / jnp.where(l_c > 0.0, l_c, 1.0))
    o_c = _dot(vcT_ref[0, 0].astype(BF16), p_c.astype(BF16))

    p_sum = p_c[:, 0:Q_TILE]
    for r in range(1, NSA_HPG):
        p_sum = p_sum + p_c[:, r * Q_TILE:(r + 1) * Q_TILE]
    p1, p2, p3 = _split3(p_sum)
    msel = msel_ref[...]
    imp = _dot(msel, p1) + _dot(msel, p2) + _dot(msel, p3)
    j_idx = lax.broadcasted_iota(jnp.int32, (nsel, Q_TILE), 0)
    cur = lax.shift_right_logical(t_q, SEL_BLOCK.bit_length() - 1)
    forced = (j_idx == 0) | (j_idx == cur) | (j_idx == cur - 1)
    score = jnp.where(j_idx <= cur, imp + jnp.where(forced, FORCE_BONUS, 0.0), -1.0)
    j_f = j_idx.astype(F32)
    sel = jnp.zeros((nsel, Q_TILE), F32)
    for _ in range(min(SEL_TOP_N, nsel)):
        best = jnp.max(score, axis=0, keepdims=True)
        first = jnp.min(jnp.where(score == best, j_f, float(nsel)), axis=0, keepdims=True)
        pick = j_f == first
        sel = jnp.where(pick, 1.0, sel)
        score = jnp.where(pick, -3.0e38, score)
    sel_ref[...] = sel

    blocks_per_tile = SEL_KV_TILE // SEL_BLOCK
    n_tiles = lax.div(s0 + (Q_TILE + SEL_KV_TILE - 1), SEL_KV_TILE)
    k_off = lax.broadcasted_iota(jnp.int32, (SEL_KV_TILE, 1), 0)

    def sel_step(j, carry):
        m_i, l_i, acc = carry
        base = pl.multiple_of(j * SEL_KV_TILE, SEL_KV_TILE)
        s = _dot(ks_ref[0, 0, pl.ds(base, SEL_KV_TILE), :], q_bf)
        member = jnp.concatenate(
            [jnp.broadcast_to(sel_ref[pl.ds(j * blocks_per_tile + b, 1), :], (SEL_BLOCK, Q_TILE))
             for b in range(blocks_per_tile)], axis=0)
        ok = (member > 0.5) & ((base + k_off) <= t_q)
        bias = jnp.where(ok, 0.0, NEG_INF)
        s = s + jnp.concatenate([bias] * NSA_HPG, axis=1)
        m_new = jnp.maximum(m_i, jnp.max(s, axis=0, keepdims=True))
        alpha = jnp.exp(m_i - m_new)
        p = jnp.exp(s - m_new)
        l_new = alpha * l_i + jnp.sum(p, axis=0, keepdims=True)
        pv = _dot(vsT_ref[0, 0, :, pl.ds(base, SEL_KV_TILE)], p.astype(BF16))
        return m_new, l_new, alpha * acc + pv

    init = (jnp.full((1, NSA_ROWS), NEG_INF, F32), jnp.zeros((1, NSA_ROWS), F32),
            jnp.zeros((NSA_HEAD_DIM, NSA_ROWS), F32))
    _, l_s, acc_s = lax.fori_loop(0, n_tiles, sel_step, init)
    o_s = acc_s * (1.0 / l_s)

    span = WINDOW + Q_TILE
    lo = pl.multiple_of(jnp.maximum(s0 - WINDOW, 0), Q_TILE)
    kpos = lo + lax.broadcasted_iota(jnp.int32, (span, 1), 0)
    sw = _dot(kw_ref[0, 0, pl.ds(lo, span), :], q_bf)
    ok_w = (kpos <= t_all) & (kpos > t_all - WINDOW)
    sw = jnp.where(ok_w, sw, NEG_INF)
    m_w = jnp.max(sw, axis=0, keepdims=True)
    e_w = jnp.where(ok_w, jnp.exp(sw - m_w), 0.0)
    l_w = jnp.sum(e_w, axis=0, keepdims=True)
    o_w = _dot(vwT_ref[0, 0, :, pl.ds(lo, span)], e_w.astype(BF16)) * (1.0 / l_w)

    g = _sigmoid(gT_ref[0, 0, 0])
    o_ref[0, 0, 0] = g[0:1] * o_c + g[1:2] * o_s + g[2:3] * o_w


def _nsa(qT, kc, vcT, msel, ks, vsT, kw, vwT, gT, seq):
    b, g, nqb = qT.shape[:3]
    d = NSA_HEAD_DIM
    ncp = seq // CMP_STRIDE
    nsel = seq // SEL_BLOCK
    per_bg = lambda i, j, k: (i, j, 0, 0)
    per_q = lambda i, j, k: (i, j, k, 0, 0)
    return pl.pallas_call(
        functools.partial(_nsa_kernel, seq=seq),
        grid=(b, g, nqb),
        in_specs=[pl.BlockSpec((1, 1, 1, d, NSA_ROWS), per_q),
                  pl.BlockSpec((1, 1, ncp, d), per_bg),
                  pl.BlockSpec((1, 1, d, ncp), per_bg),
                  pl.BlockSpec((nsel, ncp), lambda i, j, k: (0, 0)),
                  pl.BlockSpec((1, 1, seq, d), per_bg),
                  pl.BlockSpec((1, 1, d, seq), per_bg),
                  pl.BlockSpec((1, 1, seq, d), per_bg),
                  pl.BlockSpec((1, 1, d, seq), per_bg),
                  pl.BlockSpec((1, 1, 1, 3, NSA_ROWS), per_q)],
        out_specs=pl.BlockSpec((1, 1, 1, d, NSA_ROWS), per_q),
        out_shape=jax.ShapeDtypeStruct((b, g, nqb, d, NSA_ROWS), F32),
        scratch_shapes=[pltpu.VMEM((nsel, Q_TILE), F32)],
        compiler_params=_params(("parallel", "parallel", "arbitrary")),
        name="nsa",
    )(qT, kc, vcT, msel, ks, vsT, kw, vwT, gT)


def _hgrn_kernel(qT_ref, fT_ref, vT_ref, gateT_ref, lb_ref, ng_ref, o_ref, state_ref, att_ref):
    c = HG_CHUNK

    @pl.when(pl.program_id(2) == 0)
    def _():
        state_ref[...] = jnp.zeros_like(state_ref)

    lb = lb_ref[0]
    f = fT_ref[0, 0]
    log_f = jnp.log(lb + (1.0 - lb) * _sigmoid(f))
    kk = (1.0 - lb) * _sigmoid(-f)
    row = lax.broadcasted_iota(jnp.int32, (c, c), 0)
    col = lax.broadcasted_iota(jnp.int32, (c, c), 1)
    upper = jnp.where(row <= col, 1.0, 0.0).astype(BF16)
    ones = jnp.ones((HG_VAL_DIM, c), BF16)
    l1, l2, l3 = _split3(log_f)
    bt = _dot(l1, upper) + _dot(l2, upper) + _dot(l3, upper)
    b_last = _dot_nt(ones, l1) + _dot_nt(ones, l2) + _dot_nt(ones, l3)

    qT = qT_ref[0, 0]
    vT = vT_ref[0, 0].astype(BF16)
    for s in range(c):
        decay = jnp.exp(jnp.minimum(bt - bt[:, s:s + 1], 0.0))
        att_ref[s:s + 1, :] = jnp.sum((qT * kk[:, s:s + 1]) * decay, axis=0, keepdims=True)
    attT = jnp.where(row <= col, att_ref[...], 0.0)

    state = state_ref[...]
    inter = _dot(state.astype(BF16), (qT * jnp.exp(bt)).astype(BF16))
    intra = _dot(vT, attT.astype(BF16))
    k_dec = kk * jnp.exp(bt[:, c - 1:c] - bt)
    state_ref[...] = state * jnp.exp(b_last) + _dot_nt(vT, k_dec.astype(BF16))

    o = inter + intra
    ms = jnp.mean(o * o, axis=0, keepdims=True)
    gate = gateT_ref[0, 0]
    o_ref[0, 0] = o * lax.rsqrt(ms + RMS_EPS) * ng_ref[...] * (gate * _sigmoid(gate))


def _hgrn(qT, fT, vT, gateT, lb, ng):
    b, h, dk, seq = qT.shape
    dv = vT.shape[2]
    blk = lambda d: pl.BlockSpec((1, 1, d, HG_CHUNK), lambda i, j, k: (i, j, 0, k))
    return pl.pallas_call(
        _hgrn_kernel,
        grid=(b, h, seq // HG_CHUNK),
        in_specs=[blk(dk), blk(dk), blk(dv), blk(dv),
                  pl.BlockSpec((1, dk, 1), lambda i, j, k: (j, 0, 0)),
                  pl.BlockSpec((dv, 1), lambda i, j, k: (0, 0))],
        out_specs=blk(dv),
        out_shape=jax.ShapeDtypeStruct((b, h, dv, seq), F32),
        scratch_shapes=[pltpu.VMEM((dv, dk), F32), pltpu.VMEM((HG_CHUNK, HG_CHUNK), F32)],
        compiler_params=_params(("parallel", "parallel", "arbitrary")),
        name="hgrn",
    )(qT, fT, vT, gateT, lb, ng)


def _mix_kernel(nsa_ref, hg_ref, x_ref, w_ref, ng_ref, g_ref, b_ref, o_ref):
    o_n = nsa_ref[...]
    half = o_n.shape[1]
    o_n = o_n * lax.rsqrt(jnp.mean(o_n * o_n, axis=-1, keepdims=True) + RMS_EPS) * ng_ref[...]
    mix = _dot(o_n.astype(BF16), w_ref[:half]) + _dot(hg_ref[...].astype(BF16), w_ref[half:])
    o_ref[...] = _layer_norm(DEEPNORM_ALPHA * x_ref[...] + mix, g_ref[...], b_ref[...])


def _mix(o_nsa, o_hg, x, w_out, ng, g, b, tm):
    t, d = x.shape
    half = o_nsa.shape[1]
    rows = lambda w: pl.BlockSpec((tm, w), lambda i: (i, 0))
    full = lambda r, c: pl.BlockSpec((r, c), lambda i: (0, 0))
    return pl.pallas_call(
        _mix_kernel,
        grid=(t // tm,),
        in_specs=[rows(half), rows(half), rows(d), full(d, d), full(1, half), full(1, d), full(1, d)],
        out_specs=rows(d),
        out_shape=jax.ShapeDtypeStruct((t, d), F32),
        compiler_params=_params(("parallel",)),
        name="mix",
    )(o_nsa, o_hg, x, w_out, ng, g, b)


def _xattn_kernel(x_ref, wq_ref, kT_ref, v_ref, wo_ref, g_ref, b_ref, o_ref):
    x = x_ref[0]
    d = x.shape[1]
    dh = d // XA_HEADS
    q = _dot(x.astype(BF16), wq_ref[...])
    heads = []
    for h in range(XA_HEADS):
        cols = slice(h * dh, (h + 1) * dh)
        s = _dot(q[:, cols].astype(BF16), kT_ref[0, cols, :]) * (dh ** -0.5)
        e = jnp.exp(s - jnp.max(s, axis=-1, keepdims=True))
        p = e * (1.0 / jnp.sum(e, axis=-1, keepdims=True))
        heads.append(_dot(p.astype(BF16), v_ref[0, :, cols]))
    o = jnp.concatenate(heads, axis=1)
    xa = _dot(o.astype(BF16), wo_ref[...])
    o_ref[0] = _layer_norm(DEEPNORM_ALPHA * x + xa, g_ref[...], b_ref[...])


def _xattn(x, wq, kT, v, wo, g, b, tm):
    bsz, seq, d = x.shape
    m = v.shape[1]
    full = lambda r, c: pl.BlockSpec((r, c), lambda i, j: (0, 0))
    return pl.pallas_call(
        _xattn_kernel,
        grid=(bsz, seq // tm),
        in_specs=[pl.BlockSpec((1, tm, d), lambda i, j: (i, j, 0)),
                  full(d, d),
                  pl.BlockSpec((1, d, m), lambda i, j: (i, 0, 0)),
                  pl.BlockSpec((1, m, d), lambda i, j: (i, 0, 0)),
                  full(d, d), full(1, d), full(1, d)],
        out_specs=pl.BlockSpec((1, tm, d), lambda i, j: (i, j, 0)),
        out_shape=jax.ShapeDtypeStruct((bsz, seq, d), F32),
        compiler_params=_params(("parallel", "parallel")),
        name="xattn",
    )(x, wq, kT, v, wo, g, b)


def _route_kernel(x_ref, wh_ref, wl_ref, bias_ref, ids_ref, wts_ref):
    xh, xl = _split2(x_ref[...])
    logits = _dot(xh, wh_ref[...]) + _dot(xh, wl_ref[...]) + _dot(xl, wh_ref[...]) + bias_ref[...]
    lane = lax.broadcasted_iota(jnp.int32, logits.shape, 1)
    lane_f = lane.astype(F32)
    big = float(LANES)

    is_g = lane < MOE_GROUPS
    g_max = jnp.max(jnp.where(is_g, logits, NEG_INF), axis=-1, keepdims=True)
    g_sum = jnp.sum(jnp.where(is_g, jnp.exp(logits - g_max), 0.0), axis=-1, keepdims=True)
    g_w = 1.0 / g_sum
    g_idx = jnp.min(jnp.where(is_g & (logits == g_max), lane_f, big), axis=-1, keepdims=True)

    e_lo = MOE_GROUPS + MOE_EXPERTS_PER_GROUP * g_idx
    is_e = (lane_f >= e_lo) & (lane_f < e_lo + MOE_EXPERTS_PER_GROUP)
    e_log = jnp.where(is_e, logits, NEG_INF)
    e_max = jnp.max(e_log, axis=-1, keepdims=True)
    e_exp = jnp.where(is_e, jnp.exp(logits - e_max), 0.0)
    e_sum = jnp.sum(e_exp, axis=-1, keepdims=True)
    i1 = jnp.min(jnp.where(is_e & (e_log == e_max), lane_f, big), axis=-1, keepdims=True)
    rest = jnp.where(lane_f == i1, NEG_INF, e_log)
    r_max = jnp.max(rest, axis=-1, keepdims=True)
    i2 = jnp.min(jnp.where(is_e & (lane_f != i1) & (rest == r_max), lane_f, big), axis=-1, keepdims=True)
    p1 = 1.0 / e_sum
    p2 = jnp.exp(r_max - e_max) / e_sum
    tot = p1 + p2
    w1 = g_w * (p1 / tot)
    w2 = g_w * (p2 / tot)
    ids = jnp.where(lane == 0, i1, i2) - float(MOE_GROUPS)
    ids_ref[...] = ids.astype(jnp.int32)
    wts_ref[...] = jnp.where(lane == 0, w1, w2)


def _route(x, wh, wl, bias, tm):
    t, d = x.shape
    rows = lambda w: pl.BlockSpec((tm, w), lambda i: (i, 0))
    full = lambda r, c: pl.BlockSpec((r, c), lambda i: (0, 0))
    return pl.pallas_call(
        _route_kernel,
        grid=(t // tm,),
        in_specs=[rows(d), full(d, LANES), full(d, LANES), full(1, LANES)],
        out_specs=[rows(LANES), rows(LANES)],
        out_shape=[jax.ShapeDtypeStruct((t, LANES), jnp.int32), jax.ShapeDtypeStruct((t, LANES), F32)],
        compiler_params=_params(("parallel",)),
        name="route",
    )(x, wh, wl, bias)


def _experts_kernel(code_ref, bexp_ref, nact_ref, x_hbm, wg_ref, wu_ref, wd_ref, sw_ref, out_hbm,
                    xbuf, ybuf, gsem, ssem):
    i = pl.program_id(0)
    base = i * MOE_BLOCK

    def gather(j):
        tok = lax.div(jnp.maximum(code_ref[base + j], 0), MOE_TOP_K)
        return pltpu.make_async_copy(x_hbm.at[pl.ds(tok, 1)], xbuf.at[pl.ds(j, 1)], gsem)

    def scatter(j):
        dst = jnp.maximum(code_ref[base + j], 0)
        return pltpu.make_async_copy(ybuf.at[pl.ds(j, 1)], out_hbm.at[pl.ds(dst, 1)], ssem)

    @pl.when(i < nact_ref[0])
    def _():
        def start_gather(j, c):
            gather(j).start()
            return c

        def wait_gather(j, c):
            gather(j).wait()
            return c

        lax.fori_loop(0, MOE_BLOCK, start_gather, 0)
        lax.fori_loop(0, MOE_BLOCK, wait_gather, 0)
        xb = xbuf[...].astype(BF16)
        gate = _dot(xb, wg_ref[0])
        hid = (gate * _sigmoid(gate)) * _dot(xb, wu_ref[0])
        ybuf[...] = _dot(hid.astype(BF16), wd_ref[0]) * sw_ref[...]

        def start_scatter(j, c):
            @pl.when(code_ref[base + j] >= 0)
            def _():
                scatter(j).start()
            return c

        def wait_scatter(j, c):
            @pl.when(code_ref[base + j] >= 0)
            def _():
                scatter(j).wait()
            return c

        lax.fori_loop(0, MOE_BLOCK, start_scatter, 0)
        lax.fori_loop(0, MOE_BLOCK, wait_scatter, 0)


def _experts(code, bexp, nact, x, wg, wu, wd, slot_w):
    t, d = x.shape
    dff = wg.shape[-1]
    n_blocks = bexp.shape[0]
    by_expert = lambda i, code, bexp, nact: (bexp[i], 0, 0)
    grid_spec = pltpu.PrefetchScalarGridSpec(
        num_scalar_prefetch=3,
        grid=(n_blocks,),
        in_specs=[pl.BlockSpec(memory_space=pl.ANY),
                  pl.BlockSpec((1, d, dff), by_expert),
                  pl.BlockSpec((1, d, dff), by_expert),
                  pl.BlockSpec((1, dff, d), by_expert),
                  pl.BlockSpec((MOE_BLOCK, 1), lambda i, code, bexp, nact: (i, 0))],
        out_specs=pl.BlockSpec(memory_space=pl.ANY),
        scratch_shapes=[pltpu.VMEM((MOE_BLOCK, d), F32), pltpu.VMEM((MOE_BLOCK, d), F32),
                        pltpu.SemaphoreType.DMA, pltpu.SemaphoreType.DMA])
    return pl.pallas_call(
        _experts_kernel,
        grid_spec=grid_spec,
        out_shape=jax.ShapeDtypeStruct((t * MOE_TOP_K, d), F32),
        compiler_params=_params(("arbitrary",)),
        name="experts",
    )(code, bexp, nact, x, wg, wu, wd, slot_w)


def _combine_kernel(y_ref, x_ref, g_ref, b_ref, o_ref):
    d = x_ref.shape[1]
    ff = y_ref[:, :d] + y_ref[:, d:]
    o_ref[...] = _layer_norm(DEEPNORM_ALPHA * x_ref[...] + ff, g_ref[...], b_ref[...])


def _combine(y2, x, g, b, tm):
    t, d = x.shape
    return pl.pallas_call(
        _combine_kernel,
        grid=(t // tm,),
        in_specs=[pl.BlockSpec((tm, MOE_TOP_K * d), lambda i: (i, 0)),
                  pl.BlockSpec((tm, d), lambda i: (i, 0)),
                  pl.BlockSpec((1, d), lambda i: (0, 0)),
                  pl.BlockSpec((1, d), lambda i: (0, 0))],
        out_specs=pl.BlockSpec((tm, d), lambda i: (i, 0)),
        out_shape=jax.ShapeDtypeStruct((t, d), F32),
        compiler_params=_params(("parallel",)),
        name="combine",
    )(y2, x, g, b)


def _selection_matrix(nsel, ncmp_pad):
    ratio = SEL_BLOCK // CMP_STRIDE
    n_sub = CMP_BLOCK // CMP_STRIDE
    j = jnp.arange(nsel)[:, None]
    n = jnp.arange(ncmp_pad)[None, :]
    hit = (n >= ratio * j - (n_sub - 1)) & (n <= ratio * j + ratio - 1) & (n < ncmp_pad - (n_sub - 1))
    return hit.astype(BF16)


def _dispatch_plan(ids, wts, n_tok):
    n_assign = n_tok * MOE_TOP_K
    experts = ids.reshape(-1)
    onehot = (experts[:, None] == jnp.arange(MOE_N_EXPERTS)[None, :]).astype(jnp.int32)
    rank = jnp.sum((jnp.cumsum(onehot, axis=0) - onehot) * onehot, axis=1)
    counts = jnp.sum(onehot, axis=0)
    padded = ((counts + MOE_BLOCK - 1) // MOE_BLOCK) * MOE_BLOCK
    pend = jnp.cumsum(padded)
    pstart = pend - padded
    dest = pstart[experts] + rank
    cap = n_assign + MOE_N_EXPERTS * MOE_BLOCK
    n_blocks = cap // MOE_BLOCK
    code = jnp.full((cap,), -1, jnp.int32).at[dest].set(jnp.arange(n_assign, dtype=jnp.int32))
    slot_w = jnp.zeros((cap,), F32).at[dest].set(wts.reshape(-1))
    bexp = jnp.minimum(jnp.searchsorted(pend, jnp.arange(n_blocks) * MOE_BLOCK, side='right'),
                       MOE_N_EXPERTS - 1).astype(jnp.int32)
    nact = (pend[-1] // MOE_BLOCK).astype(jnp.int32).reshape(1)
    return code, bexp, nact, slot_w.reshape(cap, 1)


def kernel(x, mem, w_in, cmp_pe_k, cmp_pe_v, cmp_w1_k, cmp_w2_k, cmp_w1_v, cmp_w2_v, nsa_norm_g,
           hg_lb_logits, hg_norm_g, w_out, ln1_g, ln1_b, xa_wq, xa_wk, xa_wv, xa_wo, ln2_g, ln2_b,
           moe_w_group, moe_b_group, moe_w_expert, moe_b_expert, moe_w_gate, moe_w_up, moe_w_down,
           ln3_g, ln3_b):
    b, s, d = x.shape
    t = b * s
    g, dh, hpg = NSA_KV_GROUPS, NSA_HEAD_DIM, NSA_HPG
    nqb = s // Q_TILE
    nch = s // CMP_STRIDE
    row = lambda a: a.reshape(1, -1)
    lb_all = jnp.cumsum(jax.nn.softmax(hg_lb_logits.astype(F32), axis=0), axis=0)
    xt = x.reshape(t, d)
    for l in range(DEPTH):
        h = _matmul(xt, w_in[l].astype(BF16), 256)
        offs = [0]
        for n in (NSA_Q_COLS,) + (NSA_KV_COLS,) * 6 + (NSA_GATE_COLS,) + (HG_COLS,) * 4:
            offs.append(offs[-1] + n)
        (nq, nkc, nvc, nks, nvs, nkw, nvw, ngate, hq, hf, hi, hgate) = [
            h[:, offs[i]:offs[i + 1]] for i in range(12)]

        chunks = lambda a: jnp.transpose(a.reshape(b, nch, CMP_STRIDE, g, dh), (0, 3, 1, 2, 4)).reshape(
            b, g, nch, CMP_STRIDE * dh)
        comp = _compress(jnp.stack([chunks(nkc), chunks(nvc)]),
                         jnp.stack([cmp_pe_k[l].reshape(1, -1), cmp_pe_v[l].reshape(1, -1)]),
                         jnp.stack([cmp_w1_k[l], cmp_w1_v[l]]),
                         jnp.stack([cmp_w2_k[l], cmp_w2_v[l]]))
        kc, vcT = comp[0], jnp.swapaxes(comp[1], -1, -2)

        keys = lambda a: jnp.transpose(a.reshape(b, s, g, dh), (0, 2, 1, 3)).astype(BF16)
        valsT = lambda a: jnp.transpose(a.reshape(b, s, g, dh), (0, 2, 3, 1)).astype(BF16)
        qT = jnp.transpose(nq.reshape(b, nqb, Q_TILE, g, hpg, dh), (0, 3, 1, 5, 4, 2)).reshape(
            b, g, nqb, dh, NSA_ROWS)
        gT = jnp.transpose(ngate.reshape(b, nqb, Q_TILE, g, hpg, 3), (0, 3, 1, 5, 4, 2)).reshape(
            b, g, nqb, 3, NSA_ROWS)
        msel = _selection_matrix(s // SEL_BLOCK, nch)
        oT = _nsa(qT, kc, vcT, msel, keys(nks), valsT(nvs), keys(nkw), valsT(nvw), gT, s)
        o_nsa = jnp.transpose(oT.reshape(b, g, nqb, dh, hpg, Q_TILE), (0, 2, 5, 1, 4, 3)).reshape(
            t, NSA_Q_COLS)

        headsT = lambda a: jnp.transpose(a.reshape(b, s, HG_HEADS, -1), (0, 2, 3, 1))
        o_hgT = _hgrn(headsT(hq), headsT(hf), headsT(hi), headsT(hgate),
                      lb_all[l].reshape(HG_HEADS, HG_KEY_DIM, 1), hg_norm_g[l].reshape(HG_VAL_DIM, 1))
        o_hg = jnp.transpose(o_hgT, (0, 3, 1, 2)).reshape(t, HG_COLS)

        x1 = _mix(o_nsa, o_hg, xt, w_out[l].astype(BF16), row(nsa_norm_g[l]), row(ln1_g[l]), row(ln1_b[l]), 256)

        n_mem = mem.shape[1]
        kv = _matmul(mem.reshape(b * n_mem, d),
                     jnp.concatenate([xa_wk[l], xa_wv[l]], axis=1).astype(BF16), n_mem)
        kT = jnp.swapaxes(kv[:, :d].reshape(b, n_mem, d), 1, 2).astype(BF16)
        v = kv[:, d:].reshape(b, n_mem, d).astype(BF16)
        x2 = _xattn(x1.reshape(b, s, d), xa_wq[l].astype(BF16), kT, v, xa_wo[l].astype(BF16),
                    row(ln2_g[l]), row(ln2_b[l]), 256).reshape(t, d)

        w_r = jnp.concatenate([moe_w_group[l], moe_w_expert[l]], axis=1)
        w_r = jnp.pad(w_r, ((0, 0), (0, LANES - w_r.shape[1])))
        b_r = jnp.pad(jnp.concatenate([moe_b_group[l], moe_b_expert[l]]), (0, LANES - MOE_GROUPS - MOE_N_EXPERTS))
        w_rh, w_rl = _split2(w_r)
        ids, wts = _route(x2, w_rh, w_rl, row(b_r), 512)
        code, bexp, nact, slot_w = _dispatch_plan(ids[:, :MOE_TOP_K], wts[:, :MOE_TOP_K], t)
        y2 = _experts(code, bexp, nact, x2, moe_w_gate[l].astype(BF16), moe_w_up[l].astype(BF16),
                      moe_w_down[l].astype(BF16), slot_w)
        xt = _combine(y2.reshape(t, MOE_TOP_K * d), x2, row(ln3_g[l]), row(ln3_b[l]), 256)
    return xt.reshape(b, s, d)
```

```python
import functools

import jax
import jax.numpy as jnp
from jax import lax
from jax.experimental import pallas as pl
from jax.experimental.pallas import tpu as pltpu

F32 = jnp.float32
BF16 = jnp.bfloat16

NSA_HEAD_DIM = 64
NSA_HEADS = 8
NSA_KV_GROUPS = 2
NSA_HPG = NSA_HEADS // NSA_KV_GROUPS
CMP_BLOCK = 32
CMP_STRIDE = 16
CMP_HIDDEN = 256
SEL_BLOCK = 64
SEL_TOP_N = 16
WINDOW = 512
FORCE_BONUS = 1.0e4
HG_KEY_DIM = 128
HG_VAL_DIM = 128
HG_HEADS = 4
XA_HEADS = 4
MOE_GROUPS = 4
MOE_EXPERTS_PER_GROUP = 8
MOE_N_EXPERTS = MOE_GROUPS * MOE_EXPERTS_PER_GROUP
MOE_TOP_K = 2
MOE_BLOCK = 128
DEPTH = 1
DEEPNORM_ALPHA = (2.0 * DEPTH) ** 0.25
LN_EPS = 1e-5
RMS_EPS = 1e-6
NEG_INF = -1e30

NSA_Q_COLS = NSA_HEADS * NSA_HEAD_DIM
NSA_KV_COLS = NSA_KV_GROUPS * NSA_HEAD_DIM
NSA_GATE_COLS = NSA_HEADS * 3
HG_COLS = HG_HEADS * HG_KEY_DIM

LANES = 128
Q_TILE = 128
NSA_ROWS = NSA_HPG * Q_TILE
SEL_KV_TILE = 512
SEL_GROUP = 4
NSA_V_ROWS = NSA_HEAD_DIM + 16
LOG2_E = 1.4426950408889634
HG_CHUNK = 128
HG_SUB = 16
HG_ROWS = 512
VMEM_LIMIT = 48 * 1024 * 1024


def _dot(a, b):
    return jnp.dot(a, b, preferred_element_type=F32)


def _dot_nt(a, b):
    return lax.dot_general(a, b, (((1,), (1,)), ((), ())), preferred_element_type=F32)


def _split2(a):
    hi = a.astype(BF16)
    lo = (a - hi.astype(F32)).astype(BF16)
    return hi, lo


def _split3(a):
    p1 = a.astype(BF16)
    r1 = a - p1.astype(F32)
    p2 = r1.astype(BF16)
    p3 = (r1 - p2.astype(F32)).astype(BF16)
    return p1, p2, p3


def _dot3(a, b):
    ah, al = _split2(a)
    bh, bl = _split2(b)
    return _dot(ah, bh) + _dot(ah, bl) + _dot(al, bh)


def _sigmoid(x):
    return 1.0 / (1.0 + jnp.exp(-x))


def _layer_norm(y, g, b):
    mu = jnp.mean(y, axis=-1, keepdims=True)
    d = y - mu
    var = jnp.mean(d * d, axis=-1, keepdims=True)
    return d * lax.rsqrt(var + LN_EPS) * g + b


def _params(sem):
    return pltpu.CompilerParams(dimension_semantics=sem, vmem_limit_bytes=VMEM_LIMIT)


def _matmul_kernel(x_ref, w_ref, o_ref):
    o_ref[...] = _dot(x_ref[...].astype(BF16), w_ref[...]).astype(o_ref.dtype)


def _matmul(x, w, tm):
    m, k = x.shape
    n = w.shape[1]
    return pl.pallas_call(
        _matmul_kernel,
        grid=(m // tm,),
        in_specs=[pl.BlockSpec((tm, k), lambda i: (i, 0)),
                  pl.BlockSpec((k, n), lambda i: (0, 0))],
        out_specs=pl.BlockSpec((tm, n), lambda i: (i, 0)),
        out_shape=jax.ShapeDtypeStruct((m, n), F32),
        compiler_params=_params(("parallel",)),
        name="proj",
    )(x, w)


def _compress_kernel(ch_ref, pe_ref, w1_ref, w2_ref, o_ref):
    ch = ch_ref[0, 0, 0]
    half = ch.shape[1]
    nch = ch.shape[0]
    pe = pe_ref[0]
    w1 = w1_ref[0]
    top = _dot3(ch + pe[:, :half], w1[:half])
    bot = _dot3(ch + pe[:, half:], w1[half:])
    hid = top + pltpu.roll(bot, nch - 1, 0)
    c = 0.7978845608028654
    act = 0.5 * hid * (1.0 + jnp.tanh(c * (hid + 0.044715 * hid * hid * hid)))
    out = _dot3(act, w2_ref[0])
    row = lax.broadcasted_iota(jnp.int32, out.shape, 0)
    o_ref[0, 0, 0] = jnp.where(row < nch - 1, out, 0.0)


def _compress(ch, pe, w1, w2):
    _, b, g, nch, width = ch.shape
    hidden = w1.shape[-1]
    d = w2.shape[-1]
    return pl.pallas_call(
        _compress_kernel,
        grid=(2, b, g),
        in_specs=[pl.BlockSpec((1, 1, 1, nch, width), lambda a, i, j: (a, i, j, 0, 0)),
                  pl.BlockSpec((1, 1, 2 * width), lambda a, i, j: (a, 0, 0)),
                  pl.BlockSpec((1, 2 * width, hidden), lambda a, i, j: (a, 0, 0)),
                  pl.BlockSpec((1, hidden, d), lambda a, i, j: (a, 0, 0))],
        out_specs=pl.BlockSpec((1, 1, 1, nch, d), lambda a, i, j: (a, i, j, 0, 0)),
        out_shape=jax.ShapeDtypeStruct((2, b, g, nch, d), F32),
        compiler_params=_params(("parallel", "parallel", "parallel")),
        name="compress",
    )(ch, pe, w1, w2)


def _tile_heads(a):
    return jnp.concatenate([a] * NSA_HPG, axis=1)


def _nsa_kernel(qT_ref, kch_ref, kcl_ref, vcT_ref, ks_ref, vsT_ref, kw_ref, vwT_ref, gT_ref,
                o_ref, sel_ref, qx_ref, psum_ref, *, seq):
    s0 = pl.program_id(2) * Q_TILE
    nsel = seq // SEL_BLOCK
    ncmp_pad = seq // CMP_STRIDE
    dh = NSA_HEAD_DIM

    q = qT_ref[0, 0, 0] * (dh ** -0.5 * LOG2_E)
    q_hi, q_lo = _split2(q)
    t_q = s0 + lax.broadcasted_iota(jnp.int32, (1, Q_TILE), 1)
    t_all = _tile_heads(t_q)

    kch = kch_ref[0, 0]
    sc = _dot(kch, q_hi) + _dot(kch, q_lo) + _dot(kcl_ref[0, 0], q_hi)
    n_end = lax.broadcasted_iota(jnp.int32, (ncmp_pad, 1), 0) * CMP_STRIDE + (CMP_BLOCK - 1)
    sc = sc + _tile_heads(jnp.where(n_end <= t_q, 0.0, NEG_INF))
    m_c = jnp.max(sc, axis=0, keepdims=True)
    e_c = jnp.exp2(sc - m_c)
    l_c = jnp.sum(e_c, axis=0, keepdims=True)
    p_c = e_c * jnp.where(t_all >= CMP_BLOCK - 1, 1.0 / l_c, 0.0)
    o_c = _dot(vcT_ref[0, 0], p_c.astype(BF16))

    p_sum = p_c[:, 0:Q_TILE]
    for r in range(1, NSA_HPG):
        p_sum = p_sum + p_c[:, r * Q_TILE:(r + 1) * Q_TILE]
    ratio = SEL_BLOCK // CMP_STRIDE
    first_row = 8 - (CMP_BLOCK // CMP_STRIDE - 1)
    psum_ref[0:8, :] = jnp.zeros((8, Q_TILE), F32)
    psum_ref[8:, :] = p_sum
    imp = psum_ref[pl.ds(first_row, nsel, stride=ratio), :]
    for m in range(1, ratio + CMP_BLOCK // CMP_STRIDE - 1):
        imp = imp + psum_ref[pl.ds(first_row + m, nsel, stride=ratio), :]
    j_idx = lax.broadcasted_iota(jnp.int32, (nsel, Q_TILE), 0)
    cur = lax.shift_right_logical(t_q, SEL_BLOCK.bit_length() - 1)
    forced = (j_idx == 0) | (j_idx == cur) | (j_idx == cur - 1)
    score = jnp.where(j_idx <= cur, imp + jnp.where(forced, FORCE_BONUS, 0.0), -1.0)
    j_f = j_idx.astype(F32)
    sel = jnp.zeros((nsel, Q_TILE), F32)
    for _ in range(min(SEL_TOP_N, nsel)):
        best = jnp.max(score, axis=0, keepdims=True)
        first = jnp.min(jnp.where(score == best, j_f, float(nsel)), axis=0, keepdims=True)
        pick = j_f == first
        sel = jnp.where(pick, 1.0, sel)
        score = jnp.where(pick, -3.0e38, score)
    sel_ref[...] = sel

    blocks_per_tile = SEL_KV_TILE // SEL_BLOCK
    for slot in range(SEL_GROUP):
        qx_ref[slot, 0:dh, :] = q_hi
        qx_ref[slot, dh:, :] = jnp.zeros((qx_ref.shape[1] - dh, NSA_ROWS), BF16)
    k_off = lax.broadcasted_iota(jnp.int32, (SEL_KV_TILE, 1), 0)

    def sel_scores(j, slot, causal):
        base = pl.multiple_of(j * SEL_KV_TILE, SEL_KV_TILE)
        member = sel_ref[pl.ds(pl.multiple_of(j * blocks_per_tile, blocks_per_tile), blocks_per_tile), :]
        bias = _tile_heads(jnp.where(member > 0.5, 0.0, NEG_INF))
        qx_ref[slot, dh:dh + 2 * blocks_per_tile, :] = jnp.concatenate(
            [bias, jnp.zeros_like(bias)], axis=0).astype(BF16)
        s = _dot(ks_ref[0, 0, pl.ds(base, SEL_KV_TILE), :], qx_ref[slot])
        if causal:
            s = s + _tile_heads(jnp.where((base + k_off) <= t_q, 0.0, NEG_INF))
        return s

    def sel_update(j, s, m_i, acc):
        base = pl.multiple_of(j * SEL_KV_TILE, SEL_KV_TILE)
        m_new = jnp.maximum(m_i, jnp.max(s, axis=0, keepdims=True))
        p = jnp.exp2(s - m_new).astype(BF16)
        pv = _dot(vsT_ref[0, 0, :, pl.ds(base, SEL_KV_TILE)], p)
        return m_new, jnp.exp2(m_i - m_new) * acc + pv

    def sel_group(i, carry, causal):
        tiles = [SEL_GROUP * i + u for u in range(SEL_GROUP)]
        scores = [sel_scores(j, u, causal) for u, j in enumerate(tiles)]
        for j, s in zip(tiles, scores):
            carry = sel_update(j, s, *carry)
        return carry

    group_keys = SEL_GROUP * SEL_KV_TILE
    n_groups = lax.div(s0 + (Q_TILE + group_keys - 1), group_keys)
    init = (jnp.full((1, NSA_ROWS), NEG_INF, F32), jnp.zeros((vsT_ref.shape[2], NSA_ROWS), F32))
    carry = lax.fori_loop(0, n_groups - 1, lambda i, c: sel_group(i, c, False), init)
    _, acc_s = sel_group(n_groups - 1, carry, True)
    o_s = acc_s[0:dh] * (1.0 / acc_s[dh:dh + 1])

    span = WINDOW + Q_TILE
    lo = pl.multiple_of(jnp.maximum(s0 - WINDOW, 0), Q_TILE)
    kpos = lo + lax.broadcasted_iota(jnp.int32, (span, 1), 0)
    sw = _dot(kw_ref[0, 0, pl.ds(lo, span), :], q_hi)
    sw = sw + _tile_heads(jnp.where((kpos <= t_q) & (kpos > t_q - WINDOW), 0.0, NEG_INF))
    e_w = jnp.exp2(sw - jnp.max(sw, axis=0, keepdims=True)).astype(BF16)
    acc_w = _dot(vwT_ref[0, 0, :, pl.ds(lo, span)], e_w)
    o_w = acc_w[0:dh] * (1.0 / acc_w[dh:dh + 1])

    g = _sigmoid(gT_ref[0, 0, 0])
    o_ref[0, 0, 0] = g[0:1] * o_c + g[1:2] * o_s + g[2:3] * o_w


def _nsa(qT, kch, kcl, vcT, ks, vsT, kw, vwT, gT, seq):
    b, g, nqb = qT.shape[:3]
    d = NSA_HEAD_DIM
    ncp = seq // CMP_STRIDE
    nsel = seq // SEL_BLOCK
    dk = ks.shape[-1]
    dv = vsT.shape[2]
    per_bg = lambda i, j, k: (i, j, 0, 0)
    per_q = lambda i, j, k: (i, j, k, 0, 0)
    return pl.pallas_call(
        functools.partial(_nsa_kernel, seq=seq),
        grid=(b, g, nqb),
        in_specs=[pl.BlockSpec((1, 1, 1, d, NSA_ROWS), per_q),
                  pl.BlockSpec((1, 1, ncp, d), per_bg),
                  pl.BlockSpec((1, 1, ncp, d), per_bg),
                  pl.BlockSpec((1, 1, d, ncp), per_bg),
                  pl.BlockSpec((1, 1, seq, dk), per_bg),
                  pl.BlockSpec((1, 1, dv, seq), per_bg),
                  pl.BlockSpec((1, 1, seq, d), per_bg),
                  pl.BlockSpec((1, 1, dv, seq), per_bg),
                  pl.BlockSpec((1, 1, 1, 3, NSA_ROWS), per_q)],
        out_specs=pl.BlockSpec((1, 1, 1, d, NSA_ROWS), per_q),
        out_shape=jax.ShapeDtypeStruct((b, g, nqb, d, NSA_ROWS), F32),
        scratch_shapes=[pltpu.VMEM((nsel, Q_TILE), F32), pltpu.VMEM((SEL_GROUP, dk, NSA_ROWS), BF16),
                        pltpu.VMEM((ncp + 8, Q_TILE), F32)],
        compiler_params=_params(("parallel", "parallel", "arbitrary")),
        name="nsa",
    )(qT, kch, kcl, vcT, ks, vsT, kw, vwT, gT)


def _hgrn_chunk(q, f, v, lb, state, lower, gstart, dmat, off_mask):
    c = HG_CHUNK
    log_f = jnp.log(lb + (1.0 - lb) * _sigmoid(f))
    kk = (1.0 - lb) * _sigmoid(-f)
    l1, l2, l3 = _split3(log_f)
    bt = _dot(lower, l1) + _dot(lower, l2) + _dot(lower, l3)
    bs = _dot(gstart, l1) + _dot(gstart, l2) + _dot(gstart, l3)
    btT, bsT, kkT, qT = bt.T, bs.T, kk.T, q.T

    attT_diag = jnp.zeros((c, c), F32)
    for d in range(HG_SUB):
        if d == 0:
            prod = qT * kkT
        else:
            decay = jnp.exp(jnp.minimum(btT - pltpu.roll(btT, d, 1), 0.0))
            prod = (qT * pltpu.roll(kkT, d, 1)) * decay
        band = jnp.sum(prod, axis=0, keepdims=True)
        attT_diag = jnp.where(dmat == d, band, attT_diag)

    qh = (q * jnp.exp(bt - bs)).astype(BF16)
    blocks = [jnp.zeros((HG_SUB, c), F32)]
    for i in range(1, c // HG_SUB):
        b_i = bsT[:, i * HG_SUB:i * HG_SUB + 1]
        khT = kkT * jnp.exp(jnp.minimum(b_i - btT, 0.0))
        blocks.append(_dot(qh[i * HG_SUB:(i + 1) * HG_SUB], khT.astype(BF16)))
    att = jnp.where(off_mask, jnp.concatenate(blocks, axis=0), 0.0) + attT_diag.T

    vb = v.astype(BF16)
    o = _dot((q * jnp.exp(bt)).astype(BF16), state.astype(BF16)) + _dot(att.astype(BF16), vb)
    b_last = btT[:, c - 1:c]
    k_decT = kkT * jnp.exp(b_last - btT)
    return o, state * jnp.exp(b_last) + _dot(k_decT.astype(BF16), vb)


def _hgrn_kernel(q_ref, f_ref, v_ref, gate_ref, lb_ref, ng_ref, o_ref, state_ref):
    c = HG_CHUNK

    @pl.when(pl.program_id(2) == 0)
    def _():
        state_ref[...] = jnp.zeros_like(state_ref)

    row = lax.broadcasted_iota(jnp.int32, (c, c), 0)
    col = lax.broadcasted_iota(jnp.int32, (c, c), 1)
    sub = HG_SUB.bit_length() - 1
    row_blk = lax.shift_right_logical(row, sub)
    col_blk = lax.shift_right_logical(col, sub)
    lower = jnp.where(col <= row, 1.0, 0.0).astype(BF16)
    gstart = jnp.where(col_blk < row_blk, 1.0, 0.0).astype(BF16)
    dmat = jnp.where(row_blk == col_blk, col - row, -1)
    off_mask = col_blk < row_blk
    lb = lb_ref[0]
    ng = ng_ref[...]

    def step(i, state):
        rows = pl.ds(pl.multiple_of(i * c, c), c)
        o, state = _hgrn_chunk(q_ref[rows, :], f_ref[rows, :], v_ref[rows, :], lb, state,
                               lower, gstart, dmat, off_mask)
        gate = gate_ref[rows, :]
        ms = jnp.mean(o * o, axis=-1, keepdims=True)
        o_ref[rows, :] = o * lax.rsqrt(ms + RMS_EPS) * ng * (gate * _sigmoid(gate))
        return state

    state_ref[...] = lax.fori_loop(0, HG_ROWS // c, step, state_ref[...])


def _hgrn(h, lb, ng, bsz, seq, col0):
    nblk = seq // HG_ROWS
    grp = HG_COLS // HG_KEY_DIM
    blk = lambda k: pl.BlockSpec((HG_ROWS, HG_KEY_DIM), lambda i, j, c: (i * nblk + c, col0 + k * grp + j))
    return pl.pallas_call(
        _hgrn_kernel,
        grid=(bsz, HG_HEADS, nblk),
        in_specs=[blk(0), blk(1), blk(2), blk(3),
                  pl.BlockSpec((1, 1, HG_KEY_DIM), lambda i, j, c: (j, 0, 0)),
                  pl.BlockSpec((1, HG_VAL_DIM), lambda i, j, c: (0, 0))],
        out_specs=pl.BlockSpec((HG_ROWS, HG_VAL_DIM), lambda i, j, c: (i * nblk + c, j)),
        out_shape=jax.ShapeDtypeStruct((bsz * seq, HG_COLS), F32),
        scratch_shapes=[pltpu.VMEM((HG_KEY_DIM, HG_VAL_DIM), F32)],
        compiler_params=_params(("parallel", "parallel", "arbitrary")),
        name="hgrn",
    )(h, h, h, h, lb, ng)


def _mix_kernel(nsa_ref, hg_ref, x_ref, w_ref, ng_ref, g_ref, b_ref, o_ref):
    o_n = nsa_ref[...]
    half = o_n.shape[1]
    o_n = o_n * lax.rsqrt(jnp.mean(o_n * o_n, axis=-1, keepdims=True) + RMS_EPS) * ng_ref[...]
    mix = _dot(o_n.astype(BF16), w_ref[:half]) + _dot(hg_ref[...].astype(BF16), w_ref[half:])
    o_ref[...] = _layer_norm(DEEPNORM_ALPHA * x_ref[...] + mix, g_ref[...], b_ref[...])


def _mix(o_nsa, o_hg, x, w_out, ng, g, b, tm):
    t, d = x.shape
    half = o_nsa.shape[1]
    rows = lambda w: pl.BlockSpec((tm, w), lambda i: (i, 0))
    full = lambda r, c: pl.BlockSpec((r, c), lambda i: (0, 0))
    return pl.pallas_call(
        _mix_kernel,
        grid=(t // tm,),
        in_specs=[rows(half), rows(half), rows(d), full(d, d), full(1, half), full(1, d), full(1, d)],
        out_specs=rows(d),
        out_shape=jax.ShapeDtypeStruct((t, d), F32),
        compiler_params=_params(("parallel",)),
        name="mix",
    )(o_nsa, o_hg, x, w_out, ng, g, b)


def _xattn_kernel(x_ref, wq_ref, kT_ref, v_ref, wo_ref, g_ref, b_ref, o_ref):
    x = x_ref[0]
    d = x.shape[1]
    dh = d // XA_HEADS
    q = _dot(x.astype(BF16), wq_ref[...])
    heads = []
    for h in range(XA_HEADS):
        cols = slice(h * dh, (h + 1) * dh)
        s = _dot(q[:, cols].astype(BF16), kT_ref[0, cols, :]) * (dh ** -0.5)
        e = jnp.exp(s - jnp.max(s, axis=-1, keepdims=True))
        p = e * (1.0 / jnp.sum(e, axis=-1, keepdims=True))
        heads.append(_dot(p.astype(BF16), v_ref[0, :, cols]))
    o = jnp.concatenate(heads, axis=1)
    xa = _dot(o.astype(BF16), wo_ref[...])
    o_ref[0] = _layer_norm(DEEPNORM_ALPHA * x + xa, g_ref[...], b_ref[...])


def _xattn(x, wq, kT, v, wo, g, b, tm):
    bsz, seq, d = x.shape
    m = v.shape[1]
    full = lambda r, c: pl.BlockSpec((r, c), lambda i, j: (0, 0))
    return pl.pallas_call(
        _xattn_kernel,
        grid=(bsz, seq // tm),
        in_specs=[pl.BlockSpec((1, tm, d), lambda i, j: (i, j, 0)),
                  full(d, d),
                  pl.BlockSpec((1, d, m), lambda i, j: (i, 0, 0)),
                  pl.BlockSpec((1, m, d), lambda i, j: (i, 0, 0)),
                  full(d, d), full(1, d), full(1, d)],
        out_specs=pl.BlockSpec((1, tm, d), lambda i, j: (i, j, 0)),
        out_shape=jax.ShapeDtypeStruct((bsz, seq, d), F32),
        compiler_params=_params(("parallel", "parallel")),
        name="xattn",
    )(x, wq, kT, v, wo, g, b)


def _route_kernel(x_ref, wh_ref, wl_ref, bias_ref, ids_ref, wts_ref):
    xh, xl = _split2(x_ref[...])
    logits = _dot(xh, wh_ref[...]) + _dot(xh, wl_ref[...]) + _dot(xl, wh_ref[...]) + bias_ref[...]
    lane = lax.broadcasted_iota(jnp.int32, logits.shape, 1)
    lane_f = lane.astype(F32)
    big = float(LANES)

    is_g = lane < MOE_GROUPS
    g_max = jnp.max(jnp.where(is_g, logits, NEG_INF), axis=-1, keepdims=True)
    g_sum = jnp.sum(jnp.where(is_g, jnp.exp(logits - g_max), 0.0), axis=-1, keepdims=True)
    g_w = 1.0 / g_sum
    g_idx = jnp.min(jnp.where(is_g & (logits == g_max), lane_f, big), axis=-1, keepdims=True)

    e_lo = MOE_GROUPS + MOE_EXPERTS_PER_GROUP * g_idx
    is_e = (lane_f >= e_lo) & (lane_f < e_lo + MOE_EXPERTS_PER_GROUP)
    e_log = jnp.where(is_e, logits, NEG_INF)
    e_max = jnp.max(e_log, axis=-1, keepdims=True)
    e_exp = jnp.where(is_e, jnp.exp(logits - e_max), 0.0)
    e_sum = jnp.sum(e_exp, axis=-1, keepdims=True)
    i1 = jnp.min(jnp.where(is_e & (e_log == e_max), lane_f, big), axis=-1, keepdims=True)
    rest = jnp.where(lane_f == i1, NEG_INF, e_log)
    r_max = jnp.max(rest, axis=-1, keepdims=True)
    i2 = jnp.min(jnp.where(is_e & (lane_f != i1) & (rest == r_max), lane_f, big), axis=-1, keepdims=True)
    p1 = 1.0 / e_sum
    p2 = jnp.exp(r_max - e_max) / e_sum
    tot = p1 + p2
    w1 = g_w * (p1 / tot)
    w2 = g_w * (p2 / tot)
    ids = jnp.where(lane == 0, i1, i2) - float(MOE_GROUPS)
    ids_ref[...] = ids.astype(jnp.int32)
    wts_ref[...] = jnp.where(lane == 0, w1, w2)


def _route(x, wh, wl, bias, tm):
    t, d = x.shape
    rows = lambda w: pl.BlockSpec((tm, w), lambda i: (i, 0))
    full = lambda r, c: pl.BlockSpec((r, c), lambda i: (0, 0))
    return pl.pallas_call(
        _route_kernel,
        grid=(t // tm,),
        in_specs=[rows(d), full(d, LANES), full(d, LANES), full(1, LANES)],
        out_specs=[rows(LANES), rows(LANES)],
        out_shape=[jax.ShapeDtypeStruct((t, LANES), jnp.int32), jax.ShapeDtypeStruct((t, LANES), F32)],
        compiler_params=_params(("parallel",)),
        name="route",
    )(x, wh, wl, bias)


def _experts_kernel(code_ref, bexp_ref, nact_ref, x_hbm, wg_ref, wu_ref, wd_ref, sw_ref, out_hbm,
                    xbuf, ybuf, gsem, ssem):
    i = pl.program_id(0)
    base = i * MOE_BLOCK

    def gather(j):
        tok = lax.div(jnp.maximum(code_ref[base + j], 0), MOE_TOP_K)
        return pltpu.make_async_copy(x_hbm.at[pl.ds(tok, 1)], xbuf.at[pl.ds(j, 1)], gsem)

    def scatter(j):
        dst = jnp.maximum(code_ref[base + j], 0)
        return pltpu.make_async_copy(ybuf.at[pl.ds(j, 1)], out_hbm.at[pl.ds(dst, 1)], ssem)

    @pl.when(i < nact_ref[0])
    def _():
        def start_gather(j, c):
            gather(j).start()
            return c

        def wait_gather(j, c):
            gather(j).wait()
            return c

        lax.fori_loop(0, MOE_BLOCK, start_gather, 0)
        lax.fori_loop(0, MOE_BLOCK, wait_gather, 0)
        xb = xbuf[...].astype(BF16)
        gate = _dot(xb, wg_ref[0])
        hid = (gate * _sigmoid(gate)) * _dot(xb, wu_ref[0])
        ybuf[...] = _dot(hid.astype(BF16), wd_ref[0]) * sw_ref[...]

        def start_scatter(j, c):
            @pl.when(code_ref[base + j] >= 0)
            def _():
                scatter(j).start()
            return c

        def wait_scatter(j, c):
            @pl.when(code_ref[base + j] >= 0)
            def _():
                scatter(j).wait()
            return c

        lax.fori_loop(0, MOE_BLOCK, start_scatter, 0)
        lax.fori_loop(0, MOE_BLOCK, wait_scatter, 0)


def _experts(code, bexp, nact, x, wg, wu, wd, slot_w):
    t, d = x.shape
    dff = wg.shape[-1]
    n_blocks = bexp.shape[0]
    by_expert = lambda i, code, bexp, nact: (bexp[i], 0, 0)
    grid_spec = pltpu.PrefetchScalarGridSpec(
        num_scalar_prefetch=3,
        grid=(n_blocks,),
        in_specs=[pl.BlockSpec(memory_space=pl.ANY),
                  pl.BlockSpec((1, d, dff), by_expert),
                  pl.BlockSpec((1, d, dff), by_expert),
                  pl.BlockSpec((1, dff, d), by_expert),
                  pl.BlockSpec((MOE_BLOCK, 1), lambda i, code, bexp, nact: (i, 0))],
        out_specs=pl.BlockSpec(memory_space=pl.ANY),
        scratch_shapes=[pltpu.VMEM((MOE_BLOCK, d), F32), pltpu.VMEM((MOE_BLOCK, d), F32),
                        pltpu.SemaphoreType.DMA, pltpu.SemaphoreType.DMA])
    return pl.pallas_call(
        _experts_kernel,
        grid_spec=grid_spec,
        out_shape=jax.ShapeDtypeStruct((t * MOE_TOP_K, d), F32),
        compiler_params=_params(("arbitrary",)),
        name="experts",
    )(code, bexp, nact, x, wg, wu, wd, slot_w)


def _combine_kernel(y_ref, x_ref, g_ref, b_ref, o_ref):
    d = x_ref.shape[1]
    ff = y_ref[:, :d] + y_ref[:, d:]
    o_ref[...] = _layer_norm(DEEPNORM_ALPHA * x_ref[...] + ff, g_ref[...], b_ref[...])


def _combine(y2, x, g, b, tm):
    t, d = x.shape
    return pl.pallas_call(
        _combine_kernel,
        grid=(t // tm,),
        in_specs=[pl.BlockSpec((tm, MOE_TOP_K * d), lambda i: (i, 0)),
                  pl.BlockSpec((tm, d), lambda i: (i, 0)),
                  pl.BlockSpec((1, d), lambda i: (0, 0)),
                  pl.BlockSpec((1, d), lambda i: (0, 0))],
        out_specs=pl.BlockSpec((tm, d), lambda i: (i, 0)),
        out_shape=jax.ShapeDtypeStruct((t, d), F32),
        compiler_params=_params(("parallel",)),
        name="combine",
    )(y2, x, g, b)


def _dispatch_plan(ids, wts, n_tok):
    n_assign = n_tok * MOE_TOP_K
    experts = ids.reshape(-1)
    onehot = (experts[:, None] == jnp.arange(MOE_N_EXPERTS)[None, :]).astype(jnp.int32)
    rank = jnp.sum((jnp.cumsum(onehot, axis=0) - onehot) * onehot, axis=1)
    counts = jnp.sum(onehot, axis=0)
    padded = ((counts + MOE_BLOCK - 1) // MOE_BLOCK) * MOE_BLOCK
    pend = jnp.cumsum(padded)
    pstart = pend - padded
    dest = pstart[experts] + rank
    cap = n_assign + MOE_N_EXPERTS * MOE_BLOCK
    n_blocks = cap // MOE_BLOCK
    code = jnp.full((cap,), -1, jnp.int32).at[dest].set(jnp.arange(n_assign, dtype=jnp.int32))
    slot_w = jnp.zeros((cap,), F32).at[dest].set(wts.reshape(-1))
    bexp = jnp.minimum(jnp.searchsorted(pend, jnp.arange(n_blocks) * MOE_BLOCK, side='right'),
                       MOE_N_EXPERTS - 1).astype(jnp.int32)
    nact = (pend[-1] // MOE_BLOCK).astype(jnp.int32).reshape(1)
    return code, bexp, nact, slot_w.reshape(cap, 1)


def kernel(x, mem, w_in, cmp_pe_k, cmp_pe_v, cmp_w1_k, cmp_w2_k, cmp_w1_v, cmp_w2_v, nsa_norm_g,
           hg_lb_logits, hg_norm_g, w_out, ln1_g, ln1_b, xa_wq, xa_wk, xa_wv, xa_wo, ln2_g, ln2_b,
           moe_w_group, moe_b_group, moe_w_expert, moe_b_expert, moe_w_gate, moe_w_up, moe_w_down,
           ln3_g, ln3_b):
    b, s, d = x.shape
    t = b * s
    g, dh, hpg = NSA_KV_GROUPS, NSA_HEAD_DIM, NSA_HPG
    nqb = s // Q_TILE
    nch = s // CMP_STRIDE
    row = lambda a: a.reshape(1, -1)
    lb_all = jnp.cumsum(jax.nn.softmax(hg_lb_logits.astype(F32), axis=0), axis=0)
    xt = x.reshape(t, d)
    for l in range(DEPTH):
        nsa_cols = NSA_Q_COLS + 6 * NSA_KV_COLS
        w_l = w_in[l]
        w_perm = jnp.concatenate(
            [w_l[:, :nsa_cols], w_l[:, nsa_cols + NSA_GATE_COLS:], w_l[:, nsa_cols:nsa_cols + NSA_GATE_COLS],
             jnp.zeros((d, LANES - NSA_GATE_COLS), w_l.dtype)], axis=1)
        h = _matmul(xt, w_perm.astype(BF16), 256)
        offs = [0]
        for n in (NSA_Q_COLS,) + (NSA_KV_COLS,) * 6:
            offs.append(offs[-1] + n)
        (nq, nkc, nvc, nks, nvs, nkw, nvw) = [h[:, offs[i]:offs[i + 1]] for i in range(7)]
        hg_col0 = nsa_cols // LANES
        gate0 = nsa_cols + 4 * HG_COLS
        ngate = h[:, gate0:gate0 + NSA_GATE_COLS]

        chunks = lambda a: jnp.transpose(a.reshape(b, nch, CMP_STRIDE, g, dh), (0, 3, 1, 2, 4)).reshape(
            b, g, nch, CMP_STRIDE * dh)
        comp = _compress(jnp.stack([chunks(nkc), chunks(nvc)]),
                         jnp.stack([cmp_pe_k[l].reshape(1, -1), cmp_pe_v[l].reshape(1, -1)]),
                         jnp.stack([cmp_w1_k[l], cmp_w1_v[l]]),
                         jnp.stack([cmp_w2_k[l], cmp_w2_v[l]]))
        kch, kcl = _split2(comp[0])
        vcT = jnp.swapaxes(comp[1], -1, -2).astype(BF16)

        keys = lambda a: jnp.transpose(a.reshape(b, s, g, dh), (0, 2, 1, 3))
        tile_block = (jnp.arange(s) % SEL_KV_TILE) // SEL_BLOCK
        block_onehot = (tile_block[:, None] == jnp.arange(LANES - dh)[None, :]).astype(F32)
        keys_ext = lambda a: jnp.concatenate(
            [keys(a), jnp.broadcast_to(block_onehot, (b, g, s, LANES - dh))], axis=-1).astype(BF16)
        ones_row = jnp.zeros((b, g, NSA_V_ROWS - dh, s), F32).at[:, :, 0].set(1.0)
        valsT = lambda a: jnp.concatenate(
            [jnp.transpose(a.reshape(b, s, g, dh), (0, 2, 3, 1)), ones_row], axis=2).astype(BF16)
        qT = jnp.transpose(nq.reshape(b, nqb, Q_TILE, g, hpg, dh), (0, 3, 1, 5, 4, 2)).reshape(
            b, g, nqb, dh, NSA_ROWS)
        gT = jnp.transpose(ngate.reshape(b, nqb, Q_TILE, g, hpg, 3), (0, 3, 1, 5, 4, 2)).reshape(
            b, g, nqb, 3, NSA_ROWS)
        oT = _nsa(qT, kch, kcl, vcT, keys_ext(nks), valsT(nvs), keys(nkw).astype(BF16), valsT(nvw), gT, s)
        o_nsa = jnp.transpose(oT.reshape(b, g, nqb, dh, hpg, Q_TILE), (0, 2, 5, 1, 4, 3)).reshape(
            t, NSA_Q_COLS)

        o_hg = _hgrn(h, lb_all[l].reshape(HG_HEADS, 1, HG_KEY_DIM), row(hg_norm_g[l]), b, s, hg_col0)

        x1 = _mix(o_nsa, o_hg, xt, w_out[l].astype(BF16), row(nsa_norm_g[l]), row(ln1_g[l]), row(ln1_b[l]), 256)

        n_mem = mem.shape[1]
        kv = _matmul(mem.reshape(b * n_mem, d),
                     jnp.concatenate([xa_wk[l], xa_wv[l]], axis=1).astype(BF16), n_mem)
        kT = jnp.swapaxes(kv[:, :d].reshape(b, n_mem, d), 1, 2).astype(BF16)
        v = kv[:, d:].reshape(b, n_mem, d).astype(BF16)
        x2 = _xattn(x1.reshape(b, s, d), xa_wq[l].astype(BF16), kT, v, xa_wo[l].astype(BF16),
                    row(ln2_g[l]), row(ln2_b[l]), 256).reshape(t, d)

        w_r = jnp.concatenate([moe_w_group[l], moe_w_expert[l]], axis=1)
        w_r = jnp.pad(w_r, ((0, 0), (0, LANES - w_r.shape[1])))
        b_r = jnp.pad(jnp.concatenate([moe_b_group[l], moe_b_expert[l]]), (0, LANES - MOE_GROUPS - MOE_N_EXPERTS))
        w_rh, w_rl = _split2(w_r)
        ids, wts = _route(x2, w_rh, w_rl, row(b_r), 512)
        code, bexp, nact, slot_w = _dispatch_plan(ids[:, :MOE_TOP_K], wts[:, :MOE_TOP_K], t)
        y2 = _experts(code, bexp, nact, x2, moe_w_gate[l].astype(BF16), moe_w_up[l].astype(BF16),
                      moe_w_down[l].astype(BF16), slot_w)
        xt = _combine(y2.reshape(t, MOE_TOP_K * d), x2, row(ln3_g[l]), row(ln3_b[l]), 256)
    return xt.reshape(b, s, d)
```

```python
import functools

import jax
import jax.numpy as jnp
from jax import lax
from jax.experimental import pallas as pl
from jax.experimental.pallas import tpu as pltpu

F32 = jnp.float32
BF16 = jnp.bfloat16

NSA_HEAD_DIM = 64
NSA_HEADS = 8
NSA_KV_GROUPS = 2
NSA_HPG = NSA_HEADS // NSA_KV_GROUPS
CMP_BLOCK = 32
CMP_STRIDE = 16
CMP_HIDDEN = 256
SEL_BLOCK = 64
SEL_TOP_N = 16
WINDOW = 512
FORCE_BONUS = 1.0e4
HG_KEY_DIM = 128
HG_VAL_DIM = 128
HG_HEADS = 4
XA_HEADS = 4
MOE_GROUPS = 4
MOE_EXPERTS_PER_GROUP = 8
MOE_N_EXPERTS = MOE_GROUPS * MOE_EXPERTS_PER_GROUP
MOE_TOP_K = 2
MOE_BLOCK = 128
DEPTH = 1
DEEPNORM_ALPHA = (2.0 * DEPTH) ** 0.25
LN_EPS = 1e-5
RMS_EPS = 1e-6
NEG_INF = -1e30

NSA_Q_COLS = NSA_HEADS * NSA_HEAD_DIM
NSA_KV_COLS = NSA_KV_GROUPS * NSA_HEAD_DIM
NSA_GATE_COLS = NSA_HEADS * 3
HG_COLS = HG_HEADS * HG_KEY_DIM

LANES = 128
Q_TILE = 128
NSA_ROWS = NSA_HPG * Q_TILE
SEL_KV_TILE = 512
SEL_GROUP = 4
NSA_V_ROWS = NSA_HEAD_DIM + 16
LOG2_E = 1.4426950408889634
HG_CHUNK = 128
HG_SUB = 16
HG_ROWS = 512
VMEM_LIMIT = 48 * 1024 * 1024


def _dot(a, b):
    return jnp.dot(a, b, preferred_element_type=F32)


def _dot_nt(a, b):
    return lax.dot_general(a, b, (((1,), (1,)), ((), ())), preferred_element_type=F32)


def _split2(a):
    hi = a.astype(BF16)
    lo = (a - hi.astype(F32)).astype(BF16)
    return hi, lo


def _split3(a):
    p1 = a.astype(BF16)
    r1 = a - p1.astype(F32)
    p2 = r1.astype(BF16)
    p3 = (r1 - p2.astype(F32)).astype(BF16)
    return p1, p2, p3


def _dot3(a, b):
    ah, al = _split2(a)
    bh, bl = _split2(b)
    return _dot(ah, bh) + _dot(ah, bl) + _dot(al, bh)


def _sigmoid(x):
    return 1.0 / (1.0 + jnp.exp(-x))


def _layer_norm(y, g, b):
    mu = jnp.mean(y, axis=-1, keepdims=True)
    d = y - mu
    var = jnp.mean(d * d, axis=-1, keepdims=True)
    return d * lax.rsqrt(var + LN_EPS) * g + b


def _params(sem):
    return pltpu.CompilerParams(dimension_semantics=sem, vmem_limit_bytes=VMEM_LIMIT)


def _matmul_kernel(x_ref, w_ref, o_ref):
    o_ref[...] = _dot(x_ref[...].astype(BF16), w_ref[...]).astype(o_ref.dtype)


def _matmul(x, w, tm):
    m, k = x.shape
    n = w.shape[1]
    return pl.pallas_call(
        _matmul_kernel,
        grid=(m // tm,),
        in_specs=[pl.BlockSpec((tm, k), lambda i: (i, 0)),
                  pl.BlockSpec((k, n), lambda i: (0, 0))],
        out_specs=pl.BlockSpec((tm, n), lambda i: (i, 0)),
        out_shape=jax.ShapeDtypeStruct((m, n), F32),
        compiler_params=_params(("parallel",)),
        name="proj",
    )(x, w)


def _compress_kernel(ch_ref, pe_ref, w1_ref, w2_ref, o_ref):
    ch = ch_ref[0, 0, 0]
    half = ch.shape[1]
    nch = ch.shape[0]
    pe = pe_ref[0]
    w1 = w1_ref[0]
    top = _dot3(ch + pe[:, :half], w1[:half])
    bot = _dot3(ch + pe[:, half:], w1[half:])
    hid = top + pltpu.roll(bot, nch - 1, 0)
    c = 0.7978845608028654
    act = 0.5 * hid * (1.0 + jnp.tanh(c * (hid + 0.044715 * hid * hid * hid)))
    out = _dot3(act, w2_ref[0])
    row = lax.broadcasted_iota(jnp.int32, out.shape, 0)
    o_ref[0, 0, 0] = jnp.where(row < nch - 1, out, 0.0)


def _compress(ch, pe, w1, w2):
    _, b, g, nch, width = ch.shape
    hidden = w1.shape[-1]
    d = w2.shape[-1]
    return pl.pallas_call(
        _compress_kernel,
        grid=(2, b, g),
        in_specs=[pl.BlockSpec((1, 1, 1, nch, width), lambda a, i, j: (a, i, j, 0, 0)),
                  pl.BlockSpec((1, 1, 2 * width), lambda a, i, j: (a, 0, 0)),
                  pl.BlockSpec((1, 2 * width, hidden), lambda a, i, j: (a, 0, 0)),
                  pl.BlockSpec((1, hidden, d), lambda a, i, j: (a, 0, 0))],
        out_specs=pl.BlockSpec((1, 1, 1, nch, d), lambda a, i, j: (a, i, j, 0, 0)),
        out_shape=jax.ShapeDtypeStruct((2, b, g, nch, d), F32),
        compiler_params=_params(("parallel", "parallel", "parallel")),
        name="compress",
    )(ch, pe, w1, w2)


def _tile_heads(a):
    return jnp.concatenate([a] * NSA_HPG, axis=1)


def _nsa_kernel(qT_ref, kch_ref, kcl_ref, vcT_ref, ks_ref, vsT_ref, kw_ref, vwT_ref, gT_ref,
                o_ref, sel_ref, qx_ref, psum_ref, *, seq):
    s0 = pl.program_id(2) * Q_TILE
    nsel = seq // SEL_BLOCK
    ncmp_pad = seq // CMP_STRIDE
    dh = NSA_HEAD_DIM

    q = qT_ref[0, 0, 0] * (dh ** -0.5 * LOG2_E)
    q_hi, q_lo = _split2(q)
    t_q = s0 + lax.broadcasted_iota(jnp.int32, (1, Q_TILE), 1)
    t_all = _tile_heads(t_q)

    kch = kch_ref[0, 0]
    sc = _dot(kch, q_hi) + _dot(kch, q_lo) + _dot(kcl_ref[0, 0], q_hi)
    n_end = lax.broadcasted_iota(jnp.int32, (ncmp_pad, 1), 0) * CMP_STRIDE + (CMP_BLOCK - 1)
    sc = sc + _tile_heads(jnp.where(n_end <= t_q, 0.0, NEG_INF))
    m_c = jnp.max(sc, axis=0, keepdims=True)
    e_c = jnp.exp2(sc - m_c)
    l_c = jnp.sum(e_c, axis=0, keepdims=True)
    p_c = e_c * jnp.where(t_all >= CMP_BLOCK - 1, 1.0 / l_c, 0.0)
    o_c = _dot(vcT_ref[0, 0], p_c.astype(BF16))

    p_sum = p_c[:, 0:Q_TILE]
    for r in range(1, NSA_HPG):
        p_sum = p_sum + p_c[:, r * Q_TILE:(r + 1) * Q_TILE]
    ratio = SEL_BLOCK // CMP_STRIDE
    first_row = 8 - (CMP_BLOCK // CMP_STRIDE - 1)
    psum_ref[0:8, :] = jnp.zeros((8, Q_TILE), F32)
    psum_ref[8:, :] = p_sum
    imp = psum_ref[pl.ds(first_row, nsel, stride=ratio), :]
    for m in range(1, ratio + CMP_BLOCK // CMP_STRIDE - 1):
        imp = imp + psum_ref[pl.ds(first_row + m, nsel, stride=ratio), :]
    j_idx = lax.broadcasted_iota(jnp.int32, (nsel, Q_TILE), 0)
    cur = lax.shift_right_logical(t_q, SEL_BLOCK.bit_length() - 1)
    forced = (j_idx == 0) | (j_idx == cur) | (j_idx == cur - 1)
    score = jnp.where(j_idx <= cur, imp + jnp.where(forced, FORCE_BONUS, 0.0), -1.0)
    j_f = j_idx.astype(F32)
    sel = jnp.zeros((nsel, Q_TILE), F32)
    for _ in range(min(SEL_TOP_N, nsel)):
        best = jnp.max(score, axis=0, keepdims=True)
        first = jnp.min(jnp.where(score == best, j_f, float(nsel)), axis=0, keepdims=True)
        pick = j_f == first
        sel = jnp.where(pick, 1.0, sel)
        score = jnp.where(pick, -3.0e38, score)
    sel_ref[...] = sel

    blocks_per_tile = SEL_KV_TILE // SEL_BLOCK
    for slot in range(SEL_GROUP):
        qx_ref[slot, 0:dh, :] = q_hi
        qx_ref[slot, dh:, :] = jnp.zeros((qx_ref.shape[1] - dh, NSA_ROWS), BF16)
    k_off = lax.broadcasted_iota(jnp.int32, (SEL_KV_TILE, 1), 0)

    def sel_scores(j, slot, causal):
        base = pl.multiple_of(j * SEL_KV_TILE, SEL_KV_TILE)
        member = sel_ref[pl.ds(pl.multiple_of(j * blocks_per_tile, blocks_per_tile), blocks_per_tile), :]
        bias = _tile_heads(jnp.where(member > 0.5, 0.0, NEG_INF))
        qx_ref[slot, dh:dh + 2 * blocks_per_tile, :] = jnp.concatenate(
            [bias, jnp.zeros_like(bias)], axis=0).astype(BF16)
        s = _dot(ks_ref[0, 0, pl.ds(base, SEL_KV_TILE), :], qx_ref[slot])
        if causal:
            s = s + _tile_heads(jnp.where((base + k_off) <= t_q, 0.0, NEG_INF))
        return s

    def sel_update(j, s, m_i, acc):
        base = pl.multiple_of(j * SEL_KV_TILE, SEL_KV_TILE)
        m_new = jnp.maximum(m_i, jnp.max(s, axis=0, keepdims=True))
        p = jnp.exp2(s - m_new).astype(BF16)
        pv = _dot(vsT_ref[0, 0, :, pl.ds(base, SEL_KV_TILE)], p)
        return m_new, jnp.exp2(m_i - m_new) * acc + pv

    def sel_group(i, carry, causal):
        tiles = [SEL_GROUP * i + u for u in range(SEL_GROUP)]
        scores = [sel_scores(j, u, causal) for u, j in enumerate(tiles)]
        for j, s in zip(tiles, scores):
            carry = sel_update(j, s, *carry)
        return carry

    group_keys = SEL_GROUP * SEL_KV_TILE
    n_groups = lax.div(s0 + (Q_TILE + group_keys - 1), group_keys)
    init = (jnp.full((1, NSA_ROWS), NEG_INF, F32), jnp.zeros((vsT_ref.shape[2], NSA_ROWS), F32))
    carry = lax.fori_loop(0, n_groups - 1, lambda i, c: sel_group(i, c, False), init)
    _, acc_s = sel_group(n_groups - 1, carry, True)
    o_s = acc_s[0:dh] * (1.0 / acc_s[dh:dh + 1])

    span = WINDOW + Q_TILE
    lo = pl.multiple_of(jnp.maximum(s0 - WINDOW, 0), Q_TILE)
    kpos = lo + lax.broadcasted_iota(jnp.int32, (span, 1), 0)
    sw = _dot(kw_ref[0, 0, pl.ds(lo, span), :], q_hi)
    sw = sw + _tile_heads(jnp.where((kpos <= t_q) & (kpos > t_q - WINDOW), 0.0, NEG_INF))
    e_w = jnp.exp2(sw - jnp.max(sw, axis=0, keepdims=True)).astype(BF16)
    acc_w = _dot(vwT_ref[0, 0, :, pl.ds(lo, span)], e_w)
    o_w = acc_w[0:dh] * (1.0 / acc_w[dh:dh + 1])

    g = _sigmoid(gT_ref[0, 0, 0])
    o_ref[0, 0, 0] = g[0:1] * o_c + g[1:2] * o_s + g[2:3] * o_w


def _nsa(qT, kch, kcl, vcT, ks, vsT, kw, vwT, gT, seq):
    b, g, nqb = qT.shape[:3]
    d = NSA_HEAD_DIM
    ncp = seq // CMP_STRIDE
    nsel = seq // SEL_BLOCK
    dk = ks.shape[-1]
    dv = vsT.shape[2]
    per_bg = lambda i, j, k: (i, j, 0, 0)
    per_q = lambda i, j, k: (i, j, k, 0, 0)
    return pl.pallas_call(
        functools.partial(_nsa_kernel, seq=seq),
        grid=(b, g, nqb),
        in_specs=[pl.BlockSpec((1, 1, 1, d, NSA_ROWS), per_q),
                  pl.BlockSpec((1, 1, ncp, d), per_bg),
                  pl.BlockSpec((1, 1, ncp, d), per_bg),
                  pl.BlockSpec((1, 1, d, ncp), per_bg),
                  pl.BlockSpec((1, 1, seq, dk), per_bg),
                  pl.BlockSpec((1, 1, dv, seq), per_bg),
                  pl.BlockSpec((1, 1, seq, d), per_bg),
                  pl.BlockSpec((1, 1, dv, seq), per_bg),
                  pl.BlockSpec((1, 1, 1, 3, NSA_ROWS), per_q)],
        out_specs=pl.BlockSpec((1, 1, 1, d, NSA_ROWS), per_q),
        out_shape=jax.ShapeDtypeStruct((b, g, nqb, d, NSA_ROWS), F32),
        scratch_shapes=[pltpu.VMEM((nsel, Q_TILE), F32), pltpu.VMEM((SEL_GROUP, dk, NSA_ROWS), BF16),
                        pltpu.VMEM((ncp + 8, Q_TILE), F32)],
        compiler_params=_params(("parallel", "parallel", "arbitrary")),
        name="nsa",
    )(qT, kch, kcl, vcT, ks, vsT, kw, vwT, gT)


def _hgrn_chunk(q, f, v, lb, state, lower, gstart, dmat, off_mask):
    c = HG_CHUNK
    log_f = jnp.log(lb + (1.0 - lb) * _sigmoid(f))
    kk = (1.0 - lb) * _sigmoid(-f)
    l1, l2, l3 = _split3(log_f)
    bt = _dot(lower, l1) + _dot(lower, l2) + _dot(lower, l3)
    bs = _dot(gstart, l1) + _dot(gstart, l2) + _dot(gstart, l3)
    btT, bsT, kkT, qT = bt.T, bs.T, kk.T, q.T

    attT_diag = jnp.zeros((c, c), F32)
    for d in range(HG_SUB):
        if d == 0:
            prod = qT * kkT
        else:
            decay = jnp.exp(jnp.minimum(btT - pltpu.roll(btT, d, 1), 0.0))
            prod = (qT * pltpu.roll(kkT, d, 1)) * decay
        band = jnp.sum(prod, axis=0, keepdims=True)
        attT_diag = jnp.where(dmat == d, band, attT_diag)

    qh = (q * jnp.exp(bt - bs)).astype(BF16)
    blocks = [jnp.zeros((HG_SUB, c), F32)]
    for i in range(1, c // HG_SUB):
        b_i = bsT[:, i * HG_SUB:i * HG_SUB + 1]
        khT = kkT * jnp.exp(jnp.minimum(b_i - btT, 0.0))
        blocks.append(_dot(qh[i * HG_SUB:(i + 1) * HG_SUB], khT.astype(BF16)))
    att = jnp.where(off_mask, jnp.concatenate(blocks, axis=0), 0.0) + attT_diag.T

    vb = v.astype(BF16)
    o = _dot((q * jnp.exp(bt)).astype(BF16), state.astype(BF16)) + _dot(att.astype(BF16), vb)
    b_last = btT[:, c - 1:c]
    k_decT = kkT * jnp.exp(b_last - btT)
    return o, state * jnp.exp(b_last) + _dot(k_decT.astype(BF16), vb)


def _hgrn_kernel(q_ref, f_ref, v_ref, gate_ref, lb_ref, ng_ref, o_ref, state_ref):
    c = HG_CHUNK

    @pl.when(pl.program_id(2) == 0)
    def _():
        state_ref[...] = jnp.zeros_like(state_ref)

    row = lax.broadcasted_iota(jnp.int32, (c, c), 0)
    col = lax.broadcasted_iota(jnp.int32, (c, c), 1)
    sub = HG_SUB.bit_length() - 1
    row_blk = lax.shift_right_logical(row, sub)
    col_blk = lax.shift_right_logical(col, sub)
    lower = jnp.where(col <= row, 1.0, 0.0).astype(BF16)
    gstart = jnp.where(col_blk < row_blk, 1.0, 0.0).astype(BF16)
    dmat = jnp.where(row_blk == col_blk, col - row, -1)
    off_mask = col_blk < row_blk
    lb = lb_ref[0]
    ng = ng_ref[...]

    def step(i, state):
        rows = pl.ds(pl.multiple_of(i * c, c), c)
        o, state = _hgrn_chunk(q_ref[rows, :], f_ref[rows, :], v_ref[rows, :], lb, state,
                               lower, gstart, dmat, off_mask)
        gate = gate_ref[rows, :]
        ms = jnp.mean(o * o, axis=-1, keepdims=True)
        o_ref[rows, :] = o * lax.rsqrt(ms + RMS_EPS) * ng * (gate * _sigmoid(gate))
        return state

    state_ref[...] = lax.fori_loop(0, HG_ROWS // c, step, state_ref[...])


def _hgrn(h, lb, ng, bsz, seq, col0):
    nblk = seq // HG_ROWS
    grp = HG_COLS // HG_KEY_DIM
    blk = lambda k: pl.BlockSpec((HG_ROWS, HG_KEY_DIM), lambda i, j, c: (i * nblk + c, col0 + k * grp + j))
    return pl.pallas_call(
        _hgrn_kernel,
        grid=(bsz, HG_HEADS, nblk),
        in_specs=[blk(0), blk(1), blk(2), blk(3),
                  pl.BlockSpec((1, 1, HG_KEY_DIM), lambda i, j, c: (j, 0, 0)),
                  pl.BlockSpec((1, HG_VAL_DIM), lambda i, j, c: (0, 0))],
        out_specs=pl.BlockSpec((HG_ROWS, HG_VAL_DIM), lambda i, j, c: (i * nblk + c, j)),
        out_shape=jax.ShapeDtypeStruct((bsz * seq, HG_COLS), F32),
        scratch_shapes=[pltpu.VMEM((HG_KEY_DIM, HG_VAL_DIM), F32)],
        compiler_params=_params(("parallel", "parallel", "arbitrary")),
        name="hgrn",
    )(h, h, h, h, lb, ng)


def _mix_kernel(nsa_ref, hg_ref, x_ref, w_ref, ng_ref, g_ref, b_ref, o_ref):
    o_n = nsa_ref[...]
    half = o_n.shape[1]
    o_n = o_n * lax.rsqrt(jnp.mean(o_n * o_n, axis=-1, keepdims=True) + RMS_EPS) * ng_ref[...]
    mix = _dot(o_n.astype(BF16), w_ref[:half]) + _dot(hg_ref[...].astype(BF16), w_ref[half:])
    o_ref[...] = _layer_norm(DEEPNORM_ALPHA * x_ref[...] + mix, g_ref[...], b_ref[...])


def _mix(o_nsa, o_hg, x, w_out, ng, g, b, tm):
    t, d = x.shape
    half = o_nsa.shape[1]
    rows = lambda w: pl.BlockSpec((tm, w), lambda i: (i, 0))
    full = lambda r, c: pl.BlockSpec((r, c), lambda i: (0, 0))
    return pl.pallas_call(
        _mix_kernel,
        grid=(t // tm,),
        in_specs=[rows(half), rows(half), rows(d), full(d, d), full(1, half), full(1, d), full(1, d)],
        out_specs=rows(d),
        out_shape=jax.ShapeDtypeStruct((t, d), F32),
        compiler_params=_params(("parallel",)),
        name="mix",
    )(o_nsa, o_hg, x, w_out, ng, g, b)


def _xattn_kernel(x_ref, wq_ref, kT_ref, v_ref, wo_ref, g_ref, b_ref, o_ref):
    x = x_ref[0]
    d = x.shape[1]
    dh = d // XA_HEADS
    q = _dot(x.astype(BF16), wq_ref[...])
    heads = []
    for h in range(XA_HEADS):
        cols = slice(h * dh, (h + 1) * dh)
        s = _dot(q[:, cols].astype(BF16), kT_ref[0, cols, :]) * (dh ** -0.5)
        e = jnp.exp(s - jnp.max(s, axis=-1, keepdims=True))
        p = e * (1.0 / jnp.sum(e, axis=-1, keepdims=True))
        heads.append(_dot(p.astype(BF16), v_ref[0, :, cols]))
    o = jnp.concatenate(heads, axis=1)
    xa = _dot(o.astype(BF16), wo_ref[...])
    o_ref[0] = _layer_norm(DEEPNORM_ALPHA * x + xa, g_ref[...], b_ref[...])


def _xattn(x, wq, kT, v, wo, g, b, tm):
    bsz, seq, d = x.shape
    m = v.shape[1]
    full = lambda r, c: pl.BlockSpec((r, c), lambda i, j: (0, 0))
    return pl.pallas_call(
        _xattn_kernel,
        grid=(bsz, seq // tm),
        in_specs=[pl.BlockSpec((1, tm, d), lambda i, j: (i, j, 0)),
                  full(d, d),
                  pl.BlockSpec((1, d, m), lambda i, j: (i, 0, 0)),
                  pl.BlockSpec((1, m, d), lambda i, j: (i, 0, 0)),
                  full(d, d), full(1, d), full(1, d)],
        out_specs=pl.BlockSpec((1, tm, d), lambda i, j: (i, j, 0)),
        out_shape=jax.ShapeDtypeStruct((bsz, seq, d), F32),
        compiler_params=_params(("parallel", "parallel")),
        name="xattn",
    )(x, wq, kT, v, wo, g, b)


def _route_kernel(x_ref, wh_ref, wl_ref, bias_ref, ids_ref, wts_ref, rnk_ref, counts_ref, cnt_ref):
    xh, xl = _split2(x_ref[...])
    logits = _dot(xh, wh_ref[...]) + _dot(xh, wl_ref[...]) + _dot(xl, wh_ref[...]) + bias_ref[...]
    lane = lax.broadcasted_iota(jnp.int32, logits.shape, 1)
    lane_f = lane.astype(F32)
    big = float(LANES)

    is_g = lane < MOE_GROUPS
    g_max = jnp.max(jnp.where(is_g, logits, NEG_INF), axis=-1, keepdims=True)
    g_sum = jnp.sum(jnp.where(is_g, jnp.exp(logits - g_max), 0.0), axis=-1, keepdims=True)
    g_w = 1.0 / g_sum
    g_idx = jnp.min(jnp.where(is_g & (logits == g_max), lane_f, big), axis=-1, keepdims=True)

    e_lo = MOE_GROUPS + MOE_EXPERTS_PER_GROUP * g_idx
    is_e = (lane_f >= e_lo) & (lane_f < e_lo + MOE_EXPERTS_PER_GROUP)
    e_log = jnp.where(is_e, logits, NEG_INF)
    e_max = jnp.max(e_log, axis=-1, keepdims=True)
    e_exp = jnp.where(is_e, jnp.exp(logits - e_max), 0.0)
    e_sum = jnp.sum(e_exp, axis=-1, keepdims=True)
    i1 = jnp.min(jnp.where(is_e & (e_log == e_max), lane_f, big), axis=-1, keepdims=True)
    rest = jnp.where(lane_f == i1, NEG_INF, e_log)
    r_max = jnp.max(rest, axis=-1, keepdims=True)
    i2 = jnp.min(jnp.where(is_e & (lane_f != i1) & (rest == r_max), lane_f, big), axis=-1, keepdims=True)
    p1 = 1.0 / e_sum
    p2 = jnp.exp(r_max - e_max) / e_sum
    tot = p1 + p2
    w1 = g_w * (p1 / tot)
    w2 = g_w * (p2 / tot)
    ids = jnp.where(lane == 0, i1, i2) - float(MOE_GROUPS)
    ids_ref[...] = ids.astype(jnp.int32)
    wts_ref[...] = jnp.where(lane == 0, w1, w2)

    @pl.when(pl.program_id(0) == 0)
    def _():
        cnt_ref[...] = jnp.zeros_like(cnt_ref)

    tm = logits.shape[0]
    hit1 = lane_f == i1
    hit2 = lane_f == i2
    hits = jnp.where(hit1 | hit2, 1.0, 0.0)
    earlier = lax.broadcasted_iota(jnp.int32, (tm, tm), 1) < lax.broadcasted_iota(jnp.int32, (tm, tm), 0)
    before = _dot(jnp.where(earlier, 1.0, 0.0).astype(BF16), hits.astype(BF16)) + cnt_ref[...]
    rank1 = jnp.sum(jnp.where(hit1, before, 0.0), axis=-1, keepdims=True)
    rank2 = jnp.sum(jnp.where(hit2, before, 0.0), axis=-1, keepdims=True)
    rnk_ref[...] = jnp.where(lane == 0, rank1, rank2).astype(jnp.int32)
    cnt_ref[...] = cnt_ref[...] + jnp.sum(hits, axis=0, keepdims=True)
    counts_ref[...] = cnt_ref[...].astype(jnp.int32)


def _route(x, wh, wl, bias, tm):
    t, d = x.shape
    rows = lambda w: pl.BlockSpec((tm, w), lambda i: (i, 0))
    full = lambda r, c: pl.BlockSpec((r, c), lambda i: (0, 0))
    i32 = jnp.int32
    return pl.pallas_call(
        _route_kernel,
        grid=(t // tm,),
        in_specs=[rows(d), full(d, LANES), full(d, LANES), full(1, LANES)],
        out_specs=[rows(LANES), rows(LANES), rows(LANES), full(1, LANES)],
        out_shape=[jax.ShapeDtypeStruct((t, LANES), i32), jax.ShapeDtypeStruct((t, LANES), F32),
                   jax.ShapeDtypeStruct((t, LANES), i32), jax.ShapeDtypeStruct((1, LANES), i32)],
        scratch_shapes=[pltpu.VMEM((1, LANES), F32)],
        compiler_params=_params(("arbitrary",)),
        name="route",
    )(x, wh, wl, bias)


def _row_copy_wait(src, dst, sem, rows):
    pltpu.make_async_copy(src.at[pl.ds(0, rows)], dst.at[pl.ds(0, rows)], sem).wait()


def _dispatch_kernel(dest_ref, zfill_ref, x_ref, xs_hbm, zbuf, zsem, sem):
    i = pl.program_id(0)
    tm = x_ref.shape[0]

    @pl.when(i == 0)
    def _():
        zbuf[...] = jnp.zeros_like(zbuf)
        tail = lambda e: xs_hbm.at[pl.ds(pl.multiple_of(zfill_ref[e], MOE_BLOCK), MOE_BLOCK)]
        for e in range(zfill_ref.shape[0]):
            @pl.when(zfill_ref[e] >= 0)
            def _():
                pltpu.make_async_copy(zbuf, tail(e), zsem).start()
        for e in range(zfill_ref.shape[0]):
            @pl.when(zfill_ref[e] >= 0)
            def _():
                pltpu.make_async_copy(zbuf, tail(e), zsem).wait()

    def send(j, c):
        for k in range(MOE_TOP_K):
            slot = dest_ref[(i * tm + j) * MOE_TOP_K + k]
            pltpu.make_async_copy(x_ref.at[pl.ds(j, 1)], xs_hbm.at[pl.ds(slot, 1)], sem).start()
        return c

    lax.fori_loop(0, tm, send, 0, unroll=8)
    for _ in range(MOE_TOP_K):
        _row_copy_wait(x_ref, xs_hbm, sem, tm)


def _dispatch(dest, zfill, x, cap, tm):
    t, d = x.shape
    grid_spec = pltpu.PrefetchScalarGridSpec(
        num_scalar_prefetch=2,
        grid=(t // tm,),
        in_specs=[pl.BlockSpec((tm, d), lambda i, dest, zfill: (i, 0))],
        out_specs=pl.BlockSpec(memory_space=pl.ANY),
        scratch_shapes=[pltpu.VMEM((MOE_BLOCK, d), F32), pltpu.SemaphoreType.DMA, pltpu.SemaphoreType.DMA])
    return pl.pallas_call(
        _dispatch_kernel,
        grid_spec=grid_spec,
        out_shape=jax.ShapeDtypeStruct((cap, d), F32),
        compiler_params=_params(("arbitrary",)),
        name="dispatch",
    )(dest, zfill, x)


def _experts_kernel(bexp_ref, nact_ref, xs_ref, wg_ref, wu_ref, wd_ref, y_ref):
    @pl.when(pl.program_id(0) < nact_ref[0])
    def _():
        xb = xs_ref[...].astype(BF16)
        gate = _dot(xb, wg_ref[0])
        hid = (gate * _sigmoid(gate)) * _dot(xb, wu_ref[0])
        y_ref[...] = _dot(hid.astype(BF16), wd_ref[0])

    @pl.when(pl.program_id(0) >= nact_ref[0])
    def _():
        y_ref[...] = jnp.zeros_like(y_ref)


def _experts(bexp, nact, xs, wg, wu, wd):
    cap, d = xs.shape
    dff = wg.shape[-1]
    by_expert = lambda i, bexp, nact: (bexp[i], 0, 0)
    active = lambda i, bexp, nact: (jnp.minimum(i, nact[0] - 1), 0)
    grid_spec = pltpu.PrefetchScalarGridSpec(
        num_scalar_prefetch=2,
        grid=(cap // MOE_BLOCK,),
        in_specs=[pl.BlockSpec((MOE_BLOCK, d), active),
                  pl.BlockSpec((1, d, dff), by_expert),
                  pl.BlockSpec((1, d, dff), by_expert),
                  pl.BlockSpec((1, dff, d), by_expert)],
        out_specs=pl.BlockSpec((MOE_BLOCK, d), lambda i, bexp, nact: (i, 0)))
    return pl.pallas_call(
        _experts_kernel,
        grid_spec=grid_spec,
        out_shape=jax.ShapeDtypeStruct((cap, d), F32),
        compiler_params=_params(("arbitrary",)),
        name="experts",
    )(bexp, nact, xs, wg, wu, wd)


def _combine_kernel(dest_ref, y_hbm, x_ref, wts_ref, g_ref, b_ref, o_ref, ybuf, sem):
    i = pl.program_id(0)
    n = pl.num_programs(0)
    tm = x_ref.shape[0]
    cur = lax.rem(i, 2)

    def fetch(tile, buf):
        def one(j, c):
            for k in range(MOE_TOP_K):
                slot = dest_ref[(tile * tm + j) * MOE_TOP_K + k]
                pltpu.make_async_copy(y_hbm.at[pl.ds(slot, 1)], ybuf.at[buf, k, pl.ds(j, 1)], sem.at[buf]).start()
            return c
        lax.fori_loop(0, tm, one, 0, unroll=8)

    @pl.when(i == 0)
    def _():
        fetch(0, 0)

    @pl.when(i + 1 < n)
    def _():
        fetch(i + 1, 1 - cur)

    for k in range(MOE_TOP_K):
        _row_copy_wait(y_hbm, ybuf.at[cur, k], sem.at[cur], tm)
    w = wts_ref[...]
    ff = w[:, 0:1] * ybuf[cur, 0]
    for k in range(1, MOE_TOP_K):
        ff = ff + w[:, k:k + 1] * ybuf[cur, k]
    o_ref[...] = _layer_norm(DEEPNORM_ALPHA * x_ref[...] + ff, g_ref[...], b_ref[...])


def _combine(dest, y, x, wts, g, b, tm):
    t, d = x.shape
    rows = lambda w: pl.BlockSpec((tm, w), lambda i, dest: (i, 0))
    full = lambda r, c: pl.BlockSpec((r, c), lambda i, dest: (0, 0))
    grid_spec = pltpu.PrefetchScalarGridSpec(
        num_scalar_prefetch=1,
        grid=(t // tm,),
        in_specs=[pl.BlockSpec(memory_space=pl.ANY), rows(d), rows(LANES), full(1, d), full(1, d)],
        out_specs=rows(d),
        scratch_shapes=[pltpu.VMEM((2, MOE_TOP_K, tm, d), F32), pltpu.SemaphoreType.DMA((2,))])
    return pl.pallas_call(
        _combine_kernel,
        grid_spec=grid_spec,
        out_shape=jax.ShapeDtypeStruct((t, d), F32),
        compiler_params=_params(("arbitrary",)),
        name="combine",
    )(dest, y, x, wts, g, b)


def _dispatch_plan(ids, rnk, counts, n_tok):
    padded = ((counts + MOE_BLOCK - 1) // MOE_BLOCK) * MOE_BLOCK
    pend = jnp.cumsum(padded)
    pstart = pend - padded
    dest = (pstart[ids] + rnk).reshape(-1).astype(jnp.int32)
    cap = n_tok * MOE_TOP_K + MOE_N_EXPERTS * MOE_BLOCK
    n_blocks = cap // MOE_BLOCK
    block_start = jnp.arange(n_blocks, dtype=jnp.int32) * MOE_BLOCK
    bexp = jnp.minimum(jnp.sum(block_start[:, None] >= pend[None, :], axis=1), MOE_N_EXPERTS - 1).astype(jnp.int32)
    nact = (pend[-1] // MOE_BLOCK).astype(jnp.int32).reshape(1)
    tails = jnp.where(padded > 0, pend - MOE_BLOCK, -1)
    idle = pend[-1] + block_start[:MOE_N_EXPERTS]
    zfill = jnp.concatenate([tails, jnp.where(idle < cap, idle, -1)]).astype(jnp.int32)
    return dest, bexp, nact, zfill, cap


def kernel(x, mem, w_in, cmp_pe_k, cmp_pe_v, cmp_w1_k, cmp_w2_k, cmp_w1_v, cmp_w2_v, nsa_norm_g,
           hg_lb_logits, hg_norm_g, w_out, ln1_g, ln1_b, xa_wq, xa_wk, xa_wv, xa_wo, ln2_g, ln2_b,
           moe_w_group, moe_b_group, moe_w_expert, moe_b_expert, moe_w_gate, moe_w_up, moe_w_down,
           ln3_g, ln3_b):
    b, s, d = x.shape
    t = b * s
    g, dh, hpg = NSA_KV_GROUPS, NSA_HEAD_DIM, NSA_HPG
    nqb = s // Q_TILE
    nch = s // CMP_STRIDE
    row = lambda a: a.reshape(1, -1)
    lb_all = jnp.cumsum(jax.nn.softmax(hg_lb_logits.astype(F32), axis=0), axis=0)
    xt = x.reshape(t, d)
    for l in range(DEPTH):
        nsa_cols = NSA_Q_COLS + 6 * NSA_KV_COLS
        w_l = w_in[l]
        w_perm = jnp.concatenate(
            [w_l[:, :nsa_cols], w_l[:, nsa_cols + NSA_GATE_COLS:], w_l[:, nsa_cols:nsa_cols + NSA_GATE_COLS],
             jnp.zeros((d, LANES - NSA_GATE_COLS), w_l.dtype)], axis=1)
        h = _matmul(xt, w_perm.astype(BF16), 256)
        offs = [0]
        for n in (NSA_Q_COLS,) + (NSA_KV_COLS,) * 6:
            offs.append(offs[-1] + n)
        (nq, nkc, nvc, nks, nvs, nkw, nvw) = [h[:, offs[i]:offs[i + 1]] for i in range(7)]
        hg_col0 = nsa_cols // LANES
        gate0 = nsa_cols + 4 * HG_COLS
        ngate = h[:, gate0:gate0 + NSA_GATE_COLS]

        chunks = lambda a: jnp.transpose(a.reshape(b, nch, CMP_STRIDE, g, dh), (0, 3, 1, 2, 4)).reshape(
            b, g, nch, CMP_STRIDE * dh)
        comp = _compress(jnp.stack([chunks(nkc), chunks(nvc)]),
                         jnp.stack([cmp_pe_k[l].reshape(1, -1), cmp_pe_v[l].reshape(1, -1)]),
                         jnp.stack([cmp_w1_k[l], cmp_w1_v[l]]),
                         jnp.stack([cmp_w2_k[l], cmp_w2_v[l]]))
        kch, kcl = _split2(comp[0])
        vcT = jnp.swapaxes(comp[1], -1, -2).astype(BF16)

        keys = lambda a: jnp.transpose(a.reshape(b, s, g, dh), (0, 2, 1, 3))
        tile_block = (jnp.arange(s) % SEL_KV_TILE) // SEL_BLOCK
        block_onehot = (tile_block[:, None] == jnp.arange(LANES - dh)[None, :]).astype(F32)
        keys_ext = lambda a: jnp.concatenate(
            [keys(a), jnp.broadcast_to(block_onehot, (b, g, s, LANES - dh))], axis=-1).astype(BF16)
        ones_row = jnp.zeros((b, g, NSA_V_ROWS - dh, s), F32).at[:, :, 0].set(1.0)
        valsT = lambda a: jnp.concatenate(
            [jnp.transpose(a.reshape(b, s, g, dh), (0, 2, 3, 1)), ones_row], axis=2).astype(BF16)
        qT = jnp.transpose(nq.reshape(b, nqb, Q_TILE, g, hpg, dh), (0, 3, 1, 5, 4, 2)).reshape(
            b, g, nqb, dh, NSA_ROWS)
        gT = jnp.transpose(ngate.reshape(b, nqb, Q_TILE, g, hpg, 3), (0, 3, 1, 5, 4, 2)).reshape(
            b, g, nqb, 3, NSA_ROWS)
        oT = _nsa(qT, kch, kcl, vcT, keys_ext(nks), valsT(nvs), keys(nkw).astype(BF16), valsT(nvw), gT, s)
        o_nsa = jnp.transpose(oT.reshape(b, g, nqb, dh, hpg, Q_TILE), (0, 2, 5, 1, 4, 3)).reshape(
            t, NSA_Q_COLS)

        o_hg = _hgrn(h, lb_all[l].reshape(HG_HEADS, 1, HG_KEY_DIM), row(hg_norm_g[l]), b, s, hg_col0)

        x1 = _mix(o_nsa, o_hg, xt, w_out[l].astype(BF16), row(nsa_norm_g[l]), row(ln1_g[l]), row(ln1_b[l]), 256)

        n_mem = mem.shape[1]
        kv = _matmul(mem.reshape(b * n_mem, d),
                     jnp.concatenate([xa_wk[l], xa_wv[l]], axis=1).astype(BF16), n_mem)
        kT = jnp.swapaxes(kv[:, :d].reshape(b, n_mem, d), 1, 2).astype(BF16)
        v = kv[:, d:].reshape(b, n_mem, d).astype(BF16)
        x2 = _xattn(x1.reshape(b, s, d), xa_wq[l].astype(BF16), kT, v, xa_wo[l].astype(BF16),
                    row(ln2_g[l]), row(ln2_b[l]), 256).reshape(t, d)

        w_r = jnp.concatenate([moe_w_group[l], moe_w_expert[l]], axis=1)
        w_r = jnp.pad(w_r, ((0, 0), (0, LANES - w_r.shape[1])))
        b_r = jnp.pad(jnp.concatenate([moe_b_group[l], moe_b_expert[l]]), (0, LANES - MOE_GROUPS - MOE_N_EXPERTS))
        w_rh, w_rl = _split2(w_r)
        ids, wts, rnk, counts = _route(x2, w_rh, w_rl, row(b_r), 512)
        dest, bexp, nact, zfill, cap = _dispatch_plan(
            ids[:, :MOE_TOP_K], rnk[:, :MOE_TOP_K], counts[0, MOE_GROUPS:MOE_GROUPS + MOE_N_EXPERTS], t)
        xs = _dispatch(dest, zfill, x2, cap, 256)
        ys = _experts(bexp, nact, xs, moe_w_gate[l].astype(BF16), moe_w_up[l].astype(BF16),
                      moe_w_down[l].astype(BF16))
        xt = _combine(dest, ys, x2, wts, row(ln3_g[l]), row(ln3_b[l]), 256)
    return xt.reshape(b, s, d)
```

```python
import functools

import jax
import jax.numpy as jnp
from jax import lax
from jax.experimental import pallas as pl
from jax.experimental.pallas import tpu as pltpu

F32 = jnp.float32
BF16 = jnp.bfloat16

NSA_HEAD_DIM = 64
NSA_HEADS = 8
NSA_KV_GROUPS = 2
NSA_HPG = NSA_HEADS // NSA_KV_GROUPS
CMP_BLOCK = 32
CMP_STRIDE = 16
CMP_HIDDEN = 256
SEL_BLOCK = 64
SEL_TOP_N = 16
WINDOW = 512
FORCE_BONUS = 1.0e4
HG_KEY_DIM = 128
HG_VAL_DIM = 128
HG_HEADS = 4
XA_HEADS = 4
MOE_GROUPS = 4
MOE_EXPERTS_PER_GROUP = 8
MOE_N_EXPERTS = MOE_GROUPS * MOE_EXPERTS_PER_GROUP
MOE_TOP_K = 2
MOE_BLOCK = 128
DEPTH = 1
DEEPNORM_ALPHA = (2.0 * DEPTH) ** 0.25
LN_EPS = 1e-5
RMS_EPS = 1e-6
NEG_INF = -1e30

NSA_Q_COLS = NSA_HEADS * NSA_HEAD_DIM
NSA_KV_COLS = NSA_KV_GROUPS * NSA_HEAD_DIM
NSA_GATE_COLS = NSA_HEADS * 3
HG_COLS = HG_HEADS * HG_KEY_DIM

LANES = 128
Q_TILE = 128
NSA_ROWS = NSA_HPG * Q_TILE
SEL_KV_TILE = 512
SEL_GROUP = 4
NSA_V_ROWS = NSA_HEAD_DIM + 16
LOG2_E = 1.4426950408889634
HG_CHUNK = 128
HG_SUB = 16
HG_ROWS = 512
HG_HEADS_PER_STEP = 2
VMEM_LIMIT = 48 * 1024 * 1024


def _dot(a, b):
    return jnp.dot(a, b, preferred_element_type=F32)


def _dot_nt(a, b):
    return lax.dot_general(a, b, (((1,), (1,)), ((), ())), preferred_element_type=F32)


def _split2(a):
    hi = a.astype(BF16)
    lo = (a - hi.astype(F32)).astype(BF16)
    return hi, lo


def _split3(a):
    p1 = a.astype(BF16)
    r1 = a - p1.astype(F32)
    p2 = r1.astype(BF16)
    p3 = (r1 - p2.astype(F32)).astype(BF16)
    return p1, p2, p3


def _dot3(a, b):
    ah, al = _split2(a)
    bh, bl = _split2(b)
    return _dot(ah, bh) + _dot(ah, bl) + _dot(al, bh)


def _sigmoid(x):
    return 1.0 / (1.0 + jnp.exp(-x))


def _layer_norm(y, g, b):
    mu = jnp.mean(y, axis=-1, keepdims=True)
    d = y - mu
    var = jnp.mean(d * d, axis=-1, keepdims=True)
    return d * lax.rsqrt(var + LN_EPS) * g + b


def _params(sem):
    return pltpu.CompilerParams(dimension_semantics=sem, vmem_limit_bytes=VMEM_LIMIT)


def _matmul_kernel(x_ref, w_ref, o_ref):
    o_ref[...] = _dot(x_ref[...].astype(BF16), w_ref[...]).astype(o_ref.dtype)


def _matmul(x, w, tm):
    m, k = x.shape
    n = w.shape[1]
    return pl.pallas_call(
        _matmul_kernel,
        grid=(m // tm,),
        in_specs=[pl.BlockSpec((tm, k), lambda i: (i, 0)),
                  pl.BlockSpec((k, n), lambda i: (0, 0))],
        out_specs=pl.BlockSpec((tm, n), lambda i: (i, 0)),
        out_shape=jax.ShapeDtypeStruct((m, n), F32),
        compiler_params=_params(("parallel",)),
        name="proj",
    )(x, w)


def _inproj_kernel(x_ref, w_ref, h_ref, ks_ref, kw_ref, vsT_ref, vwT_ref, *, tiles_per_seq):
    h = _dot(x_ref[...].astype(BF16), w_ref[...])
    h_ref[...] = h
    tm = h.shape[0]
    dh = NSA_HEAD_DIM
    k_sel0 = NSA_Q_COLS + 2 * NSA_KV_COLS
    v_sel0, k_win0, v_win0 = k_sel0 + NSA_KV_COLS, k_sel0 + 2 * NSA_KV_COLS, k_sel0 + 3 * NSA_KV_COLS
    pos = lax.rem(pl.program_id(0), tiles_per_seq) * tm + lax.broadcasted_iota(jnp.int32, (tm, 1), 0)
    blk = lax.shift_right_logical(pos & (SEL_KV_TILE - 1), SEL_BLOCK.bit_length() - 1)
    onehot = jnp.where(blk == lax.broadcasted_iota(jnp.int32, (1, LANES - dh), 1), 1.0, 0.0)
    ones_row = jnp.where(lax.broadcasted_iota(jnp.int32, (NSA_V_ROWS - dh, tm), 0) == 0, 1.0, 0.0)
    vsT = h[:, v_sel0:v_sel0 + NSA_KV_COLS].T
    vwT = h[:, v_win0:v_win0 + NSA_KV_COLS].T
    for g in range(NSA_KV_GROUPS):
        cols = slice(g * dh, (g + 1) * dh)
        ks_ref[0, g] = jnp.concatenate([h[:, k_sel0 + g * dh:k_sel0 + (g + 1) * dh], onehot], axis=1).astype(BF16)
        kw_ref[0, g] = h[:, k_win0 + g * dh:k_win0 + (g + 1) * dh].astype(BF16)
        vsT_ref[0, g] = jnp.concatenate([vsT[cols], ones_row], axis=0).astype(BF16)
        vwT_ref[0, g] = jnp.concatenate([vwT[cols], ones_row], axis=0).astype(BF16)


def _inproj(x, w, bsz, seq, tm):
    t, k = x.shape
    n = w.shape[1]
    g, dh = NSA_KV_GROUPS, NSA_HEAD_DIM
    tps = seq // tm
    keys = lambda width: pl.BlockSpec((1, g, tm, width), lambda i: (i // tps, 0, i % tps, 0))
    vals = pl.BlockSpec((1, g, NSA_V_ROWS, tm), lambda i: (i // tps, 0, 0, i % tps))
    return pl.pallas_call(
        functools.partial(_inproj_kernel, tiles_per_seq=tps),
        grid=(t // tm,),
        in_specs=[pl.BlockSpec((tm, k), lambda i: (i, 0)),
                  pl.BlockSpec((k, n), lambda i: (0, 0))],
        out_specs=[pl.BlockSpec((tm, n), lambda i: (i, 0)), keys(LANES), keys(dh), vals, vals],
        out_shape=[jax.ShapeDtypeStruct((t, n), F32),
                   jax.ShapeDtypeStruct((bsz, g, seq, LANES), BF16),
                   jax.ShapeDtypeStruct((bsz, g, seq, dh), BF16),
                   jax.ShapeDtypeStruct((bsz, g, NSA_V_ROWS, seq), BF16),
                   jax.ShapeDtypeStruct((bsz, g, NSA_V_ROWS, seq), BF16)],
        compiler_params=_params(("parallel",)),
        name="inproj",
    )(x, w)


def _compress_kernel(ch_ref, pe_ref, w1_ref, w2_ref, o_ref):
    ch = ch_ref[0, 0, 0]
    half = ch.shape[1]
    nch = ch.shape[0]
    pe = pe_ref[0]
    w1 = w1_ref[0]
    top = _dot3(ch + pe[:, :half], w1[:half])
    bot = _dot3(ch + pe[:, half:], w1[half:])
    hid = top + pltpu.roll(bot, nch - 1, 0)
    c = 0.7978845608028654
    act = 0.5 * hid * (1.0 + jnp.tanh(c * (hid + 0.044715 * hid * hid * hid)))
    out = _dot3(act, w2_ref[0])
    row = lax.broadcasted_iota(jnp.int32, out.shape, 0)
    o_ref[0, 0, 0] = jnp.where(row < nch - 1, out, 0.0)


def _compress(ch, pe, w1, w2):
    _, b, g, nch, width = ch.shape
    hidden = w1.shape[-1]
    d = w2.shape[-1]
    return pl.pallas_call(
        _compress_kernel,
        grid=(2, b, g),
        in_specs=[pl.BlockSpec((1, 1, 1, nch, width), lambda a, i, j: (a, i, j, 0, 0)),
                  pl.BlockSpec((1, 1, 2 * width), lambda a, i, j: (a, 0, 0)),
                  pl.BlockSpec((1, 2 * width, hidden), lambda a, i, j: (a, 0, 0)),
                  pl.BlockSpec((1, hidden, d), lambda a, i, j: (a, 0, 0))],
        out_specs=pl.BlockSpec((1, 1, 1, nch, d), lambda a, i, j: (a, i, j, 0, 0)),
        out_shape=jax.ShapeDtypeStruct((2, b, g, nch, d), F32),
        compiler_params=_params(("parallel", "parallel", "parallel")),
        name="compress",
    )(ch, pe, w1, w2)


def _tile_heads(a):
    return jnp.concatenate([a] * NSA_HPG, axis=1)


def _nsa_kernel(q_ref, kch_ref, kcl_ref, vcT_ref, ks_ref, vsT_ref, kw_ref, vwT_ref, gate_ref,
                o_ref, sel_ref, qx_ref, psum_ref, gt_ref, *, seq):
    s0 = pl.program_id(2) * Q_TILE
    nsel = seq // SEL_BLOCK
    ncmp_pad = seq // CMP_STRIDE
    dh = NSA_HEAD_DIM

    q_t = q_ref[...].T
    q = jnp.concatenate([q_t[r * dh:(r + 1) * dh] for r in range(NSA_HPG)], axis=1)
    q = q * (dh ** -0.5 * LOG2_E)
    q_hi, q_lo = _split2(q)
    t_q = s0 + lax.broadcasted_iota(jnp.int32, (1, Q_TILE), 1)
    t_all = _tile_heads(t_q)

    kch = kch_ref[0, 0]
    sc = _dot(kch, q_hi) + _dot(kch, q_lo) + _dot(kcl_ref[0, 0], q_hi)
    n_end = lax.broadcasted_iota(jnp.int32, (ncmp_pad, 1), 0) * CMP_STRIDE + (CMP_BLOCK - 1)
    sc = sc + _tile_heads(jnp.where(n_end <= t_q, 0.0, NEG_INF))
    m_c = jnp.max(sc, axis=0, keepdims=True)
    e_c = jnp.exp2(sc - m_c)
    l_c = jnp.sum(e_c, axis=0, keepdims=True)
    p_c = e_c * jnp.where(t_all >= CMP_BLOCK - 1, 1.0 / l_c, 0.0)
    o_c = _dot(vcT_ref[0, 0], p_c.astype(BF16))

    p_sum = p_c[:, 0:Q_TILE]
    for r in range(1, NSA_HPG):
        p_sum = p_sum + p_c[:, r * Q_TILE:(r + 1) * Q_TILE]
    ratio = SEL_BLOCK // CMP_STRIDE
    first_row = 8 - (CMP_BLOCK // CMP_STRIDE - 1)
    psum_ref[0:8, :] = jnp.zeros((8, Q_TILE), F32)
    psum_ref[8:, :] = p_sum
    imp = psum_ref[pl.ds(first_row, nsel, stride=ratio), :]
    for m in range(1, ratio + CMP_BLOCK // CMP_STRIDE - 1):
        imp = imp + psum_ref[pl.ds(first_row + m, nsel, stride=ratio), :]
    j_idx = lax.broadcasted_iota(jnp.int32, (nsel, Q_TILE), 0)
    cur = lax.shift_right_logical(t_q, SEL_BLOCK.bit_length() - 1)
    forced = (j_idx == 0) | (j_idx == cur) | (j_idx == cur - 1)
    score = jnp.where(j_idx <= cur, imp + jnp.where(forced, FORCE_BONUS, 0.0), -1.0)
    j_f = j_idx.astype(F32)
    sel = jnp.zeros((nsel, Q_TILE), F32)
    for _ in range(min(SEL_TOP_N, nsel)):
        best = jnp.max(score, axis=0, keepdims=True)
        first = jnp.min(jnp.where(score == best, j_f, float(nsel)), axis=0, keepdims=True)
        pick = j_f == first
        sel = jnp.where(pick, 1.0, sel)
        score = jnp.where(pick, -3.0e38, score)
    sel_ref[...] = sel

    blocks_per_tile = SEL_KV_TILE // SEL_BLOCK
    for slot in range(SEL_GROUP):
        qx_ref[slot, 0:dh, :] = q_hi
        qx_ref[slot, dh:, :] = jnp.zeros((qx_ref.shape[1] - dh, NSA_ROWS), BF16)
    k_off = lax.broadcasted_iota(jnp.int32, (SEL_KV_TILE, 1), 0)

    def sel_scores(j, slot, causal):
        base = pl.multiple_of(j * SEL_KV_TILE, SEL_KV_TILE)
        member = sel_ref[pl.ds(pl.multiple_of(j * blocks_per_tile, blocks_per_tile), blocks_per_tile), :]
        bias = _tile_heads(jnp.where(member > 0.5, 0.0, NEG_INF))
        qx_ref[slot, dh:dh + 2 * blocks_per_tile, :] = jnp.concatenate(
            [bias, jnp.zeros_like(bias)], axis=0).astype(BF16)
        s = _dot(ks_ref[0, 0, pl.ds(base, SEL_KV_TILE), :], qx_ref[slot])
        if causal:
            s = s + _tile_heads(jnp.where((base + k_off) <= t_q, 0.0, NEG_INF))
        return s

    def sel_update(j, s, m_i, acc):
        base = pl.multiple_of(j * SEL_KV_TILE, SEL_KV_TILE)
        m_new = jnp.maximum(m_i, jnp.max(s, axis=0, keepdims=True))
        p = jnp.exp2(s - m_new).astype(BF16)
        pv = _dot(vsT_ref[0, 0, :, pl.ds(base, SEL_KV_TILE)], p)
        return m_new, jnp.exp2(m_i - m_new) * acc + pv

    def sel_group(i, carry, causal):
        tiles = [SEL_GROUP * i + u for u in range(SEL_GROUP)]
        scores = [sel_scores(j, u, causal) for u, j in enumerate(tiles)]
        for j, s in zip(tiles, scores):
            carry = sel_update(j, s, *carry)
        return carry

    group_keys = SEL_GROUP * SEL_KV_TILE
    n_groups = lax.div(s0 + (Q_TILE + group_keys - 1), group_keys)
    init = (jnp.full((1, NSA_ROWS), NEG_INF, F32), jnp.zeros((vsT_ref.shape[2], NSA_ROWS), F32))
    carry = lax.fori_loop(0, n_groups - 1, lambda i, c: sel_group(i, c, False), init)
    _, acc_s = sel_group(n_groups - 1, carry, True)
    o_s = acc_s[0:dh] * (1.0 / acc_s[dh:dh + 1])

    span = WINDOW + Q_TILE
    lo = pl.multiple_of(jnp.maximum(s0 - WINDOW, 0), Q_TILE)
    kpos = lo + lax.broadcasted_iota(jnp.int32, (span, 1), 0)
    sw = _dot(kw_ref[0, 0, pl.ds(lo, span), :], q_hi)
    sw = sw + _tile_heads(jnp.where((kpos <= t_q) & (kpos > t_q - WINDOW), 0.0, NEG_INF))
    e_w = jnp.exp2(sw - jnp.max(sw, axis=0, keepdims=True)).astype(BF16)
    acc_w = _dot(vwT_ref[0, 0, :, pl.ds(lo, span)], e_w)
    o_w = acc_w[0:dh] * (1.0 / acc_w[dh:dh + 1])

    gt_ref[...] = gate_ref[...].T
    g_row0 = pl.program_id(1) * (NSA_HPG * 3)

    def gate(branch):
        return _sigmoid(jnp.concatenate(
            [gt_ref[pl.ds(g_row0 + 3 * r + branch, 1), :] for r in range(NSA_HPG)], axis=1))

    o = gate(0) * o_c + gate(1) * o_s + gate(2) * o_w
    o_rows = jnp.concatenate([o[:, r * Q_TILE:(r + 1) * Q_TILE] for r in range(NSA_HPG)], axis=0)
    o_ref[...] = o_rows.T


def _nsa(h, kch, kcl, vcT, ks, vsT, kw, vwT, seq, gate_tile):
    b, g = ks.shape[:2]
    nqb = seq // Q_TILE
    d = NSA_HEAD_DIM
    ncp = seq // CMP_STRIDE
    nsel = seq // SEL_BLOCK
    dk = ks.shape[-1]
    dv = vsT.shape[2]
    per_bg = lambda i, j, k: (i, j, 0, 0)
    q_rows = lambda i, j, k: (i * nqb + k, j)
    return pl.pallas_call(
        functools.partial(_nsa_kernel, seq=seq),
        grid=(b, g, nqb),
        in_specs=[pl.BlockSpec((Q_TILE, NSA_HPG * d), q_rows),
                  pl.BlockSpec((1, 1, ncp, d), per_bg),
                  pl.BlockSpec((1, 1, ncp, d), per_bg),
                  pl.BlockSpec((1, 1, d, ncp), per_bg),
                  pl.BlockSpec((1, 1, seq, dk), per_bg),
                  pl.BlockSpec((1, 1, dv, seq), per_bg),
                  pl.BlockSpec((1, 1, seq, d), per_bg),
                  pl.BlockSpec((1, 1, dv, seq), per_bg),
                  pl.BlockSpec((Q_TILE, LANES), lambda i, j, k: (i * nqb + k, gate_tile))],
        out_specs=pl.BlockSpec((Q_TILE, NSA_HPG * d), q_rows),
        out_shape=jax.ShapeDtypeStruct((b * seq, NSA_Q_COLS), F32),
        scratch_shapes=[pltpu.VMEM((nsel, Q_TILE), F32), pltpu.VMEM((SEL_GROUP, dk, NSA_ROWS), BF16),
                        pltpu.VMEM((ncp + 8, Q_TILE), F32), pltpu.VMEM((LANES, Q_TILE), F32)],
        compiler_params=_params(("parallel", "parallel", "arbitrary")),
        name="nsa",
    )(h, kch, kcl, vcT, ks, vsT, kw, vwT, h)


def _hgrn_chunk(q, f, v, lb, state_t, shift_ref, lower, gstart, dmat, off_mask):
    c = HG_CHUNK
    log_f = jnp.log(lb + (1.0 - lb) * _sigmoid(f))
    kk = (1.0 - lb) * _sigmoid(-f)
    l1, l2, l3 = _split3(log_f)
    bt = _dot(lower, l1) + _dot(lower, l2) + _dot(lower, l3)
    bs = _dot(gstart, l1) + _dot(gstart, l2) + _dot(gstart, l3)

    shift_ref[0, HG_SUB:, :] = kk
    shift_ref[1, HG_SUB:, :] = bt
    ones = jnp.ones((kk.shape[1], c), BF16)
    att = jnp.zeros((c, c), F32)
    for d in range(HG_SUB):
        if d == 0:
            prod = q * kk
        else:
            rows = pl.ds(HG_SUB - d, c)
            prod = (q * shift_ref[0, rows, :]) * jnp.exp(jnp.minimum(bt - shift_ref[1, rows, :], 0.0))
        band = _dot(prod.astype(BF16), ones)
        att = jnp.where(dmat == d, band, att)

    qh = (q * jnp.exp(bt - bs)).astype(BF16)
    blocks = [jnp.zeros((HG_SUB, c), F32)]
    for i in range(1, c // HG_SUB):
        b_i = bs[i * HG_SUB:i * HG_SUB + 1, :]
        kh = kk * jnp.exp(jnp.minimum(b_i - bt, 0.0))
        blocks.append(_dot_nt(qh[i * HG_SUB:(i + 1) * HG_SUB], kh.astype(BF16)))
    att = jnp.where(off_mask, jnp.concatenate(blocks, axis=0), att)

    vb = v.astype(BF16)
    o = _dot_nt((q * jnp.exp(bt)).astype(BF16), state_t.astype(BF16)) + _dot(att.astype(BF16), vb)
    b_last = bt[c - 1:c, :]
    k_dec = kk * jnp.exp(b_last - bt)
    return o, state_t * jnp.exp(b_last) + _dot(v.T.astype(BF16), k_dec.astype(BF16))


def _hgrn_kernel(q_ref, f_ref, v_ref, gate_ref, lb_ref, ng_ref, o_ref, state_ref, shift_ref):
    c = HG_CHUNK

    @pl.when(pl.program_id(2) == 0)
    def _():
        state_ref[...] = jnp.zeros_like(state_ref)
        shift_ref[...] = jnp.zeros_like(shift_ref)

    row = lax.broadcasted_iota(jnp.int32, (c, c), 0)
    col = lax.broadcasted_iota(jnp.int32, (c, c), 1)
    sub = HG_SUB.bit_length() - 1
    row_blk = lax.shift_right_logical(row, sub)
    col_blk = lax.shift_right_logical(col, sub)
    lower = jnp.where(col <= row, 1.0, 0.0).astype(BF16)
    gstart = jnp.where(col_blk < row_blk, 1.0, 0.0).astype(BF16)
    dmat = jnp.where(row_blk == col_blk, row - col, -1)
    off_mask = col_blk < row_blk
    ng = ng_ref[...]

    def step(i, states):
        rows = pl.ds(pl.multiple_of(i * c, c), c)
        new_states = []
        for hd in range(HG_HEADS_PER_STEP):
            lanes = slice(hd * HG_KEY_DIM, (hd + 1) * HG_KEY_DIM)
            o, state = _hgrn_chunk(q_ref[rows, lanes], f_ref[rows, lanes], v_ref[rows, lanes], lb_ref[hd],
                                   states[hd], shift_ref.at[hd], lower, gstart, dmat, off_mask)
            gate = gate_ref[rows, lanes]
            ms = jnp.mean(o * o, axis=-1, keepdims=True)
            o_ref[rows, lanes] = o * lax.rsqrt(ms + RMS_EPS) * ng * (gate * _sigmoid(gate))
            new_states.append(state)
        return tuple(new_states)

    init = tuple(state_ref[hd] for hd in range(HG_HEADS_PER_STEP))
    for hd, state in enumerate(lax.fori_loop(0, HG_ROWS // c, step, init)):
        state_ref[hd] = state


def _hgrn(h, lb, ng, bsz, seq, col0):
    nblk = seq // HG_ROWS
    hps = HG_HEADS_PER_STEP
    width = hps * HG_KEY_DIM
    tiles = HG_COLS // width
    blk = lambda k: pl.BlockSpec((HG_ROWS, width),
                                 lambda i, j, c: (i * nblk + c, (col0 * LANES) // width + k * tiles + j))
    return pl.pallas_call(
        _hgrn_kernel,
        grid=(bsz, HG_HEADS // hps, nblk),
        in_specs=[blk(0), blk(1), blk(2), blk(3),
                  pl.BlockSpec((hps, 1, HG_KEY_DIM), lambda i, j, c: (j, 0, 0)),
                  pl.BlockSpec((1, HG_VAL_DIM), lambda i, j, c: (0, 0))],
        out_specs=pl.BlockSpec((HG_ROWS, width), lambda i, j, c: (i * nblk + c, j)),
        out_shape=jax.ShapeDtypeStruct((bsz * seq, HG_COLS), F32),
        scratch_shapes=[pltpu.VMEM((hps, HG_VAL_DIM, HG_KEY_DIM), F32),
                        pltpu.VMEM((hps, 2, HG_SUB + HG_CHUNK, HG_KEY_DIM), F32)],
        compiler_params=_params(("parallel", "parallel", "arbitrary")),
        name="hgrn",
    )(h, h, h, h, lb, ng)


def _mix_kernel(nsa_ref, hg_ref, x_ref, w_ref, ng_ref, g_ref, b_ref, o_ref):
    o_n = nsa_ref[...]
    half = o_n.shape[1]
    o_n = o_n * lax.rsqrt(jnp.mean(o_n * o_n, axis=-1, keepdims=True) + RMS_EPS) * ng_ref[...]
    mix = _dot(o_n.astype(BF16), w_ref[:half]) + _dot(hg_ref[...].astype(BF16), w_ref[half:])
    o_ref[...] = _layer_norm(DEEPNORM_ALPHA * x_ref[...] + mix, g_ref[...], b_ref[...])


def _mix(o_nsa, o_hg, x, w_out, ng, g, b, tm):
    t, d = x.shape
    half = o_nsa.shape[1]
    rows = lambda w: pl.BlockSpec((tm, w), lambda i: (i, 0))
    full = lambda r, c: pl.BlockSpec((r, c), lambda i: (0, 0))
    return pl.pallas_call(
        _mix_kernel,
        grid=(t // tm,),
        in_specs=[rows(half), rows(half), rows(d), full(d, d), full(1, half), full(1, d), full(1, d)],
        out_specs=rows(d),
        out_shape=jax.ShapeDtypeStruct((t, d), F32),
        compiler_params=_params(("parallel",)),
        name="mix",
    )(o_nsa, o_hg, x, w_out, ng, g, b)


def _xattn_kernel(x_ref, wq_ref, kT_ref, v_ref, wo_ref, g_ref, b_ref, o_ref):
    x = x_ref[0]
    d = x.shape[1]
    dh = d // XA_HEADS
    q = _dot(x.astype(BF16), wq_ref[...])
    heads = []
    for h in range(XA_HEADS):
        cols = slice(h * dh, (h + 1) * dh)
        s = _dot(q[:, cols].astype(BF16), kT_ref[0, cols, :]) * (dh ** -0.5)
        e = jnp.exp(s - jnp.max(s, axis=-1, keepdims=True))
        p = e * (1.0 / jnp.sum(e, axis=-1, keepdims=True))
        heads.append(_dot(p.astype(BF16), v_ref[0, :, cols]))
    o = jnp.concatenate(heads, axis=1)
    xa = _dot(o.astype(BF16), wo_ref[...])
    o_ref[0] = _layer_norm(DEEPNORM_ALPHA * x + xa, g_ref[...], b_ref[...])


def _xattn(x, wq, kT, v, wo, g, b, tm):
    bsz, seq, d = x.shape
    m = v.shape[1]
    full = lambda r, c: pl.BlockSpec((r, c), lambda i, j: (0, 0))
    return pl.pallas_call(
        _xattn_kernel,
        grid=(bsz, seq // tm),
        in_specs=[pl.BlockSpec((1, tm, d), lambda i, j: (i, j, 0)),
                  full(d, d),
                  pl.BlockSpec((1, d, m), lambda i, j: (i, 0, 0)),
                  pl.BlockSpec((1, m, d), lambda i, j: (i, 0, 0)),
                  full(d, d), full(1, d), full(1, d)],
        out_specs=pl.BlockSpec((1, tm, d), lambda i, j: (i, j, 0)),
        out_shape=jax.ShapeDtypeStruct((bsz, seq, d), F32),
        compiler_params=_params(("parallel", "parallel")),
        name="xattn",
    )(x, wq, kT, v, wo, g, b)


def _route_kernel(x_ref, wh_ref, wl_ref, bias_ref, ids_ref, wts_ref, rnk_ref, counts_ref, cnt_ref):
    xh, xl = _split2(x_ref[...])
    logits = _dot(xh, wh_ref[...]) + _dot(xh, wl_ref[...]) + _dot(xl, wh_ref[...]) + bias_ref[...]
    lane = lax.broadcasted_iota(jnp.int32, logits.shape, 1)
    lane_f = lane.astype(F32)
    big = float(LANES)

    is_g = lane < MOE_GROUPS
    g_max = jnp.max(jnp.where(is_g, logits, NEG_INF), axis=-1, keepdims=True)
    g_sum = jnp.sum(jnp.where(is_g, jnp.exp(logits - g_max), 0.0), axis=-1, keepdims=True)
    g_w = 1.0 / g_sum
    g_idx = jnp.min(jnp.where(is_g & (logits == g_max), lane_f, big), axis=-1, keepdims=True)

    e_lo = MOE_GROUPS + MOE_EXPERTS_PER_GROUP * g_idx
    is_e = (lane_f >= e_lo) & (lane_f < e_lo + MOE_EXPERTS_PER_GROUP)
    e_log = jnp.where(is_e, logits, NEG_INF)
    e_max = jnp.max(e_log, axis=-1, keepdims=True)
    e_exp = jnp.where(is_e, jnp.exp(logits - e_max), 0.0)
    e_sum = jnp.sum(e_exp, axis=-1, keepdims=True)
    i1 = jnp.min(jnp.where(is_e & (e_log == e_max), lane_f, big), axis=-1, keepdims=True)
    rest = jnp.where(lane_f == i1, NEG_INF, e_log)
    r_max = jnp.max(rest, axis=-1, keepdims=True)
    i2 = jnp.min(jnp.where(is_e & (lane_f != i1) & (rest == r_max), lane_f, big), axis=-1, keepdims=True)
    p1 = 1.0 / e_sum
    p2 = jnp.exp(r_max - e_max) / e_sum
    tot = p1 + p2
    w1 = g_w * (p1 / tot)
    w2 = g_w * (p2 / tot)
    ids = jnp.where(lane == 0, i1, i2) - float(MOE_GROUPS)
    ids_ref[...] = ids.astype(jnp.int32)
    wts_ref[...] = jnp.where(lane == 0, w1, w2)

    @pl.when(pl.program_id(0) == 0)
    def _():
        cnt_ref[...] = jnp.zeros_like(cnt_ref)

    tm = logits.shape[0]
    hit1 = lane_f == i1
    hit2 = lane_f == i2
    hits = jnp.where(hit1 | hit2, 1.0, 0.0)
    earlier = lax.broadcasted_iota(jnp.int32, (tm, tm), 1) < lax.broadcasted_iota(jnp.int32, (tm, tm), 0)
    before = _dot(jnp.where(earlier, 1.0, 0.0).astype(BF16), hits.astype(BF16)) + cnt_ref[...]
    rank1 = jnp.sum(jnp.where(hit1, before, 0.0), axis=-1, keepdims=True)
    rank2 = jnp.sum(jnp.where(hit2, before, 0.0), axis=-1, keepdims=True)
    rnk_ref[...] = jnp.where(lane == 0, rank1, rank2).astype(jnp.int32)
    cnt_ref[...] = cnt_ref[...] + jnp.sum(hits, axis=0, keepdims=True)
    counts_ref[...] = cnt_ref[...].astype(jnp.int32)


def _route(x, wh, wl, bias, tm):
    t, d = x.shape
    rows = lambda w: pl.BlockSpec((tm, w), lambda i: (i, 0))
    full = lambda r, c: pl.BlockSpec((r, c), lambda i: (0, 0))
    i32 = jnp.int32
    return pl.pallas_call(
        _route_kernel,
        grid=(t // tm,),
        in_specs=[rows(d), full(d, LANES), full(d, LANES), full(1, LANES)],
        out_specs=[rows(LANES), rows(LANES), rows(LANES), full(1, LANES)],
        out_shape=[jax.ShapeDtypeStruct((t, LANES), i32), jax.ShapeDtypeStruct((t, LANES), F32),
                   jax.ShapeDtypeStruct((t, LANES), i32), jax.ShapeDtypeStruct((1, LANES), i32)],
        scratch_shapes=[pltpu.VMEM((1, LANES), F32)],
        compiler_params=_params(("arbitrary",)),
        name="route",
    )(x, wh, wl, bias)


def _row_copy_wait(src, dst, sem, rows):
    pltpu.make_async_copy(src.at[pl.ds(0, rows)], dst.at[pl.ds(0, rows)], sem).wait()


def _dispatch_kernel(dest_ref, zfill_ref, x_ref, xs_hbm, zbuf, zsem, sem):
    i = pl.program_id(0)
    tm = x_ref.shape[0]

    @pl.when(i == 0)
    def _():
        zbuf[...] = jnp.zeros_like(zbuf)
        tail = lambda e: xs_hbm.at[pl.ds(pl.multiple_of(zfill_ref[e], MOE_BLOCK), MOE_BLOCK)]
        for e in range(zfill_ref.shape[0]):
            @pl.when(zfill_ref[e] >= 0)
            def _():
                pltpu.make_async_copy(zbuf, tail(e), zsem).start()
        for e in range(zfill_ref.shape[0]):
            @pl.when(zfill_ref[e] >= 0)
            def _():
                pltpu.make_async_copy(zbuf, tail(e), zsem).wait()

    def send(j, c):
        for k in range(MOE_TOP_K):
            slot = dest_ref[(i * tm + j) * MOE_TOP_K + k]
            pltpu.make_async_copy(x_ref.at[pl.ds(j, 1)], xs_hbm.at[pl.ds(slot, 1)], sem).start()
        return c

    lax.fori_loop(0, tm, send, 0, unroll=8)
    for _ in range(MOE_TOP_K):
        _row_copy_wait(x_ref, xs_hbm, sem, tm)


def _dispatch(dest, zfill, x, cap, tm):
    t, d = x.shape
    grid_spec = pltpu.PrefetchScalarGridSpec(
        num_scalar_prefetch=2,
        grid=(t // tm,),
        in_specs=[pl.BlockSpec((tm, d), lambda i, dest, zfill: (i, 0))],
        out_specs=pl.BlockSpec(memory_space=pl.ANY),
        scratch_shapes=[pltpu.VMEM((MOE_BLOCK, d), F32), pltpu.SemaphoreType.DMA, pltpu.SemaphoreType.DMA])
    return pl.pallas_call(
        _dispatch_kernel,
        grid_spec=grid_spec,
        out_shape=jax.ShapeDtypeStruct((cap, d), F32),
        compiler_params=_params(("arbitrary",)),
        name="dispatch",
    )(dest, zfill, x)


def _experts_kernel(bexp_ref, nact_ref, xs_ref, wg_ref, wu_ref, wd_ref, y_ref):
    @pl.when(pl.program_id(0) < nact_ref[0])
    def _():
        xb = xs_ref[...].astype(BF16)
        gate = _dot(xb, wg_ref[0])
        hid = (gate * _sigmoid(gate)) * _dot(xb, wu_ref[0])
        y_ref[...] = _dot(hid.astype(BF16), wd_ref[0])

    @pl.when(pl.program_id(0) >= nact_ref[0])
    def _():
        y_ref[...] = jnp.zeros_like(y_ref)


def _experts(bexp, nact, xs, wg, wu, wd):
    cap, d = xs.shape
    dff = wg.shape[-1]
    by_expert = lambda i, bexp, nact: (bexp[i], 0, 0)
    active = lambda i, bexp, nact: (jnp.minimum(i, nact[0] - 1), 0)
    grid_spec = pltpu.PrefetchScalarGridSpec(
        num_scalar_prefetch=2,
        grid=(cap // MOE_BLOCK,),
        in_specs=[pl.BlockSpec((MOE_BLOCK, d), active),
                  pl.BlockSpec((1, d, dff), by_expert),
                  pl.BlockSpec((1, d, dff), by_expert),
                  pl.BlockSpec((1, dff, d), by_expert)],
        out_specs=pl.BlockSpec((MOE_BLOCK, d), lambda i, bexp, nact: (i, 0)))
    return pl.pallas_call(
        _experts_kernel,
        grid_spec=grid_spec,
        out_shape=jax.ShapeDtypeStruct((cap, d), F32),
        compiler_params=_params(("arbitrary",)),
        name="experts",
    )(bexp, nact, xs, wg, wu, wd)


def _combine_kernel(dest_ref, y_hbm, x_ref, wts_ref, g_ref, b_ref, o_ref, ybuf, sem):
    i = pl.program_id(0)
    n = pl.num_programs(0)
    tm = x_ref.shape[0]
    cur = lax.rem(i, 2)

    def fetch(tile, buf):
        def one(j, c):
            for k in range(MOE_TOP_K):
                slot = dest_ref[(tile * tm + j) * MOE_TOP_K + k]
                pltpu.make_async_copy(y_hbm.at[pl.ds(slot, 1)], ybuf.at[buf, k, pl.ds(j, 1)], sem.at[buf]).start()
            return c
        lax.fori_loop(0, tm, one, 0, unroll=8)

    @pl.when(i == 0)
    def _():
        fetch(0, 0)

    @pl.when(i + 1 < n)
    def _():
        fetch(i + 1, 1 - cur)

    for k in range(MOE_TOP_K):
        _row_copy_wait(y_hbm, ybuf.at[cur, k], sem.at[cur], tm)
    w = wts_ref[...]
    ff = w[:, 0:1] * ybuf[cur, 0]
    for k in range(1, MOE_TOP_K):
        ff = ff + w[:, k:k + 1] * ybuf[cur, k]
    o_ref[...] = _layer_norm(DEEPNORM_ALPHA * x_ref[...] + ff, g_ref[...], b_ref[...])


def _combine(dest, y, x, wts, g, b, tm):
    t, d = x.shape
    rows = lambda w: pl.BlockSpec((tm, w), lambda i, dest: (i, 0))
    full = lambda r, c: pl.BlockSpec((r, c), lambda i, dest: (0, 0))
    grid_spec = pltpu.PrefetchScalarGridSpec(
        num_scalar_prefetch=1,
        grid=(t // tm,),
        in_specs=[pl.BlockSpec(memory_space=pl.ANY), rows(d), rows(LANES), full(1, d), full(1, d)],
        out_specs=rows(d),
        scratch_shapes=[pltpu.VMEM((2, MOE_TOP_K, tm, d), F32), pltpu.SemaphoreType.DMA((2,))])
    return pl.pallas_call(
        _combine_kernel,
        grid_spec=grid_spec,
        out_shape=jax.ShapeDtypeStruct((t, d), F32),
        compiler_params=_params(("arbitrary",)),
        name="combine",
    )(dest, y, x, wts, g, b)


def _dispatch_plan(ids, rnk, counts, n_tok):
    padded = ((counts + MOE_BLOCK - 1) // MOE_BLOCK) * MOE_BLOCK
    pend = jnp.cumsum(padded)
    pstart = pend - padded
    dest = (pstart[ids] + rnk).reshape(-1).astype(jnp.int32)
    cap = n_tok * MOE_TOP_K + MOE_N_EXPERTS * MOE_BLOCK
    n_blocks = cap // MOE_BLOCK
    block_start = jnp.arange(n_blocks, dtype=jnp.int32) * MOE_BLOCK
    bexp = jnp.minimum(jnp.sum(block_start[:, None] >= pend[None, :], axis=1), MOE_N_EXPERTS - 1).astype(jnp.int32)
    nact = (pend[-1] // MOE_BLOCK).astype(jnp.int32).reshape(1)
    tails = jnp.where(padded > 0, pend - MOE_BLOCK, -1)
    idle = pend[-1] + block_start[:MOE_N_EXPERTS]
    zfill = jnp.concatenate([tails, jnp.where(idle < cap, idle, -1)]).astype(jnp.int32)
    return dest, bexp, nact, zfill, cap


def kernel(x, mem, w_in, cmp_pe_k, cmp_pe_v, cmp_w1_k, cmp_w2_k, cmp_w1_v, cmp_w2_v, nsa_norm_g,
           hg_lb_logits, hg_norm_g, w_out, ln1_g, ln1_b, xa_wq, xa_wk, xa_wv, xa_wo, ln2_g, ln2_b,
           moe_w_group, moe_b_group, moe_w_expert, moe_b_expert, moe_w_gate, moe_w_up, moe_w_down,
           ln3_g, ln3_b):
    b, s, d = x.shape
    t = b * s
    g, dh = NSA_KV_GROUPS, NSA_HEAD_DIM
    nch = s // CMP_STRIDE
    row = lambda a: a.reshape(1, -1)
    lb_all = jnp.cumsum(jax.nn.softmax(hg_lb_logits.astype(F32), axis=0), axis=0)
    xt = x.reshape(t, d)
    for l in range(DEPTH):
        nsa_cols = NSA_Q_COLS + 6 * NSA_KV_COLS
        w_l = w_in[l]
        w_perm = jnp.concatenate(
            [w_l[:, :nsa_cols], w_l[:, nsa_cols + NSA_GATE_COLS:], w_l[:, nsa_cols:nsa_cols + NSA_GATE_COLS],
             jnp.zeros((d, LANES - NSA_GATE_COLS), w_l.dtype)], axis=1)
        h, ks, kw, vsT, vwT = _inproj(xt, w_perm.astype(BF16), b, s, 256)
        nkc = h[:, NSA_Q_COLS:NSA_Q_COLS + NSA_KV_COLS]
        nvc = h[:, NSA_Q_COLS + NSA_KV_COLS:NSA_Q_COLS + 2 * NSA_KV_COLS]
        hg_col0 = nsa_cols // LANES
        gate_tile = (nsa_cols + 4 * HG_COLS) // LANES

        chunks = lambda a: jnp.transpose(a.reshape(b, nch, CMP_STRIDE, g, dh), (0, 3, 1, 2, 4)).reshape(
            b, g, nch, CMP_STRIDE * dh)
        comp = _compress(jnp.stack([chunks(nkc), chunks(nvc)]),
                         jnp.stack([cmp_pe_k[l].reshape(1, -1), cmp_pe_v[l].reshape(1, -1)]),
                         jnp.stack([cmp_w1_k[l], cmp_w1_v[l]]),
                         jnp.stack([cmp_w2_k[l], cmp_w2_v[l]]))
        kch, kcl = _split2(comp[0])
        vcT = jnp.swapaxes(comp[1], -1, -2).astype(BF16)

        o_nsa = _nsa(h, kch, kcl, vcT, ks, vsT, kw, vwT, s, gate_tile)

        o_hg = _hgrn(h, lb_all[l].reshape(HG_HEADS, 1, HG_KEY_DIM), row(hg_norm_g[l]), b, s, hg_col0)

        x1 = _mix(o_nsa, o_hg, xt, w_out[l].astype(BF16), row(nsa_norm_g[l]), row(ln1_g[l]), row(ln1_b[l]), 256)

        n_mem = mem.shape[1]
        kv = _matmul(mem.reshape(b * n_mem, d),
                     jnp.concatenate([xa_wk[l], xa_wv[l]], axis=1).astype(BF16), n_mem)
        kT = jnp.swapaxes(kv[:, :d].reshape(b, n_mem, d), 1, 2).astype(BF16)
        v = kv[:, d:].reshape(b, n_mem, d).astype(BF16)
        x2 = _xattn(x1.reshape(b, s, d), xa_wq[l].astype(BF16), kT, v, xa_wo[l].astype(BF16),
                    row(ln2_g[l]), row(ln2_b[l]), 256).reshape(t, d)

        w_r = jnp.concatenate([moe_w_group[l], moe_w_expert[l]], axis=1)
        w_r = jnp.pad(w_r, ((0, 0), (0, LANES - w_r.shape[1])))
        b_r = jnp.pad(jnp.concatenate([moe_b_group[l], moe_b_expert[l]]), (0, LANES - MOE_GROUPS - MOE_N_EXPERTS))
        w_rh, w_rl = _split2(w_r)
        ids, wts, rnk, counts = _route(x2, w_rh, w_rl, row(b_r), 512)
        dest, bexp, nact, zfill, cap = _dispatch_plan(
            ids[:, :MOE_TOP_K], rnk[:, :MOE_TOP_K], counts[0, MOE_GROUPS:MOE_GROUPS + MOE_N_EXPERTS], t)
        xs = _dispatch(dest, zfill, x2, cap, 256)
        ys = _experts(bexp, nact, xs, moe_w_gate[l].astype(BF16), moe_w_up[l].astype(BF16),
                      moe_w_down[l].astype(BF16))
        xt = _combine(dest, ys, x2, wts, row(ln3_g[l]), row(ln3_b[l]), 256)
    return xt.reshape(b, s, d)
```

```python
import functools

import jax
import jax.numpy as jnp
from jax import lax
from jax.experimental import pallas as pl
from jax.experimental.pallas import tpu as pltpu

F32 = jnp.float32
BF16 = jnp.bfloat16

NSA_HEAD_DIM = 64
NSA_HEADS = 8
NSA_KV_GROUPS = 2
NSA_HPG = NSA_HEADS // NSA_KV_GROUPS
CMP_BLOCK = 32
CMP_STRIDE = 16
CMP_HIDDEN = 256
SEL_BLOCK = 64
SEL_TOP_N = 16
WINDOW = 512
FORCE_BONUS = 1.0e4
HG_KEY_DIM = 128
HG_VAL_DIM = 128
HG_HEADS = 4
XA_HEADS = 4
MOE_GROUPS = 4
MOE_EXPERTS_PER_GROUP = 8
MOE_N_EXPERTS = MOE_GROUPS * MOE_EXPERTS_PER_GROUP
MOE_TOP_K = 2
MOE_BLOCK = 128
DEPTH = 1
DEEPNORM_ALPHA = (2.0 * DEPTH) ** 0.25
LN_EPS = 1e-5
RMS_EPS = 1e-6
NEG_INF = -1e30

NSA_Q_COLS = NSA_HEADS * NSA_HEAD_DIM
NSA_KV_COLS = NSA_KV_GROUPS * NSA_HEAD_DIM
NSA_GATE_COLS = NSA_HEADS * 3
HG_COLS = HG_HEADS * HG_KEY_DIM

LANES = 128
Q_TILE = 128
NSA_ROWS = NSA_HPG * Q_TILE
SEL_KV_TILE = 512
SEL_GROUP = 4
NSA_PARTS = 4
NSA_V_ROWS = NSA_HEAD_DIM + 16
LOG2_E = 1.4426950408889634
HG_CHUNK = 128
HG_SUB = 16
HG_ROWS = 512
HG_HEADS_PER_STEP = 2
VMEM_LIMIT = 48 * 1024 * 1024


def _dot(a, b):
    return jnp.dot(a, b, preferred_element_type=F32)


def _dot_nt(a, b):
    return lax.dot_general(a, b, (((1,), (1,)), ((), ())), preferred_element_type=F32)


def _split2(a):
    hi = a.astype(BF16)
    lo = (a - hi.astype(F32)).astype(BF16)
    return hi, lo


def _split3(a):
    p1 = a.astype(BF16)
    r1 = a - p1.astype(F32)
    p2 = r1.astype(BF16)
    p3 = (r1 - p2.astype(F32)).astype(BF16)
    return p1, p2, p3


def _dot3(a, b):
    ah, al = _split2(a)
    bh, bl = _split2(b)
    return _dot(ah, bh) + _dot(ah, bl) + _dot(al, bh)


def _sigmoid(x):
    return 1.0 / (1.0 + jnp.exp(-x))


def _layer_norm(y, g, b):
    mu = jnp.mean(y, axis=-1, keepdims=True)
    d = y - mu
    var = jnp.mean(d * d, axis=-1, keepdims=True)
    return d * lax.rsqrt(var + LN_EPS) * g + b


def _params(sem):
    return pltpu.CompilerParams(dimension_semantics=sem, vmem_limit_bytes=VMEM_LIMIT)


def _matmul_kernel(x_ref, w_ref, o_ref):
    o_ref[...] = _dot(x_ref[...].astype(BF16), w_ref[...]).astype(o_ref.dtype)


def _matmul(x, w, tm):
    m, k = x.shape
    n = w.shape[1]
    return pl.pallas_call(
        _matmul_kernel,
        grid=(m // tm,),
        in_specs=[pl.BlockSpec((tm, k), lambda i: (i, 0)),
                  pl.BlockSpec((k, n), lambda i: (0, 0))],
        out_specs=pl.BlockSpec((tm, n), lambda i: (i, 0)),
        out_shape=jax.ShapeDtypeStruct((m, n), F32),
        compiler_params=_params(("parallel",)),
        name="proj",
    )(x, w)


def _inproj_kernel(x_ref, w_ref, h_ref, ks_ref, kw_ref, vsT_ref, vwT_ref, *, tiles_per_seq):
    h = _dot(x_ref[...].astype(BF16), w_ref[...])
    h_ref[...] = h
    tm = h.shape[0]
    dh = NSA_HEAD_DIM
    k_sel0 = NSA_Q_COLS + 2 * NSA_KV_COLS
    v_sel0, k_win0, v_win0 = k_sel0 + NSA_KV_COLS, k_sel0 + 2 * NSA_KV_COLS, k_sel0 + 3 * NSA_KV_COLS
    pos = lax.rem(pl.program_id(0), tiles_per_seq) * tm + lax.broadcasted_iota(jnp.int32, (tm, 1), 0)
    blk = lax.shift_right_logical(pos & (SEL_KV_TILE - 1), SEL_BLOCK.bit_length() - 1)
    onehot = jnp.where(blk == lax.broadcasted_iota(jnp.int32, (1, LANES - dh), 1), 1.0, 0.0)
    ones_row = jnp.where(lax.broadcasted_iota(jnp.int32, (NSA_V_ROWS - dh, tm), 0) == 0, 1.0, 0.0)
    vsT = h[:, v_sel0:v_sel0 + NSA_KV_COLS].T
    vwT = h[:, v_win0:v_win0 + NSA_KV_COLS].T
    for g in range(NSA_KV_GROUPS):
        cols = slice(g * dh, (g + 1) * dh)
        ks_ref[0, g] = jnp.concatenate([h[:, k_sel0 + g * dh:k_sel0 + (g + 1) * dh], onehot], axis=1).astype(BF16)
        kw_ref[0, g] = h[:, k_win0 + g * dh:k_win0 + (g + 1) * dh].astype(BF16)
        vsT_ref[0, g] = jnp.concatenate([vsT[cols], ones_row], axis=0).astype(BF16)
        vwT_ref[0, g] = jnp.concatenate([vwT[cols], ones_row], axis=0).astype(BF16)


def _inproj(x, w, bsz, seq, tm):
    t, k = x.shape
    n = w.shape[1]
    g, dh = NSA_KV_GROUPS, NSA_HEAD_DIM
    tps = seq // tm
    keys = lambda width: pl.BlockSpec((1, g, tm, width), lambda i: (i // tps, 0, i % tps, 0))
    vals = pl.BlockSpec((1, g, NSA_V_ROWS, tm), lambda i: (i // tps, 0, 0, i % tps))
    return pl.pallas_call(
        functools.partial(_inproj_kernel, tiles_per_seq=tps),
        grid=(t // tm,),
        in_specs=[pl.BlockSpec((tm, k), lambda i: (i, 0)),
                  pl.BlockSpec((k, n), lambda i: (0, 0))],
        out_specs=[pl.BlockSpec((tm, n), lambda i: (i, 0)), keys(LANES), keys(dh), vals, vals],
        out_shape=[jax.ShapeDtypeStruct((t, n), F32),
                   jax.ShapeDtypeStruct((bsz, g, seq, LANES), BF16),
                   jax.ShapeDtypeStruct((bsz, g, seq, dh), BF16),
                   jax.ShapeDtypeStruct((bsz, g, NSA_V_ROWS, seq), BF16),
                   jax.ShapeDtypeStruct((bsz, g, NSA_V_ROWS, seq), BF16)],
        compiler_params=_params(("parallel",)),
        name="inproj",
    )(x, w)


def _compress_kernel(ch_ref, pe_ref, w1_ref, w2_ref, o_ref):
    ch = ch_ref[0, 0, 0]
    half = ch.shape[1]
    nch = ch.shape[0]
    pe = pe_ref[0]
    w1 = w1_ref[0]
    top = _dot3(ch + pe[:, :half], w1[:half])
    bot = _dot3(ch + pe[:, half:], w1[half:])
    hid = top + pltpu.roll(bot, nch - 1, 0)
    c = 0.7978845608028654
    act = 0.5 * hid * (1.0 + jnp.tanh(c * (hid + 0.044715 * hid * hid * hid)))
    out = _dot3(act, w2_ref[0])
    row = lax.broadcasted_iota(jnp.int32, out.shape, 0)
    o_ref[0, 0, 0] = jnp.where(row < nch - 1, out, 0.0)


def _compress(ch, pe, w1, w2):
    _, b, g, nch, width = ch.shape
    hidden = w1.shape[-1]
    d = w2.shape[-1]
    return pl.pallas_call(
        _compress_kernel,
        grid=(2, b, g),
        in_specs=[pl.BlockSpec((1, 1, 1, nch, width), lambda a, i, j: (a, i, j, 0, 0)),
                  pl.BlockSpec((1, 1, 2 * width), lambda a, i, j: (a, 0, 0)),
                  pl.BlockSpec((1, 2 * width, hidden), lambda a, i, j: (a, 0, 0)),
                  pl.BlockSpec((1, hidden, d), lambda a, i, j: (a, 0, 0))],
        out_specs=pl.BlockSpec((1, 1, 1, nch, d), lambda a, i, j: (a, i, j, 0, 0)),
        out_shape=jax.ShapeDtypeStruct((2, b, g, nch, d), F32),
        compiler_params=_params(("parallel", "parallel", "parallel")),
        name="compress",
    )(ch, pe, w1, w2)


def _tile_heads(a):
    return jnp.concatenate([a] * NSA_HPG, axis=1)


def _nsa_kernel(q_ref, kc_ref, vcT_ref, ks_ref, vsT_ref, kw_ref, vwT_ref, gate_ref,
                o_ref, sel_ref, qx_ref, psum_ref, gt_ref, *, seq):
    s0 = pl.program_id(2) * Q_TILE
    nsel = seq // SEL_BLOCK
    ncmp_pad = seq // CMP_STRIDE
    dh = NSA_HEAD_DIM

    q_t = q_ref[...].T
    q = jnp.concatenate([q_t[r * dh:(r + 1) * dh] for r in range(NSA_HPG)], axis=1)
    q_hi = (q * (dh ** -0.5 * LOG2_E)).astype(BF16)
    t_q = s0 + lax.broadcasted_iota(jnp.int32, (1, Q_TILE), 1)
    t_all = _tile_heads(t_q)

    span = WINDOW + Q_TILE
    lo = pl.multiple_of(jnp.maximum(s0 - WINDOW, 0), Q_TILE)
    kpos = lo + lax.broadcasted_iota(jnp.int32, (span, 1), 0)
    sw = _dot(kw_ref[0, 0, pl.ds(lo, span), :], q_hi)
    sw = sw + _tile_heads(jnp.where((kpos <= t_q) & (kpos > t_q - WINDOW), 0.0, NEG_INF))
    e_w = jnp.exp2(sw - jnp.max(sw, axis=0, keepdims=True)).astype(BF16)
    acc_w = _dot(vwT_ref[0, 0, :, pl.ds(lo, span)], e_w)
    o_w = acc_w[0:dh] * (1.0 / acc_w[dh:dh + 1])

    def compressed_and_select(parts):
        ncmp = parts * (ncmp_pad // NSA_PARTS)
        rows = parts * (nsel // NSA_PARTS)
        sc = _dot(kc_ref[0, 0, 0:ncmp, :], q_hi)
        n_end = lax.broadcasted_iota(jnp.int32, (ncmp, 1), 0) * CMP_STRIDE + (CMP_BLOCK - 1)
        sc = sc + _tile_heads(jnp.where(n_end <= t_q, 0.0, NEG_INF))
        e_c = jnp.exp2(sc - jnp.max(sc, axis=0, keepdims=True))
        l_c = jnp.sum(e_c, axis=0, keepdims=True)
        p_c = e_c * jnp.where(t_all >= CMP_BLOCK - 1, 1.0 / l_c, 0.0)
        o_c = _dot(vcT_ref[0, 0, :, 0:ncmp], p_c.astype(BF16))

        p_sum = p_c[:, 0:Q_TILE]
        for r in range(1, NSA_HPG):
            p_sum = p_sum + p_c[:, r * Q_TILE:(r + 1) * Q_TILE]
        ratio = SEL_BLOCK // CMP_STRIDE
        first_row = 8 - (CMP_BLOCK // CMP_STRIDE - 1)
        psum_ref[0:8, :] = jnp.zeros((8, Q_TILE), F32)
        psum_ref[8:8 + ncmp, :] = p_sum
        imp = psum_ref[pl.ds(first_row, rows, stride=ratio), :]
        for m in range(1, ratio + CMP_BLOCK // CMP_STRIDE - 1):
            imp = imp + psum_ref[pl.ds(first_row + m, rows, stride=ratio), :]
        j_idx = lax.broadcasted_iota(jnp.int32, (rows, Q_TILE), 0)
        cur = lax.shift_right_logical(t_q, SEL_BLOCK.bit_length() - 1)
        forced = (j_idx == 0) | (j_idx == cur) | (j_idx == cur - 1)
        score = jnp.where(j_idx <= cur, imp + jnp.where(forced, FORCE_BONUS, 0.0), -1.0)
        j_f = j_idx.astype(F32)
        sel = jnp.zeros((rows, Q_TILE), F32)
        for _ in range(min(SEL_TOP_N, rows)):
            best = jnp.max(score, axis=0, keepdims=True)
            first = jnp.min(jnp.where(score == best, j_f, float(rows)), axis=0, keepdims=True)
            pick = j_f == first
            sel = jnp.where(pick, 1.0, sel)
            score = jnp.where(pick, -3.0e38, score)
        sel_ref[0:rows, :] = sel
        if parts < NSA_PARTS:
            sel_ref[rows:, :] = jnp.zeros((nsel - rows, Q_TILE), F32)
        return o_c

    part_len = seq // NSA_PARTS
    parts_needed = lax.div(s0 + (Q_TILE + part_len - 1), part_len)
    o_c = lax.switch(parts_needed - 1,
                     [functools.partial(compressed_and_select, n) for n in range(1, NSA_PARTS + 1)])

    blocks_per_tile = SEL_KV_TILE // SEL_BLOCK
    for slot in range(SEL_GROUP):
        qx_ref[slot, 0:dh, :] = q_hi
        qx_ref[slot, dh:, :] = jnp.zeros((qx_ref.shape[1] - dh, NSA_ROWS), BF16)
    k_off = lax.broadcasted_iota(jnp.int32, (SEL_KV_TILE, 1), 0)

    def sel_scores(j, slot, causal):
        base = pl.multiple_of(j * SEL_KV_TILE, SEL_KV_TILE)
        member = sel_ref[pl.ds(pl.multiple_of(j * blocks_per_tile, blocks_per_tile), blocks_per_tile), :]
        bias = _tile_heads(jnp.where(member > 0.5, 0.0, NEG_INF))
        qx_ref[slot, dh:dh + 2 * blocks_per_tile, :] = jnp.concatenate(
            [bias, jnp.zeros_like(bias)], axis=0).astype(BF16)
        s = _dot(ks_ref[0, 0, pl.ds(base, SEL_KV_TILE), :], qx_ref[slot])
        if causal:
            s = s + _tile_heads(jnp.where((base + k_off) <= t_q, 0.0, NEG_INF))
        return s

    def sel_update(j, s, m_i, acc):
        base = pl.multiple_of(j * SEL_KV_TILE, SEL_KV_TILE)
        m_new = jnp.maximum(m_i, jnp.max(s, axis=0, keepdims=True))
        p = jnp.exp2(s - m_new).astype(BF16)
        pv = _dot(vsT_ref[0, 0, :, pl.ds(base, SEL_KV_TILE)], p)
        return m_new, jnp.exp2(m_i - m_new) * acc + pv

    def sel_group(i, carry, n_tiles, causal):
        tiles = [SEL_GROUP * i + u for u in range(n_tiles)]
        scores = [sel_scores(j, u, causal) for u, j in enumerate(tiles)]
        for j, s in zip(tiles, scores):
            carry = sel_update(j, s, *carry)
        return carry

    group_keys = SEL_GROUP * SEL_KV_TILE
    n_groups = lax.div(s0 + (Q_TILE + group_keys - 1), group_keys)
    last_tiles = lax.div(s0 + (Q_TILE + SEL_KV_TILE - 1), SEL_KV_TILE) - SEL_GROUP * (n_groups - 1)
    init = (jnp.full((1, NSA_ROWS), NEG_INF, F32), jnp.zeros((vsT_ref.shape[2], NSA_ROWS), F32))
    carry = lax.fori_loop(0, n_groups - 1, lambda i, c: sel_group(i, c, SEL_GROUP, False), init)
    _, acc_s = lax.switch(
        last_tiles - 1,
        [functools.partial(sel_group, n_tiles=n, causal=True) for n in range(1, SEL_GROUP + 1)],
        n_groups - 1, carry)
    o_s = acc_s[0:dh] * (1.0 / acc_s[dh:dh + 1])

    gt_ref[...] = gate_ref[...].T
    g_row0 = pl.program_id(1) * (NSA_HPG * 3)

    def gate(branch):
        return _sigmoid(jnp.concatenate(
            [gt_ref[pl.ds(g_row0 + 3 * r + branch, 1), :] for r in range(NSA_HPG)], axis=1))

    o = gate(0) * o_c + gate(1) * o_s + gate(2) * o_w
    o_rows = jnp.concatenate([o[:, r * Q_TILE:(r + 1) * Q_TILE] for r in range(NSA_HPG)], axis=0)
    o_ref[...] = o_rows.T


def _nsa(h, kc, vcT, ks, vsT, kw, vwT, seq, gate_tile):
    b, g = ks.shape[:2]
    nqb = seq // Q_TILE
    d = NSA_HEAD_DIM
    ncp = seq // CMP_STRIDE
    nsel = seq // SEL_BLOCK
    dk = ks.shape[-1]
    dv = vsT.shape[2]
    per_bg = lambda i, j, k: (i, j, 0, 0)
    q_rows = lambda i, j, k: (i * nqb + k, j)
    return pl.pallas_call(
        functools.partial(_nsa_kernel, seq=seq),
        grid=(b, g, nqb),
        in_specs=[pl.BlockSpec((Q_TILE, NSA_HPG * d), q_rows),
                  pl.BlockSpec((1, 1, ncp, d), per_bg),
                  pl.BlockSpec((1, 1, d, ncp), per_bg),
                  pl.BlockSpec((1, 1, seq, dk), per_bg),
                  pl.BlockSpec((1, 1, dv, seq), per_bg),
                  pl.BlockSpec((1, 1, seq, d), per_bg),
                  pl.BlockSpec((1, 1, dv, seq), per_bg),
                  pl.BlockSpec((Q_TILE, LANES), lambda i, j, k: (i * nqb + k, gate_tile))],
        out_specs=pl.BlockSpec((Q_TILE, NSA_HPG * d), q_rows),
        out_shape=jax.ShapeDtypeStruct((b * seq, NSA_Q_COLS), F32),
        scratch_shapes=[pltpu.VMEM((nsel, Q_TILE), F32), pltpu.VMEM((SEL_GROUP, dk, NSA_ROWS), BF16),
                        pltpu.VMEM((ncp + 8, Q_TILE), F32), pltpu.VMEM((LANES, Q_TILE), F32)],
        compiler_params=_params(("parallel", "parallel", "arbitrary")),
        name="nsa",
    )(h, kc, vcT, ks, vsT, kw, vwT, h)


def _hgrn_chunk(q, f, v, lb, state_t, shift_ref, lower, gstart, dmat, off_mask):
    c = HG_CHUNK
    log_f = jnp.log(lb + (1.0 - lb) * _sigmoid(f))
    kk = (1.0 - lb) * _sigmoid(-f)
    l1, l2, l3 = _split3(log_f)
    bt = _dot(lower, l1) + _dot(lower, l2) + _dot(lower, l3)
    bs = _dot(gstart, l1) + _dot(gstart, l2) + _dot(gstart, l3)

    shift_ref[0, HG_SUB:, :] = kk
    shift_ref[1, HG_SUB:, :] = bt
    ones = jnp.ones((kk.shape[1], c), BF16)
    att = jnp.zeros((c, c), F32)
    for d in range(HG_SUB):
        if d == 0:
            prod = q * kk
        else:
            rows = pl.ds(HG_SUB - d, c)
            prod = (q * shift_ref[0, rows, :]) * jnp.exp(jnp.minimum(bt - shift_ref[1, rows, :], 0.0))
        band = _dot(prod.astype(BF16), ones)
        att = jnp.where(dmat == d, band, att)

    qh = (q * jnp.exp(bt - bs)).astype(BF16)
    blocks = [jnp.zeros((HG_SUB, c), F32)]
    for i in range(1, c // HG_SUB):
        b_i = bs[i * HG_SUB:i * HG_SUB + 1, :]
        kh = kk * jnp.exp(jnp.minimum(b_i - bt, 0.0))
        blocks.append(_dot_nt(qh[i * HG_SUB:(i + 1) * HG_SUB], kh.astype(BF16)))
    att = jnp.where(off_mask, jnp.concatenate(blocks, axis=0), att)

    vb = v.astype(BF16)
    o = _dot_nt((q * jnp.exp(bt)).astype(BF16), state_t.astype(BF16)) + _dot(att.astype(BF16), vb)
    b_last = bt[c - 1:c, :]
    k_dec = kk * jnp.exp(b_last - bt)
    return o, state_t * jnp.exp(b_last) + _dot(v.T.astype(BF16), k_dec.astype(BF16))


def _hgrn_kernel(q_ref, f_ref, v_ref, gate_ref, lb_ref, ng_ref, o_ref, state_ref, shift_ref):
    c = HG_CHUNK

    @pl.when(pl.program_id(2) == 0)
    def _():
        state_ref[...] = jnp.zeros_like(state_ref)
        shift_ref[...] = jnp.zeros_like(shift_ref)

    row = lax.broadcasted_iota(jnp.int32, (c, c), 0)
    col = lax.broadcasted_iota(jnp.int32, (c, c), 1)
    sub = HG_SUB.bit_length() - 1
    row_blk = lax.shift_right_logical(row, sub)
    col_blk = lax.shift_right_logical(col, sub)
    lower = jnp.where(col <= row, 1.0, 0.0).astype(BF16)
    gstart = jnp.where(col_blk < row_blk, 1.0, 0.0).astype(BF16)
    dmat = jnp.where(row_blk == col_blk, row - col, -1)
    off_mask = col_blk < row_blk
    ng = ng_ref[...]

    def step(i, states):
        rows = pl.ds(pl.multiple_of(i * c, c), c)
        new_states = []
        for hd in range(HG_HEADS_PER_STEP):
            lanes = slice(hd * HG_KEY_DIM, (hd + 1) * HG_KEY_DIM)
            o, state = _hgrn_chunk(q_ref[rows, lanes], f_ref[rows, lanes], v_ref[rows, lanes], lb_ref[hd],
                                   states[hd], shift_ref.at[hd], lower, gstart, dmat, off_mask)
            gate = gate_ref[rows, lanes]
            ms = jnp.mean(o * o, axis=-1, keepdims=True)
            o_ref[rows, lanes] = o * lax.rsqrt(ms + RMS_EPS) * ng * (gate * _sigmoid(gate))
            new_states.append(state)
        return tuple(new_states)

    init = tuple(state_ref[hd] for hd in range(HG_HEADS_PER_STEP))
    for hd, state in enumerate(lax.fori_loop(0, HG_ROWS // c, step, init)):
        state_ref[hd] = state


def _hgrn(h, lb, ng, bsz, seq, col0):
    nblk = seq // HG_ROWS
    hps = HG_HEADS_PER_STEP
    width = hps * HG_KEY_DIM
    tiles = HG_COLS // width
    blk = lambda k: pl.BlockSpec((HG_ROWS, width),
                                 lambda i, j, c: (i * nblk + c, (col0 * LANES) // width + k * tiles + j))
    return pl.pallas_call(
        _hgrn_kernel,
        grid=(bsz, HG_HEADS // hps, nblk),
        in_specs=[blk(0), blk(1), blk(2), blk(3),
                  pl.BlockSpec((hps, 1, HG_KEY_DIM), lambda i, j, c: (j, 0, 0)),
                  pl.BlockSpec((1, HG_VAL_DIM), lambda i, j, c: (0, 0))],
        out_specs=pl.BlockSpec((HG_ROWS, width), lambda i, j, c: (i * nblk + c, j)),
        out_shape=jax.ShapeDtypeStruct((bsz * seq, HG_COLS), F32),
        scratch_shapes=[pltpu.VMEM((hps, HG_VAL_DIM, HG_KEY_DIM), F32),
                        pltpu.VMEM((hps, 2, HG_SUB + HG_CHUNK, HG_KEY_DIM), F32)],
        compiler_params=_params(("parallel", "parallel", "arbitrary")),
        name="hgrn",
    )(h, h, h, h, lb, ng)


def _mix_kernel(nsa_ref, hg_ref, x_ref, w_ref, ng_ref, g_ref, b_ref, o_ref):
    o_n = nsa_ref[...]
    half = o_n.shape[1]
    o_n = o_n * lax.rsqrt(jnp.mean(o_n * o_n, axis=-1, keepdims=True) + RMS_EPS) * ng_ref[...]
    mix = _dot(o_n.astype(BF16), w_ref[:half]) + _dot(hg_ref[...].astype(BF16), w_ref[half:])
    o_ref[...] = _layer_norm(DEEPNORM_ALPHA * x_ref[...] + mix, g_ref[...], b_ref[...])


def _mix(o_nsa, o_hg, x, w_out, ng, g, b, tm):
    t, d = x.shape
    half = o_nsa.shape[1]
    rows = lambda w: pl.BlockSpec((tm, w), lambda i: (i, 0))
    full = lambda r, c: pl.BlockSpec((r, c), lambda i: (0, 0))
    return pl.pallas_call(
        _mix_kernel,
        grid=(t // tm,),
        in_specs=[rows(half), rows(half), rows(d), full(d, d), full(1, half), full(1, d), full(1, d)],
        out_specs=rows(d),
        out_shape=jax.ShapeDtypeStruct((t, d), F32),
        compiler_params=_params(("parallel",)),
        name="mix",
    )(o_nsa, o_hg, x, w_out, ng, g, b)


def _xattn_kernel(x_ref, wq_ref, kT_ref, v_ref, wo_ref, g_ref, b_ref, o_ref):
    x = x_ref[0]
    d = x.shape[1]
    dh = d // XA_HEADS
    q = _dot(x.astype(BF16), wq_ref[...])
    heads = []
    for h in range(XA_HEADS):
        cols = slice(h * dh, (h + 1) * dh)
        s = _dot(q[:, cols].astype(BF16), kT_ref[0, cols, :]) * (dh ** -0.5)
        e = jnp.exp(s - jnp.max(s, axis=-1, keepdims=True))
        p = e * (1.0 / jnp.sum(e, axis=-1, keepdims=True))
        heads.append(_dot(p.astype(BF16), v_ref[0, :, cols]))
    o = jnp.concatenate(heads, axis=1)
    xa = _dot(o.astype(BF16), wo_ref[...])
    o_ref[0] = _layer_norm(DEEPNORM_ALPHA * x + xa, g_ref[...], b_ref[...])


def _xattn(x, wq, kT, v, wo, g, b, tm):
    bsz, seq, d = x.shape
    m = v.shape[1]
    full = lambda r, c: pl.BlockSpec((r, c), lambda i, j: (0, 0))
    return pl.pallas_call(
        _xattn_kernel,
        grid=(bsz, seq // tm),
        in_specs=[pl.BlockSpec((1, tm, d), lambda i, j: (i, j, 0)),
                  full(d, d),
                  pl.BlockSpec((1, d, m), lambda i, j: (i, 0, 0)),
                  pl.BlockSpec((1, m, d), lambda i, j: (i, 0, 0)),
                  full(d, d), full(1, d), full(1, d)],
        out_specs=pl.BlockSpec((1, tm, d), lambda i, j: (i, j, 0)),
        out_shape=jax.ShapeDtypeStruct((bsz, seq, d), F32),
        compiler_params=_params(("parallel", "parallel")),
        name="xattn",
    )(x, wq, kT, v, wo, g, b)


def _route_kernel(x_ref, wh_ref, wl_ref, bias_ref, ids_ref, wts_ref, rnk_ref, counts_ref, cnt_ref):
    xh, xl = _split2(x_ref[...])
    logits = _dot(xh, wh_ref[...]) + _dot(xh, wl_ref[...]) + _dot(xl, wh_ref[...]) + bias_ref[...]
    lane = lax.broadcasted_iota(jnp.int32, logits.shape, 1)
    lane_f = lane.astype(F32)
    big = float(LANES)

    is_g = lane < MOE_GROUPS
    g_max = jnp.max(jnp.where(is_g, logits, NEG_INF), axis=-1, keepdims=True)
    g_sum = jnp.sum(jnp.where(is_g, jnp.exp(logits - g_max), 0.0), axis=-1, keepdims=True)
    g_w = 1.0 / g_sum
    g_idx = jnp.min(jnp.where(is_g & (logits == g_max), lane_f, big), axis=-1, keepdims=True)

    e_lo = MOE_GROUPS + MOE_EXPERTS_PER_GROUP * g_idx
    is_e = (lane_f >= e_lo) & (lane_f < e_lo + MOE_EXPERTS_PER_GROUP)
    e_log = jnp.where(is_e, logits, NEG_INF)
    e_max = jnp.max(e_log, axis=-1, keepdims=True)
    e_exp = jnp.where(is_e, jnp.exp(logits - e_max), 0.0)
    e_sum = jnp.sum(e_exp, axis=-1, keepdims=True)
    i1 = jnp.min(jnp.where(is_e & (e_log == e_max), lane_f, big), axis=-1, keepdims=True)
    rest = jnp.where(lane_f == i1, NEG_INF, e_log)
    r_max = jnp.max(rest, axis=-1, keepdims=True)
    i2 = jnp.min(jnp.where(is_e & (lane_f != i1) & (rest == r_max), lane_f, big), axis=-1, keepdims=True)
    p1 = 1.0 / e_sum
    p2 = jnp.exp(r_max - e_max) / e_sum
    tot = p1 + p2
    w1 = g_w * (p1 / tot)
    w2 = g_w * (p2 / tot)
    ids = jnp.where(lane == 0, i1, i2) - float(MOE_GROUPS)
    ids_ref[...] = ids.astype(jnp.int32)
    wts_ref[...] = jnp.where(lane == 0, w1, w2)

    @pl.when(pl.program_id(0) == 0)
    def _():
        cnt_ref[...] = jnp.zeros_like(cnt_ref)

    tm = logits.shape[0]
    hit1 = lane_f == i1
    hit2 = lane_f == i2
    hits = jnp.where(hit1 | hit2, 1.0, 0.0)
    earlier = lax.broadcasted_iota(jnp.int32, (tm, tm), 1) < lax.broadcasted_iota(jnp.int32, (tm, tm), 0)
    before = _dot(jnp.where(earlier, 1.0, 0.0).astype(BF16), hits.astype(BF16)) + cnt_ref[...]
    rank1 = jnp.sum(jnp.where(hit1, before, 0.0), axis=-1, keepdims=True)
    rank2 = jnp.sum(jnp.where(hit2, before, 0.0), axis=-1, keepdims=True)
    rnk_ref[...] = jnp.where(lane == 0, rank1, rank2).astype(jnp.int32)
    cnt_ref[...] = cnt_ref[...] + jnp.sum(hits, axis=0, keepdims=True)
    counts_ref[...] = cnt_ref[...].astype(jnp.int32)


def _route(x, wh, wl, bias, tm):
    t, d = x.shape
    rows = lambda w: pl.BlockSpec((tm, w), lambda i: (i, 0))
    full = lambda r, c: pl.BlockSpec((r, c), lambda i: (0, 0))
    i32 = jnp.int32
    return pl.pallas_call(
        _route_kernel,
        grid=(t // tm,),
        in_specs=[rows(d), full(d, LANES), full(d, LANES), full(1, LANES)],
        out_specs=[rows(LANES), rows(LANES), rows(LANES), full(1, LANES)],
        out_shape=[jax.ShapeDtypeStruct((t, LANES), i32), jax.ShapeDtypeStruct((t, LANES), F32),
                   jax.ShapeDtypeStruct((t, LANES), i32), jax.ShapeDtypeStruct((1, LANES), i32)],
        scratch_shapes=[pltpu.VMEM((1, LANES), F32)],
        compiler_params=_params(("arbitrary",)),
        name="route",
    )(x, wh, wl, bias)


def _row_copy_wait(src, dst, sem, rows):
    pltpu.make_async_copy(src.at[pl.ds(0, rows)], dst.at[pl.ds(0, rows)], sem).wait()


def _dispatch_kernel(dest_ref, zfill_ref, x_ref, xs_hbm, zbuf, zsem, sem):
    i = pl.program_id(0)
    tm = x_ref.shape[0]

    @pl.when(i == 0)
    def _():
        zbuf[...] = jnp.zeros_like(zbuf)
        tail = lambda e: xs_hbm.at[pl.ds(pl.multiple_of(zfill_ref[e], MOE_BLOCK), MOE_BLOCK)]
        for e in range(zfill_ref.shape[0]):
            @pl.when(zfill_ref[e] >= 0)
            def _():
                pltpu.make_async_copy(zbuf, tail(e), zsem).start()
        for e in range(zfill_ref.shape[0]):
            @pl.when(zfill_ref[e] >= 0)
            def _():
                pltpu.make_async_copy(zbuf, tail(e), zsem).wait()

    def send(j, c):
        for k in range(MOE_TOP_K):
            slot = dest_ref[(i * tm + j) * MOE_TOP_K + k]
            pltpu.make_async_copy(x_ref.at[pl.ds(j, 1)], xs_hbm.at[pl.ds(slot, 1)], sem).start()
        return c

    lax.fori_loop(0, tm, send, 0, unroll=8)
    for _ in range(MOE_TOP_K):
        _row_copy_wait(x_ref, xs_hbm, sem, tm)


def _dispatch(dest, zfill, x, cap, tm):
    t, d = x.shape
    grid_spec = pltpu.PrefetchScalarGridSpec(
        num_scalar_prefetch=2,
        grid=(t // tm,),
        in_specs=[pl.BlockSpec((tm, d), lambda i, dest, zfill: (i, 0))],
        out_specs=pl.BlockSpec(memory_space=pl.ANY),
        scratch_shapes=[pltpu.VMEM((MOE_BLOCK, d), F32), pltpu.SemaphoreType.DMA, pltpu.SemaphoreType.DMA])
    return pl.pallas_call(
        _dispatch_kernel,
        grid_spec=grid_spec,
        out_shape=jax.ShapeDtypeStruct((cap, d), F32),
        compiler_params=_params(("arbitrary",)),
        name="dispatch",
    )(dest, zfill, x)


def _experts_kernel(bexp_ref, nact_ref, xs_ref, wg_ref, wu_ref, wd_ref, y_ref):
    @pl.when(pl.program_id(0) < nact_ref[0])
    def _():
        xb = xs_ref[...].astype(BF16)
        gate = _dot(xb, wg_ref[0])
        hid = (gate * _sigmoid(gate)) * _dot(xb, wu_ref[0])
        y_ref[...] = _dot(hid.astype(BF16), wd_ref[0])

    @pl.when(pl.program_id(0) >= nact_ref[0])
    def _():
        y_ref[...] = jnp.zeros_like(y_ref)


def _experts(bexp, nact, xs, wg, wu, wd):
    cap, d = xs.shape
    dff = wg.shape[-1]
    by_expert = lambda i, bexp, nact: (bexp[i], 0, 0)
    active = lambda i, bexp, nact: (jnp.minimum(i, nact[0] - 1), 0)
    grid_spec = pltpu.PrefetchScalarGridSpec(
        num_scalar_prefetch=2,
        grid=(cap // MOE_BLOCK,),
        in_specs=[pl.BlockSpec((MOE_BLOCK, d), active),
                  pl.BlockSpec((1, d, dff), by_expert),
                  pl.BlockSpec((1, d, dff), by_expert),
                  pl.BlockSpec((1, dff, d), by_expert)],
        out_specs=pl.BlockSpec((MOE_BLOCK, d), lambda i, bexp, nact: (i, 0)))
    return pl.pallas_call(
        _experts_kernel,
        grid_spec=grid_spec,
        out_shape=jax.ShapeDtypeStruct((cap, d), F32),
        compiler_params=_params(("arbitrary",)),
        name="experts",
    )(bexp, nact, xs, wg, wu, wd)


def _combine_kernel(dest_ref, y_hbm, x_ref, wts_ref, g_ref, b_ref, o_ref, ybuf, sem):
    i = pl.program_id(0)
    n = pl.num_programs(0)
    tm = x_ref.shape[0]
    cur = lax.rem(i, 2)

    def fetch(tile, buf):
        def one(j, c):
            for k in range(MOE_TOP_K):
                slot = dest_ref[(tile * tm + j) * MOE_TOP_K + k]
                pltpu.make_async_copy(y_hbm.at[pl.ds(slot, 1)], ybuf.at[buf, k, pl.ds(j, 1)], sem.at[buf]).start()
            return c
        lax.fori_loop(0, tm, one, 0, unroll=8)

    @pl.when(i == 0)
    def _():
        fetch(0, 0)

    @pl.when(i + 1 < n)
    def _():
        fetch(i + 1, 1 - cur)

    for k in range(MOE_TOP_K):
        _row_copy_wait(y_hbm, ybuf.at[cur, k], sem.at[cur], tm)
    w = wts_ref[...]
    ff = w[:, 0:1] * ybuf[cur, 0]
    for k in range(1, MOE_TOP_K):
        ff = ff + w[:, k:k + 1] * ybuf[cur, k]
    o_ref[...] = _layer_norm(DEEPNORM_ALPHA * x_ref[...] + ff, g_ref[...], b_ref[...])


def _combine(dest, y, x, wts, g, b, tm):
    t, d = x.shape
    rows = lambda w: pl.BlockSpec((tm, w), lambda i, dest: (i, 0))
    full = lambda r, c: pl.BlockSpec((r, c), lambda i, dest: (0, 0))
    grid_spec = pltpu.PrefetchScalarGridSpec(
        num_scalar_prefetch=1,
        grid=(t // tm,),
        in_specs=[pl.BlockSpec(memory_space=pl.ANY), rows(d), rows(LANES), full(1, d), full(1, d)],
        out_specs=rows(d),
        scratch_shapes=[pltpu.VMEM((2, MOE_TOP_K, tm, d), F32), pltpu.SemaphoreType.DMA((2,))])
    return pl.pallas_call(
        _combine_kernel,
        grid_spec=grid_spec,
        out_shape=jax.ShapeDtypeStruct((t, d), F32),
        compiler_params=_params(("arbitrary",)),
        name="combine",
    )(dest, y, x, wts, g, b)


def _dispatch_plan(ids, rnk, counts, n_tok):
    padded = ((counts + MOE_BLOCK - 1) // MOE_BLOCK) * MOE_BLOCK
    pend = jnp.cumsum(padded)
    pstart = pend - padded
    dest = (pstart[ids] + rnk).reshape(-1).astype(jnp.int32)
    cap = n_tok * MOE_TOP_K + MOE_N_EXPERTS * MOE_BLOCK
    n_blocks = cap // MOE_BLOCK
    block_start = jnp.arange(n_blocks, dtype=jnp.int32) * MOE_BLOCK
    bexp = jnp.minimum(jnp.sum(block_start[:, None] >= pend[None, :], axis=1), MOE_N_EXPERTS - 1).astype(jnp.int32)
    nact = (pend[-1] // MOE_BLOCK).astype(jnp.int32).reshape(1)
    tails = jnp.where(padded > 0, pend - MOE_BLOCK, -1)
    idle = pend[-1] + block_start[:MOE_N_EXPERTS]
    zfill = jnp.concatenate([tails, jnp.where(idle < cap, idle, -1)]).astype(jnp.int32)
    return dest, bexp, nact, zfill, cap


def kernel(x, mem, w_in, cmp_pe_k, cmp_pe_v, cmp_w1_k, cmp_w2_k, cmp_w1_v, cmp_w2_v, nsa_norm_g,
           hg_lb_logits, hg_norm_g, w_out, ln1_g, ln1_b, xa_wq, xa_wk, xa_wv, xa_wo, ln2_g, ln2_b,
           moe_w_group, moe_b_group, moe_w_expert, moe_b_expert, moe_w_gate, moe_w_up, moe_w_down,
           ln3_g, ln3_b):
    b, s, d = x.shape
    t = b * s
    g, dh = NSA_KV_GROUPS, NSA_HEAD_DIM
    nch = s // CMP_STRIDE
    row = lambda a: a.reshape(1, -1)
    lb_all = jnp.cumsum(jax.nn.softmax(hg_lb_logits.astype(F32), axis=0), axis=0)
    xt = x.reshape(t, d)
    for l in range(DEPTH):
        nsa_cols = NSA_Q_COLS + 6 * NSA_KV_COLS
        w_l = w_in[l]
        w_perm = jnp.concatenate(
            [w_l[:, :nsa_cols], w_l[:, nsa_cols + NSA_GATE_COLS:], w_l[:, nsa_cols:nsa_cols + NSA_GATE_COLS],
             jnp.zeros((d, LANES - NSA_GATE_COLS), w_l.dtype)], axis=1)
        h, ks, kw, vsT, vwT = _inproj(xt, w_perm.astype(BF16), b, s, 256)
        nkc = h[:, NSA_Q_COLS:NSA_Q_COLS + NSA_KV_COLS]
        nvc = h[:, NSA_Q_COLS + NSA_KV_COLS:NSA_Q_COLS + 2 * NSA_KV_COLS]
        hg_col0 = nsa_cols // LANES
        gate_tile = (nsa_cols + 4 * HG_COLS) // LANES

        chunks = lambda a: jnp.transpose(a.reshape(b, nch, CMP_STRIDE, g, dh), (0, 3, 1, 2, 4)).reshape(
            b, g, nch, CMP_STRIDE * dh)
        comp = _compress(jnp.stack([chunks(nkc), chunks(nvc)]),
                         jnp.stack([cmp_pe_k[l].reshape(1, -1), cmp_pe_v[l].reshape(1, -1)]),
                         jnp.stack([cmp_w1_k[l], cmp_w1_v[l]]),
                         jnp.stack([cmp_w2_k[l], cmp_w2_v[l]]))
        kc = comp[0].astype(BF16)
        vcT = jnp.swapaxes(comp[1], -1, -2).astype(BF16)

        o_nsa = _nsa(h, kc, vcT, ks, vsT, kw, vwT, s, gate_tile)

        o_hg = _hgrn(h, lb_all[l].reshape(HG_HEADS, 1, HG_KEY_DIM), row(hg_norm_g[l]), b, s, hg_col0)

        x1 = _mix(o_nsa, o_hg, xt, w_out[l].astype(BF16), row(nsa_norm_g[l]), row(ln1_g[l]), row(ln1_b[l]), 256)

        n_mem = mem.shape[1]
        kv = _matmul(mem.reshape(b * n_mem, d),
                     jnp.concatenate([xa_wk[l], xa_wv[l]], axis=1).astype(BF16), n_mem)
        kT = jnp.swapaxes(kv[:, :d].reshape(b, n_mem, d), 1, 2).astype(BF16)
        v = kv[:, d:].reshape(b, n_mem, d).astype(BF16)
        x2 = _xattn(x1.reshape(b, s, d), xa_wq[l].astype(BF16), kT, v, xa_wo[l].astype(BF16),
                    row(ln2_g[l]), row(ln2_b[l]), 256).reshape(t, d)

        w_r = jnp.concatenate([moe_w_group[l], moe_w_expert[l]], axis=1)
        w_r = jnp.pad(w_r, ((0, 0), (0, LANES - w_r.shape[1])))
        b_r = jnp.pad(jnp.concatenate([moe_b_group[l], moe_b_expert[l]]), (0, LANES - MOE_GROUPS - MOE_N_EXPERTS))
        w_rh, w_rl = _split2(w_r)
        ids, wts, rnk, counts = _route(x2, w_rh, w_rl, row(b_r), 512)
        dest, bexp, nact, zfill, cap = _dispatch_plan(
            ids[:, :MOE_TOP_K], rnk[:, :MOE_TOP_K], counts[0, MOE_GROUPS:MOE_GROUPS + MOE_N_EXPERTS], t)
        xs = _dispatch(dest, zfill, x2, cap, 256)
        ys = _experts(bexp, nact, xs, moe_w_gate[l].astype(BF16), moe_w_up[l].astype(BF16),
                      moe_w_down[l].astype(BF16))
        xt = _combine(dest, ys, x2, wts, row(ln3_g[l]), row(ln3_b[l]), 256)
    return xt.reshape(b, s, d)
```

```python
import functools
import math

import jax
import jax.numpy as jnp
from jax import lax
from jax.experimental import pallas as pl
from jax.experimental.pallas import tpu as pltpu

F32 = jnp.float32
BF16 = jnp.bfloat16

NSA_HEAD_DIM = 64
NSA_HEADS = 8
NSA_KV_GROUPS = 2
NSA_HPG = NSA_HEADS // NSA_KV_GROUPS
CMP_BLOCK = 32
CMP_STRIDE = 16
CMP_HIDDEN = 256
SEL_BLOCK = 64
SEL_TOP_N = 16
WINDOW = 512
FORCE_BONUS = 1.0e4
HG_KEY_DIM = 128
HG_VAL_DIM = 128
HG_HEADS = 4
XA_HEADS = 4
MOE_GROUPS = 4
MOE_EXPERTS_PER_GROUP = 8
MOE_N_EXPERTS = MOE_GROUPS * MOE_EXPERTS_PER_GROUP
MOE_TOP_K = 2
DEPTH = 1
DEEPNORM_ALPHA = (2.0 * DEPTH) ** 0.25
LN_EPS = 1e-5
RMS_EPS = 1e-6
NEG_INF = -1e30

NSA_Q_COLS = NSA_HEADS * NSA_HEAD_DIM
NSA_KV_COLS = NSA_KV_GROUPS * NSA_HEAD_DIM
NSA_GATE_COLS = NSA_HEADS * 3
HG_COLS = HG_HEADS * HG_KEY_DIM

LANES = 128
Q_TILE = 128
NSA_ROWS = NSA_HPG * Q_TILE
SEL_KV_TILE = 512
SEL_GROUP = 4
NSA_PARTS = 4
NSA_V_ROWS = NSA_HEAD_DIM + 16
LOG2_E = 1.4426950408889634
HG_CHUNK = 128
HG_SUB = 16
HG_ROWS = 512
HG_HEADS_PER_STEP = 2
HG_FACTOR_LIMIT = 60.0
MOE_BLOCK = 256
VMEM_LIMIT = 48 * 1024 * 1024


def _dot(a, b):
    return jnp.dot(a, b, preferred_element_type=F32)


def _dot_nt(a, b):
    return lax.dot_general(a, b, (((1,), (1,)), ((), ())), preferred_element_type=F32)


def _split2(a):
    hi = a.astype(BF16)
    lo = (a - hi.astype(F32)).astype(BF16)
    return hi, lo


def _split3(a):
    p1 = a.astype(BF16)
    r1 = a - p1.astype(F32)
    p2 = r1.astype(BF16)
    p3 = (r1 - p2.astype(F32)).astype(BF16)
    return p1, p2, p3


def _dot3(a, b):
    ah, al = _split2(a)
    bh, bl = _split2(b)
    return _dot(ah, bh) + _dot(ah, bl) + _dot(al, bh)


def _sigmoid(x):
    return 1.0 / (1.0 + jnp.exp(-x))


def _layer_norm(y, g, b):
    mu = jnp.mean(y, axis=-1, keepdims=True)
    d = y - mu
    var = jnp.mean(d * d, axis=-1, keepdims=True)
    return d * lax.rsqrt(var + LN_EPS) * g + b


def _params(sem):
    return pltpu.CompilerParams(dimension_semantics=sem, vmem_limit_bytes=VMEM_LIMIT)


def _matmul_kernel(x_ref, w_ref, o_ref):
    o_ref[...] = _dot(x_ref[...].astype(BF16), w_ref[...]).astype(o_ref.dtype)


def _matmul(x, w, tm):
    m, k = x.shape
    n = w.shape[1]
    return pl.pallas_call(
        _matmul_kernel,
        grid=(m // tm,),
        in_specs=[pl.BlockSpec((tm, k), lambda i: (i, 0)),
                  pl.BlockSpec((k, n), lambda i: (0, 0))],
        out_specs=pl.BlockSpec((tm, n), lambda i: (i, 0)),
        out_shape=jax.ShapeDtypeStruct((m, n), F32),
        compiler_params=_params(("parallel",)),
        name="proj",
    )(x, w)


def _inproj_kernel(x_ref, w_ref, h_ref, ks_ref, kw_ref, vsT_ref, vwT_ref, *, tiles_per_seq):
    h = _dot(x_ref[...].astype(BF16), w_ref[...])
    h_ref[...] = h
    tm = h.shape[0]
    dh = NSA_HEAD_DIM
    k_sel0 = NSA_Q_COLS + 2 * NSA_KV_COLS
    v_sel0, k_win0, v_win0 = k_sel0 + NSA_KV_COLS, k_sel0 + 2 * NSA_KV_COLS, k_sel0 + 3 * NSA_KV_COLS
    pos = lax.rem(pl.program_id(0), tiles_per_seq) * tm + lax.broadcasted_iota(jnp.int32, (tm, 1), 0)
    blk = lax.shift_right_logical(pos & (SEL_KV_TILE - 1), SEL_BLOCK.bit_length() - 1)
    onehot = jnp.where(blk == lax.broadcasted_iota(jnp.int32, (1, LANES - dh), 1), 1.0, 0.0)
    ones_row = jnp.where(lax.broadcasted_iota(jnp.int32, (NSA_V_ROWS - dh, tm), 0) == 0, 1.0, 0.0)
    vsT = h[:, v_sel0:v_sel0 + NSA_KV_COLS].T
    vwT = h[:, v_win0:v_win0 + NSA_KV_COLS].T
    for g in range(NSA_KV_GROUPS):
        cols = slice(g * dh, (g + 1) * dh)
        ks_ref[0, g] = jnp.concatenate([h[:, k_sel0 + g * dh:k_sel0 + (g + 1) * dh], onehot], axis=1).astype(BF16)
        kw_ref[0, g] = h[:, k_win0 + g * dh:k_win0 + (g + 1) * dh].astype(BF16)
        vsT_ref[0, g] = jnp.concatenate([vsT[cols], ones_row], axis=0).astype(BF16)
        vwT_ref[0, g] = jnp.concatenate([vwT[cols], ones_row], axis=0).astype(BF16)


def _inproj(x, w, bsz, seq, tm):
    t, k = x.shape
    n = w.shape[1]
    g, dh = NSA_KV_GROUPS, NSA_HEAD_DIM
    tps = seq // tm
    keys = lambda width: pl.BlockSpec((1, g, tm, width), lambda i: (i // tps, 0, i % tps, 0))
    vals = pl.BlockSpec((1, g, NSA_V_ROWS, tm), lambda i: (i // tps, 0, 0, i % tps))
    return pl.pallas_call(
        functools.partial(_inproj_kernel, tiles_per_seq=tps),
        grid=(t // tm,),
        in_specs=[pl.BlockSpec((tm, k), lambda i: (i, 0)),
                  pl.BlockSpec((k, n), lambda i: (0, 0))],
        out_specs=[pl.BlockSpec((tm, n), lambda i: (i, 0)), keys(LANES), keys(dh), vals, vals],
        out_shape=[jax.ShapeDtypeStruct((t, n), F32),
                   jax.ShapeDtypeStruct((bsz, g, seq, LANES), BF16),
                   jax.ShapeDtypeStruct((bsz, g, seq, dh), BF16),
                   jax.ShapeDtypeStruct((bsz, g, NSA_V_ROWS, seq), BF16),
                   jax.ShapeDtypeStruct((bsz, g, NSA_V_ROWS, seq), BF16)],
        compiler_params=_params(("parallel",)),
        name="inproj",
    )(x, w)


def _compress_kernel(ch_ref, pe_ref, w1_ref, w2_ref, o_ref):
    ch = ch_ref[0, 0, 0]
    half = ch.shape[1]
    nch = ch.shape[0]
    pe = pe_ref[0]
    w1 = w1_ref[0]
    top = _dot3(ch + pe[:, :half], w1[:half])
    bot = _dot3(ch + pe[:, half:], w1[half:])
    hid = top + pltpu.roll(bot, nch - 1, 0)
    c = 0.7978845608028654
    act = 0.5 * hid * (1.0 + jnp.tanh(c * (hid + 0.044715 * hid * hid * hid)))
    out = _dot3(act, w2_ref[0])
    row = lax.broadcasted_iota(jnp.int32, out.shape, 0)
    o_ref[0, 0, 0] = jnp.where(row < nch - 1, out, 0.0)


def _compress(ch, pe, w1, w2):
    _, b, g, nch, width = ch.shape
    hidden = w1.shape[-1]
    d = w2.shape[-1]
    return pl.pallas_call(
        _compress_kernel,
        grid=(2, b, g),
        in_specs=[pl.BlockSpec((1, 1, 1, nch, width), lambda a, i, j: (a, i, j, 0, 0)),
                  pl.BlockSpec((1, 1, 2 * width), lambda a, i, j: (a, 0, 0)),
                  pl.BlockSpec((1, 2 * width, hidden), lambda a, i, j: (a, 0, 0)),
                  pl.BlockSpec((1, hidden, d), lambda a, i, j: (a, 0, 0))],
        out_specs=pl.BlockSpec((1, 1, 1, nch, d), lambda a, i, j: (a, i, j, 0, 0)),
        out_shape=jax.ShapeDtypeStruct((2, b, g, nch, d), F32),
        compiler_params=_params(("parallel", "parallel", "parallel")),
        name="compress",
    )(ch, pe, w1, w2)


def _tile_heads(a):
    return jnp.concatenate([a] * NSA_HPG, axis=1)


def _nsa_kernel(q_ref, kc_ref, vcT_ref, ks_ref, vsT_ref, kw_ref, vwT_ref, gate_ref,
                o_ref, sel_ref, qx_ref, psum_ref, gt_ref, *, seq):
    s0 = pl.program_id(2) * Q_TILE
    nsel = seq // SEL_BLOCK
    ncmp_pad = seq // CMP_STRIDE
    dh = NSA_HEAD_DIM

    q_t = q_ref[...].T
    q = jnp.concatenate([q_t[r * dh:(r + 1) * dh] for r in range(NSA_HPG)], axis=1)
    q_hi = (q * (dh ** -0.5 * LOG2_E)).astype(BF16)
    t_q = s0 + lax.broadcasted_iota(jnp.int32, (1, Q_TILE), 1)
    t_all = _tile_heads(t_q)

    def window_branch():
        span = WINDOW + Q_TILE
        lo = pl.multiple_of(jnp.maximum(s0 - WINDOW, 0), Q_TILE)
        kpos = lo + lax.broadcasted_iota(jnp.int32, (span, 1), 0)
        sw = _dot(kw_ref[0, 0, pl.ds(lo, span), :], q_hi)
        sw = sw + _tile_heads(jnp.where((kpos <= t_q) & (kpos > t_q - WINDOW), 0.0, NEG_INF))
        e_w = jnp.exp2(sw - jnp.max(sw, axis=0, keepdims=True)).astype(BF16)
        acc_w = _dot(vwT_ref[0, 0, :, pl.ds(lo, span)], e_w)
        return acc_w[0:dh] * (1.0 / acc_w[dh:dh + 1])

    def compressed_and_select(parts):
        ncmp = parts * (ncmp_pad // NSA_PARTS)
        rows = parts * (nsel // NSA_PARTS)
        o_w = window_branch()
        sc = _dot(kc_ref[0, 0, 0:ncmp, :], q_hi)
        n_end = lax.broadcasted_iota(jnp.int32, (ncmp, 1), 0) * CMP_STRIDE + (CMP_BLOCK - 1)
        sc = sc + _tile_heads(jnp.where(n_end <= t_q, 0.0, NEG_INF))
        e_c = jnp.exp2(sc - jnp.max(sc, axis=0, keepdims=True))
        l_c = jnp.sum(e_c, axis=0, keepdims=True)
        p_c = e_c * jnp.where(t_all >= CMP_BLOCK - 1, 1.0 / l_c, 0.0)
        o_c = _dot(vcT_ref[0, 0, :, 0:ncmp], p_c.astype(BF16))

        p_sum = p_c[:, 0:Q_TILE]
        for r in range(1, NSA_HPG):
            p_sum = p_sum + p_c[:, r * Q_TILE:(r + 1) * Q_TILE]
        ratio = SEL_BLOCK // CMP_STRIDE
        first_row = 8 - (CMP_BLOCK // CMP_STRIDE - 1)
        psum_ref[0:8, :] = jnp.zeros((8, Q_TILE), F32)
        psum_ref[8:8 + ncmp, :] = p_sum
        imp = psum_ref[pl.ds(first_row, rows, stride=ratio), :]
        for m in range(1, ratio + CMP_BLOCK // CMP_STRIDE - 1):
            imp = imp + psum_ref[pl.ds(first_row + m, rows, stride=ratio), :]
        j_idx = lax.broadcasted_iota(jnp.int32, (rows, Q_TILE), 0)
        cur = lax.shift_right_logical(t_q, SEL_BLOCK.bit_length() - 1)
        forced = (j_idx == 0) | (j_idx == cur) | (j_idx == cur - 1)
        score = jnp.where(j_idx <= cur, imp + jnp.where(forced, FORCE_BONUS, 0.0), -1.0)
        j_f = j_idx.astype(F32)
        sel = jnp.zeros((rows, Q_TILE), F32)
        for _ in range(min(SEL_TOP_N, rows)):
            best = jnp.max(score, axis=0, keepdims=True)
            first = jnp.min(jnp.where(score == best, j_f, float(rows)), axis=0, keepdims=True)
            pick = j_f == first
            sel = jnp.where(pick, 1.0, sel)
            score = jnp.where(pick, -3.0e38, score)
        sel_ref[0:rows, :] = sel
        if parts < NSA_PARTS:
            sel_ref[rows:, :] = jnp.zeros((nsel - rows, Q_TILE), F32)
        return o_c, o_w

    part_len = seq // NSA_PARTS
    parts_needed = lax.div(s0 + (Q_TILE + part_len - 1), part_len)
    o_c, o_w = lax.switch(parts_needed - 1,
                     [functools.partial(compressed_and_select, n) for n in range(1, NSA_PARTS + 1)])

    blocks_per_tile = SEL_KV_TILE // SEL_BLOCK
    for slot in range(SEL_GROUP):
        qx_ref[slot, 0:dh, :] = q_hi
        qx_ref[slot, dh:, :] = jnp.zeros((qx_ref.shape[1] - dh, NSA_ROWS), BF16)
    k_off = lax.broadcasted_iota(jnp.int32, (SEL_KV_TILE, 1), 0)

    def sel_scores(j, slot, causal):
        base = pl.multiple_of(j * SEL_KV_TILE, SEL_KV_TILE)
        member = sel_ref[pl.ds(pl.multiple_of(j * blocks_per_tile, blocks_per_tile), blocks_per_tile), :]
        bias = _tile_heads(jnp.where(member > 0.5, 0.0, NEG_INF))
        qx_ref[slot, dh:dh + 2 * blocks_per_tile, :] = jnp.concatenate(
            [bias, jnp.zeros_like(bias)], axis=0).astype(BF16)
        s = _dot(ks_ref[0, 0, pl.ds(base, SEL_KV_TILE), :], qx_ref[slot])
        if causal:
            s = s + _tile_heads(jnp.where((base + k_off) <= t_q, 0.0, NEG_INF))
        return s

    def sel_update(j, s, m_i, acc):
        base = pl.multiple_of(j * SEL_KV_TILE, SEL_KV_TILE)
        m_new = jnp.maximum(m_i, jnp.max(s, axis=0, keepdims=True))
        p = jnp.exp2(s - m_new).astype(BF16)
        pv = _dot(vsT_ref[0, 0, :, pl.ds(base, SEL_KV_TILE)], p)
        return m_new, jnp.exp2(m_i - m_new) * acc + pv

    def sel_group(first_tile, carry, n_tiles, causal_last):
        tiles = [first_tile + u for u in range(n_tiles)]
        scores = [sel_scores(j, u, causal_last and u == n_tiles - 1) for u, j in enumerate(tiles)]
        for j, s in zip(tiles, scores):
            carry = sel_update(j, s, *carry)
        return carry

    n_tiles = lax.div(s0 + (Q_TILE + SEL_KV_TILE - 1), SEL_KV_TILE)
    n_main = lax.div(n_tiles - 1, SEL_GROUP)
    init = (jnp.full((1, NSA_ROWS), NEG_INF, F32), jnp.zeros((vsT_ref.shape[2], NSA_ROWS), F32))
    carry = lax.fori_loop(0, n_main, lambda i, c: sel_group(SEL_GROUP * i, c, SEL_GROUP, False), init)
    _, acc_s = lax.switch(
        n_tiles - SEL_GROUP * n_main - 1,
        [functools.partial(sel_group, n_tiles=n, causal_last=True) for n in range(1, SEL_GROUP + 1)],
        SEL_GROUP * n_main, carry)
    o_s = acc_s[0:dh] * (1.0 / acc_s[dh:dh + 1])

    gt_ref[...] = gate_ref[...].T
    g_row0 = pl.program_id(1) * (NSA_HPG * 3)

    def gate(branch):
        return _sigmoid(jnp.concatenate(
            [gt_ref[pl.ds(g_row0 + 3 * r + branch, 1), :] for r in range(NSA_HPG)], axis=1))

    o = gate(0) * o_c + gate(1) * o_s + gate(2) * o_w
    o_rows = jnp.concatenate([o[:, r * Q_TILE:(r + 1) * Q_TILE] for r in range(NSA_HPG)], axis=0)
    o_ref[...] = o_rows.T


def _nsa(h, kc, vcT, ks, vsT, kw, vwT, seq, gate_tile):
    b, g = ks.shape[:2]
    nqb = seq // Q_TILE
    d = NSA_HEAD_DIM
    ncp = seq // CMP_STRIDE
    nsel = seq // SEL_BLOCK
    dk = ks.shape[-1]
    dv = vsT.shape[2]
    per_bg = lambda i, j, k: (i, j, 0, 0)
    q_rows = lambda i, j, k: (i * nqb + k, j)
    return pl.pallas_call(
        functools.partial(_nsa_kernel, seq=seq),
        grid=(b, g, nqb),
        in_specs=[pl.BlockSpec((Q_TILE, NSA_HPG * d), q_rows),
                  pl.BlockSpec((1, 1, ncp, d), per_bg),
                  pl.BlockSpec((1, 1, d, ncp), per_bg),
                  pl.BlockSpec((1, 1, seq, dk), per_bg),
                  pl.BlockSpec((1, 1, dv, seq), per_bg),
                  pl.BlockSpec((1, 1, seq, d), per_bg),
                  pl.BlockSpec((1, 1, dv, seq), per_bg),
                  pl.BlockSpec((Q_TILE, LANES), lambda i, j, k: (i * nqb + k, gate_tile))],
        out_specs=pl.BlockSpec((Q_TILE, NSA_HPG * d), q_rows),
        out_shape=jax.ShapeDtypeStruct((b * seq, NSA_Q_COLS), F32),
        scratch_shapes=[pltpu.VMEM((nsel, Q_TILE), F32), pltpu.VMEM((SEL_GROUP, dk, NSA_ROWS), BF16),
                        pltpu.VMEM((ncp + 8, Q_TILE), F32), pltpu.VMEM((LANES, Q_TILE), F32)],
        compiler_params=_params(("parallel", "parallel", "arbitrary")),
        name="nsa",
    )(h, kc, vcT, ks, vsT, kw, vwT, h)


def _hgrn_chunk(q, f, v, lb, state_t, shift_ref, lower, gstart, dmat, off_mask, factored):
    c = HG_CHUNK
    log_f = jnp.log(lb + (1.0 - lb) * _sigmoid(f))
    kk = (1.0 - lb) * _sigmoid(-f)
    l1, l2, l3 = _split3(log_f)
    bt = _dot(lower, l1) + _dot(lower, l2) + _dot(lower, l3)
    bs = _dot(gstart, l1) + _dot(gstart, l2) + _dot(gstart, l3)

    qh = (q * jnp.exp(bt - bs)).astype(BF16)

    def same_block_direct():
        shift_ref[0, HG_SUB:, :] = kk
        shift_ref[1, HG_SUB:, :] = bt
        ones = jnp.ones((kk.shape[1], c), BF16)
        acc = jnp.zeros((c, c), F32)
        for d in range(HG_SUB):
            if d == 0:
                prod = q * kk
            else:
                rows = pl.ds(HG_SUB - d, c)
                prod = (q * shift_ref[0, rows, :]) * jnp.exp(bt - shift_ref[1, rows, :])
            band = _dot(prod.astype(BF16), ones)
            acc = jnp.where(dmat == d, band, acc)
        return acc

    def same_block_factored():
        return _dot_nt(qh, (kk * jnp.exp(bs - bt)).astype(BF16))

    att = same_block_factored() if factored else same_block_direct()
    att = jnp.where(dmat >= 0, att, 0.0)

    blocks = [jnp.zeros((HG_SUB, c), F32)]
    for i in range(1, c // HG_SUB):
        b_i = bs[i * HG_SUB:i * HG_SUB + 1, :]
        kh = kk * jnp.exp(jnp.minimum(b_i - bt, 0.0))
        blocks.append(_dot_nt(qh[i * HG_SUB:(i + 1) * HG_SUB], kh.astype(BF16)))
    att = jnp.where(off_mask, jnp.concatenate(blocks, axis=0), att)

    vb = v.astype(BF16)
    o = _dot_nt((q * jnp.exp(bt)).astype(BF16), state_t.astype(BF16)) + _dot(att.astype(BF16), vb)
    b_last = bt[c - 1:c, :]
    k_dec = kk * jnp.exp(b_last - bt)
    return o, state_t * jnp.exp(b_last) + _dot(v.T.astype(BF16), k_dec.astype(BF16))


def _hgrn_kernel(q_ref, f_ref, v_ref, gate_ref, lb_ref, ng_ref, o_ref, state_ref, shift_ref):
    c = HG_CHUNK

    @pl.when(pl.program_id(2) == 0)
    def _():
        state_ref[...] = jnp.zeros_like(state_ref)
        shift_ref[...] = jnp.zeros_like(shift_ref)

    row = lax.broadcasted_iota(jnp.int32, (c, c), 0)
    col = lax.broadcasted_iota(jnp.int32, (c, c), 1)
    sub = HG_SUB.bit_length() - 1
    row_blk = lax.shift_right_logical(row, sub)
    col_blk = lax.shift_right_logical(col, sub)
    lower = jnp.where(col <= row, 1.0, 0.0).astype(BF16)
    gstart = jnp.where(col_blk < row_blk, 1.0, 0.0).astype(BF16)
    dmat = jnp.where(row_blk == col_blk, row - col, -1)
    off_mask = col_blk < row_blk
    ng = ng_ref[...]

    def step(i, states, factored):
        rows = pl.ds(pl.multiple_of(i * c, c), c)
        new_states = []
        for hd in range(HG_HEADS_PER_STEP):
            lanes = slice(hd * HG_KEY_DIM, (hd + 1) * HG_KEY_DIM)
            o, state = _hgrn_chunk(q_ref[rows, lanes], f_ref[rows, lanes], v_ref[rows, lanes], lb_ref[hd],
                                   states[hd], shift_ref.at[hd], lower, gstart, dmat, off_mask, factored)
            gate = gate_ref[rows, lanes]
            ms = jnp.mean(o * o, axis=-1, keepdims=True)
            o_ref[rows, lanes] = o * lax.rsqrt(ms + RMS_EPS) * ng * (gate * _sigmoid(gate))
            new_states.append(state)
        return tuple(new_states)

    def run(factored):
        init = tuple(state_ref[hd] for hd in range(HG_HEADS_PER_STEP))
        states = lax.fori_loop(0, HG_ROWS // c, functools.partial(step, factored=factored), init)
        for hd, state in enumerate(states):
            state_ref[hd] = state

    can_factor = jnp.min(lb_ref[...]) > math.exp(-HG_FACTOR_LIMIT / HG_SUB)
    lax.cond(can_factor, functools.partial(run, True), functools.partial(run, False))


def _hgrn(h, lb, ng, bsz, seq, col0):
    nblk = seq // HG_ROWS
    hps = HG_HEADS_PER_STEP
    width = hps * HG_KEY_DIM
    tiles = HG_COLS // width
    blk = lambda k: pl.BlockSpec((HG_ROWS, width),
                                 lambda i, j, c: (i * nblk + c, (col0 * LANES) // width + k * tiles + j))
    return pl.pallas_call(
        _hgrn_kernel,
        grid=(bsz, HG_HEADS // hps, nblk),
        in_specs=[blk(0), blk(1), blk(2), blk(3),
                  pl.BlockSpec((hps, 1, HG_KEY_DIM), lambda i, j, c: (j, 0, 0)),
                  pl.BlockSpec((1, HG_VAL_DIM), lambda i, j, c: (0, 0))],
        out_specs=pl.BlockSpec((HG_ROWS, width), lambda i, j, c: (i * nblk + c, j)),
        out_shape=jax.ShapeDtypeStruct((bsz * seq, HG_COLS), F32),
        scratch_shapes=[pltpu.VMEM((hps, HG_VAL_DIM, HG_KEY_DIM), F32),
                        pltpu.VMEM((hps, 2, HG_SUB + HG_CHUNK, HG_KEY_DIM), F32)],
        compiler_params=_params(("parallel", "parallel", "arbitrary")),
        name="hgrn",
    )(h, h, h, h, lb, ng)


def _mix_kernel(nsa_ref, hg_ref, x_ref, w_ref, ng_ref, g_ref, b_ref, o_ref):
    o_n = nsa_ref[...]
    half = o_n.shape[1]
    o_n = o_n * lax.rsqrt(jnp.mean(o_n * o_n, axis=-1, keepdims=True) + RMS_EPS) * ng_ref[...]
    mix = _dot(o_n.astype(BF16), w_ref[:half]) + _dot(hg_ref[...].astype(BF16), w_ref[half:])
    o_ref[...] = _layer_norm(DEEPNORM_ALPHA * x_ref[...] + mix, g_ref[...], b_ref[...])


def _mix(o_nsa, o_hg, x, w_out, ng, g, b, tm):
    t, d = x.shape
    half = o_nsa.shape[1]
    rows = lambda w: pl.BlockSpec((tm, w), lambda i: (i, 0))
    full = lambda r, c: pl.BlockSpec((r, c), lambda i: (0, 0))
    return pl.pallas_call(
        _mix_kernel,
        grid=(t // tm,),
        in_specs=[rows(half), rows(half), rows(d), full(d, d), full(1, half), full(1, d), full(1, d)],
        out_specs=rows(d),
        out_shape=jax.ShapeDtypeStruct((t, d), F32),
        compiler_params=_params(("parallel",)),
        name="mix",
    )(o_nsa, o_hg, x, w_out, ng, g, b)


def _xattn_kernel(x_ref, wq_ref, kT_ref, v_ref, wo_ref, g_ref, b_ref, o_ref):
    x = x_ref[0]
    d = x.shape[1]
    dh = d // XA_HEADS
    q = _dot(x.astype(BF16), wq_ref[...])
    heads = []
    for h in range(XA_HEADS):
        cols = slice(h * dh, (h + 1) * dh)
        s = _dot(q[:, cols].astype(BF16), kT_ref[0, cols, :]) * (dh ** -0.5)
        e = jnp.exp(s - jnp.max(s, axis=-1, keepdims=True))
        p = e * (1.0 / jnp.sum(e, axis=-1, keepdims=True))
        heads.append(_dot(p.astype(BF16), v_ref[0, :, cols]))
    o = jnp.concatenate(heads, axis=1)
    xa = _dot(o.astype(BF16), wo_ref[...])
    o_ref[0] = _layer_norm(DEEPNORM_ALPHA * x + xa, g_ref[...], b_ref[...])


def _xattn(x, wq, kT, v, wo, g, b, tm):
    bsz, seq, d = x.shape
    m = v.shape[1]
    full = lambda r, c: pl.BlockSpec((r, c), lambda i, j: (0, 0))
    return pl.pallas_call(
        _xattn_kernel,
        grid=(bsz, seq // tm),
        in_specs=[pl.BlockSpec((1, tm, d), lambda i, j: (i, j, 0)),
                  full(d, d),
                  pl.BlockSpec((1, d, m), lambda i, j: (i, 0, 0)),
                  pl.BlockSpec((1, m, d), lambda i, j: (i, 0, 0)),
                  full(d, d), full(1, d), full(1, d)],
        out_specs=pl.BlockSpec((1, tm, d), lambda i, j: (i, j, 0)),
        out_shape=jax.ShapeDtypeStruct((bsz, seq, d), F32),
        compiler_params=_params(("parallel", "parallel")),
        name="xattn",
    )(x, wq, kT, v, wo, g, b)


def _route_kernel(x_ref, wh_ref, wl_ref, bias_ref, ids_ref, wts_ref, rnk_ref, counts_ref, cnt_ref):
    xh, xl = _split2(x_ref[...])
    logits = _dot(xh, wh_ref[...]) + _dot(xh, wl_ref[...]) + _dot(xl, wh_ref[...]) + bias_ref[...]
    lane = lax.broadcasted_iota(jnp.int32, logits.shape, 1)
    lane_f = lane.astype(F32)
    big = float(LANES)

    is_g = lane < MOE_GROUPS
    g_max = jnp.max(jnp.where(is_g, logits, NEG_INF), axis=-1, keepdims=True)
    g_sum = jnp.sum(jnp.where(is_g, jnp.exp(logits - g_max), 0.0), axis=-1, keepdims=True)
    g_w = 1.0 / g_sum
    g_idx = jnp.min(jnp.where(is_g & (logits == g_max), lane_f, big), axis=-1, keepdims=True)

    e_lo = MOE_GROUPS + MOE_EXPERTS_PER_GROUP * g_idx
    is_e = (lane_f >= e_lo) & (lane_f < e_lo + MOE_EXPERTS_PER_GROUP)
    e_log = jnp.where(is_e, logits, NEG_INF)
    e_max = jnp.max(e_log, axis=-1, keepdims=True)
    e_exp = jnp.where(is_e, jnp.exp(logits - e_max), 0.0)
    e_sum = jnp.sum(e_exp, axis=-1, keepdims=True)
    i1 = jnp.min(jnp.where(is_e & (e_log == e_max), lane_f, big), axis=-1, keepdims=True)
    rest = jnp.where(lane_f == i1, NEG_INF, e_log)
    r_max = jnp.max(rest, axis=-1, keepdims=True)
    i2 = jnp.min(jnp.where(is_e & (lane_f != i1) & (rest == r_max), lane_f, big), axis=-1, keepdims=True)
    p1 = 1.0 / e_sum
    p2 = jnp.exp(r_max - e_max) / e_sum
    tot = p1 + p2
    w1 = g_w * (p1 / tot)
    w2 = g_w * (p2 / tot)
    ids = jnp.where(lane == 0, i1, i2) - float(MOE_GROUPS)
    ids_ref[...] = ids.astype(jnp.int32)
    wts_ref[...] = jnp.where(lane == 0, w1, w2)

    @pl.when(pl.program_id(0) == 0)
    def _():
        cnt_ref[...] = jnp.zeros_like(cnt_ref)

    tm = logits.shape[0]
    hit1 = lane_f == i1
    hit2 = lane_f == i2
    hits = jnp.where(hit1 | hit2, 1.0, 0.0)
    earlier = lax.broadcasted_iota(jnp.int32, (tm, tm), 1) < lax.broadcasted_iota(jnp.int32, (tm, tm), 0)
    before = _dot(jnp.where(earlier, 1.0, 0.0).astype(BF16), hits.astype(BF16)) + cnt_ref[...]
    rank1 = jnp.sum(jnp.where(hit1, before, 0.0), axis=-1, keepdims=True)
    rank2 = jnp.sum(jnp.where(hit2, before, 0.0), axis=-1, keepdims=True)
    rnk_ref[...] = jnp.where(lane == 0, rank1, rank2).astype(jnp.int32)
    cnt_ref[...] = cnt_ref[...] + jnp.sum(hits, axis=0, keepdims=True)
    counts_ref[...] = cnt_ref[...].astype(jnp.int32)


def _route(x, wh, wl, bias, tm):
    t, d = x.shape
    rows = lambda w: pl.BlockSpec((tm, w), lambda i: (i, 0))
    full = lambda r, c: pl.BlockSpec((r, c), lambda i: (0, 0))
    i32 = jnp.int32
    return pl.pallas_call(
        _route_kernel,
        grid=(t // tm,),
        in_specs=[rows(d), full(d, LANES), full(d, LANES), full(1, LANES)],
        out_specs=[rows(LANES), rows(LANES), rows(LANES), full(1, LANES)],
        out_shape=[jax.ShapeDtypeStruct((t, LANES), i32), jax.ShapeDtypeStruct((t, LANES), F32),
                   jax.ShapeDtypeStruct((t, LANES), i32), jax.ShapeDtypeStruct((1, LANES), i32)],
        scratch_shapes=[pltpu.VMEM((1, LANES), F32)],
        compiler_params=_params(("arbitrary",)),
        name="route",
    )(x, wh, wl, bias)


def _row_copy_wait(src, dst, sem, rows):
    pltpu.make_async_copy(src.at[pl.ds(0, rows)], dst.at[pl.ds(0, rows)], sem).wait()


def _dispatch_kernel(dest_ref, zfill_ref, x_ref, xs_hbm, zbuf, zsem, sem):
    i = pl.program_id(0)
    tm = x_ref.shape[0]

    @pl.when(i == 0)
    def _():
        zbuf[...] = jnp.zeros_like(zbuf)
        tail = lambda e: xs_hbm.at[pl.ds(pl.multiple_of(zfill_ref[e], MOE_BLOCK), MOE_BLOCK)]
        for e in range(zfill_ref.shape[0]):
            @pl.when(zfill_ref[e] >= 0)
            def _():
                pltpu.make_async_copy(zbuf, tail(e), zsem).start()
        for e in range(zfill_ref.shape[0]):
            @pl.when(zfill_ref[e] >= 0)
            def _():
                pltpu.make_async_copy(zbuf, tail(e), zsem).wait()

    def send(j, c):
        for k in range(MOE_TOP_K):
            slot = dest_ref[(i * tm + j) * MOE_TOP_K + k]
            pltpu.make_async_copy(x_ref.at[pl.ds(j, 1)], xs_hbm.at[pl.ds(slot, 1)], sem).start()
        return c

    lax.fori_loop(0, tm, send, 0, unroll=8)
    for _ in range(MOE_TOP_K):
        _row_copy_wait(x_ref, xs_hbm, sem, tm)


def _dispatch(dest, zfill, x, cap, tm):
    t, d = x.shape
    grid_spec = pltpu.PrefetchScalarGridSpec(
        num_scalar_prefetch=2,
        grid=(t // tm,),
        in_specs=[pl.BlockSpec((tm, d), lambda i, dest, zfill: (i, 0))],
        out_specs=pl.BlockSpec(memory_space=pl.ANY),
        scratch_shapes=[pltpu.VMEM((MOE_BLOCK, d), F32), pltpu.SemaphoreType.DMA, pltpu.SemaphoreType.DMA])
    return pl.pallas_call(
        _dispatch_kernel,
        grid_spec=grid_spec,
        out_shape=jax.ShapeDtypeStruct((cap, d), F32),
        compiler_params=_params(("arbitrary",)),
        name="dispatch",
    )(dest, zfill, x)


def _experts_kernel(bexp_ref, nact_ref, xs_ref, wg_ref, wu_ref, wd_ref, y_ref):
    @pl.when(pl.program_id(0) < nact_ref[0])
    def _():
        xb = xs_ref[...].astype(BF16)
        gate = _dot(xb, wg_ref[0])
        hid = (gate * _sigmoid(gate)) * _dot(xb, wu_ref[0])
        y_ref[...] = _dot(hid.astype(BF16), wd_ref[0])

    @pl.when(pl.program_id(0) >= nact_ref[0])
    def _():
        y_ref[...] = jnp.zeros_like(y_ref)


def _experts(bexp, nact, xs, wg, wu, wd):
    cap, d = xs.shape
    dff = wg.shape[-1]
    by_expert = lambda i, bexp, nact: (bexp[i], 0, 0)
    active = lambda i, bexp, nact: (jnp.minimum(i, nact[0] - 1), 0)
    grid_spec = pltpu.PrefetchScalarGridSpec(
        num_scalar_prefetch=2,
        grid=(cap // MOE_BLOCK,),
        in_specs=[pl.BlockSpec((MOE_BLOCK, d), active),
                  pl.BlockSpec((1, d, dff), by_expert),
                  pl.BlockSpec((1, d, dff), by_expert),
                  pl.BlockSpec((1, dff, d), by_expert)],
        out_specs=pl.BlockSpec((MOE_BLOCK, d), lambda i, bexp, nact: (i, 0)))
    return pl.pallas_call(
        _experts_kernel,
        grid_spec=grid_spec,
        out_shape=jax.ShapeDtypeStruct((cap, d), F32),
        compiler_params=_params(("arbitrary",)),
        name="experts",
    )(bexp, nact, xs, wg, wu, wd)


def _combine_kernel(dest_ref, y_hbm, x_ref, wts_ref, g_ref, b_ref, o_ref, ybuf, sem):
    i = pl.program_id(0)
    n = pl.num_programs(0)
    tm = x_ref.shape[0]
    cur = lax.rem(i, 2)

    def fetch(tile, buf):
        def one(j, c):
            for k in range(MOE_TOP_K):
                slot = dest_ref[(tile * tm + j) * MOE_TOP_K + k]
                pltpu.make_async_copy(y_hbm.at[pl.ds(slot, 1)], ybuf.at[buf, k, pl.ds(j, 1)], sem.at[buf]).start()
            return c
        lax.fori_loop(0, tm, one, 0, unroll=8)

    @pl.when(i == 0)
    def _():
        fetch(0, 0)

    @pl.when(i + 1 < n)
    def _():
        fetch(i + 1, 1 - cur)

    for k in range(MOE_TOP_K):
        _row_copy_wait(y_hbm, ybuf.at[cur, k], sem.at[cur], tm)
    w = wts_ref[...]
    ff = w[:, 0:1] * ybuf[cur, 0]
    for k in range(1, MOE_TOP_K):
        ff = ff + w[:, k:k + 1] * ybuf[cur, k]
    o_ref[...] = _layer_norm(DEEPNORM_ALPHA * x_ref[...] + ff, g_ref[...], b_ref[...])


def _combine(dest, y, x, wts, g, b, tm):
    t, d = x.shape
    rows = lambda w: pl.BlockSpec((tm, w), lambda i, dest: (i, 0))
    full = lambda r, c: pl.BlockSpec((r, c), lambda i, dest: (0, 0))
    grid_spec = pltpu.PrefetchScalarGridSpec(
        num_scalar_prefetch=1,
        grid=(t // tm,),
        in_specs=[pl.BlockSpec(memory_space=pl.ANY), rows(d), rows(LANES), full(1, d), full(1, d)],
        out_specs=rows(d),
        scratch_shapes=[pltpu.VMEM((2, MOE_TOP_K, tm, d), F32), pltpu.SemaphoreType.DMA((2,))])
    return pl.pallas_call(
        _combine_kernel,
        grid_spec=grid_spec,
        out_shape=jax.ShapeDtypeStruct((t, d), F32),
        compiler_params=_params(("arbitrary",)),
        name="combine",
    )(dest, y, x, wts, g, b)


def _dispatch_plan(ids, rnk, counts, n_tok):
    padded = ((counts + MOE_BLOCK - 1) // MOE_BLOCK) * MOE_BLOCK
    pend = jnp.cumsum(padded)
    pstart = pend - padded
    dest = (pstart[ids] + rnk).reshape(-1).astype(jnp.int32)
    cap = n_tok * MOE_TOP_K + MOE_N_EXPERTS * MOE_BLOCK
    n_blocks = cap // MOE_BLOCK
    block_start = jnp.arange(n_blocks, dtype=jnp.int32) * MOE_BLOCK
    bexp = jnp.minimum(jnp.sum(block_start[:, None] >= pend[None, :], axis=1), MOE_N_EXPERTS - 1).astype(jnp.int32)
    nact = (pend[-1] // MOE_BLOCK).astype(jnp.int32).reshape(1)
    tails = jnp.where(padded > 0, pend - MOE_BLOCK, -1)
    idle = pend[-1] + block_start[:MOE_N_EXPERTS]
    zfill = jnp.concatenate([tails, jnp.where(idle < cap, idle, -1)]).astype(jnp.int32)
    return dest, bexp, nact, zfill, cap


def kernel(x, mem, w_in, cmp_pe_k, cmp_pe_v, cmp_w1_k, cmp_w2_k, cmp_w1_v, cmp_w2_v, nsa_norm_g,
           hg_lb_logits, hg_norm_g, w_out, ln1_g, ln1_b, xa_wq, xa_wk, xa_wv, xa_wo, ln2_g, ln2_b,
           moe_w_group, moe_b_group, moe_w_expert, moe_b_expert, moe_w_gate, moe_w_up, moe_w_down,
           ln3_g, ln3_b):
    b, s, d = x.shape
    t = b * s
    g, dh = NSA_KV_GROUPS, NSA_HEAD_DIM
    nch = s // CMP_STRIDE
    row = lambda a: a.reshape(1, -1)
    lb_all = jnp.cumsum(jax.nn.softmax(hg_lb_logits.astype(F32), axis=0), axis=0)
    xt = x.reshape(t, d)
    for l in range(DEPTH):
        nsa_cols = NSA_Q_COLS + 6 * NSA_KV_COLS
        w_l = w_in[l]
        w_perm = jnp.concatenate(
            [w_l[:, :nsa_cols], w_l[:, nsa_cols + NSA_GATE_COLS:], w_l[:, nsa_cols:nsa_cols + NSA_GATE_COLS],
             jnp.zeros((d, LANES - NSA_GATE_COLS), w_l.dtype)], axis=1)
        h, ks, kw, vsT, vwT = _inproj(xt, w_perm.astype(BF16), b, s, 256)
        nkc = h[:, NSA_Q_COLS:NSA_Q_COLS + NSA_KV_COLS]
        nvc = h[:, NSA_Q_COLS + NSA_KV_COLS:NSA_Q_COLS + 2 * NSA_KV_COLS]
        hg_col0 = nsa_cols // LANES
        gate_tile = (nsa_cols + 4 * HG_COLS) // LANES

        chunks = lambda a: jnp.transpose(a.reshape(b, nch, CMP_STRIDE, g, dh), (0, 3, 1, 2, 4)).reshape(
            b, g, nch, CMP_STRIDE * dh)
        comp = _compress(jnp.stack([chunks(nkc), chunks(nvc)]),
                         jnp.stack([cmp_pe_k[l].reshape(1, -1), cmp_pe_v[l].reshape(1, -1)]),
                         jnp.stack([cmp_w1_k[l], cmp_w1_v[l]]),
                         jnp.stack([cmp_w2_k[l], cmp_w2_v[l]]))
        kc = comp[0].astype(BF16)
        vcT = jnp.swapaxes(comp[1], -1, -2).astype(BF16)

        o_nsa = _nsa(h, kc, vcT, ks, vsT, kw, vwT, s, gate_tile)

        o_hg = _hgrn(h, lb_all[l].reshape(HG_HEADS, 1, HG_KEY_DIM), row(hg_norm_g[l]), b, s, hg_col0)

        x1 = _mix(o_nsa, o_hg, xt, w_out[l].astype(BF16), row(nsa_norm_g[l]), row(ln1_g[l]), row(ln1_b[l]), 512)

        n_mem = mem.shape[1]
        kv = _matmul(mem.reshape(b * n_mem, d),
                     jnp.concatenate([xa_wk[l], xa_wv[l]], axis=1).astype(BF16), n_mem)
        kT = jnp.swapaxes(kv[:, :d].reshape(b, n_mem, d), 1, 2).astype(BF16)
        v = kv[:, d:].reshape(b, n_mem, d).astype(BF16)
        x2 = _xattn(x1.reshape(b, s, d), xa_wq[l].astype(BF16), kT, v, xa_wo[l].astype(BF16),
                    row(ln2_g[l]), row(ln2_b[l]), 512).reshape(t, d)

        w_r = jnp.concatenate([moe_w_group[l], moe_w_expert[l]], axis=1)
        w_r = jnp.pad(w_r, ((0, 0), (0, LANES - w_r.shape[1])))
        b_r = jnp.pad(jnp.concatenate([moe_b_group[l], moe_b_expert[l]]), (0, LANES - MOE_GROUPS - MOE_N_EXPERTS))
        w_rh, w_rl = _split2(w_r)
        ids, wts, rnk, counts = _route(x2, w_rh, w_rl, row(b_r), 512)
        dest, bexp, nact, zfill, cap = _dispatch_plan(
            ids[:, :MOE_TOP_K], rnk[:, :MOE_TOP_K], counts[0, MOE_GROUPS:MOE_GROUPS + MOE_N_EXPERTS], t)
        xs = _dispatch(dest, zfill, x2, cap, 256)
        ys = _experts(bexp, nact, xs, moe_w_gate[l].astype(BF16), moe_w_up[l].astype(BF16),
                      moe_w_down[l].astype(BF16))
        xt = _combine(dest, ys, x2, wts, row(ln3_g[l]), row(ln3_b[l]), 256)
    return xt.reshape(b, s, d)
```

```python
import functools
import math

import jax
import jax.numpy as jnp
from jax import lax
from jax.experimental import pallas as pl
from jax.experimental.pallas import tpu as pltpu

F32 = jnp.float32
BF16 = jnp.bfloat16

NSA_HEAD_DIM = 64
NSA_HEADS = 8
NSA_KV_GROUPS = 2
NSA_HPG = NSA_HEADS // NSA_KV_GROUPS
CMP_BLOCK = 32
CMP_STRIDE = 16
CMP_HIDDEN = 256
SEL_BLOCK = 64
SEL_TOP_N = 16
WINDOW = 512
FORCE_BONUS = 1.0e4
HG_KEY_DIM = 128
HG_VAL_DIM = 128
HG_HEADS = 4
XA_HEADS = 4
MOE_GROUPS = 4
MOE_EXPERTS_PER_GROUP = 8
MOE_N_EXPERTS = MOE_GROUPS * MOE_EXPERTS_PER_GROUP
MOE_TOP_K = 2
DEPTH = 1
DEEPNORM_ALPHA = (2.0 * DEPTH) ** 0.25
LN_EPS = 1e-5
RMS_EPS = 1e-6
NEG_INF = -1e30

NSA_Q_COLS = NSA_HEADS * NSA_HEAD_DIM
NSA_KV_COLS = NSA_KV_GROUPS * NSA_HEAD_DIM
NSA_GATE_COLS = NSA_HEADS * 3
HG_COLS = HG_HEADS * HG_KEY_DIM

LANES = 128
Q_TILE = 128
NSA_ROWS = NSA_HPG * Q_TILE
SEL_KV_TILE = 512
SEL_GROUP = 6
SEL_SPLIT = 2
NSA_PARTS = 4
NSA_V_ROWS = NSA_HEAD_DIM + 16
LOG2_E = 1.4426950408889634
HG_CHUNK = 128
HG_SUB = 16
HG_ROWS = 512
HG_HEADS_PER_STEP = 2
HG_FACTOR_LIMIT = 60.0
MOE_BLOCK = 256
VMEM_LIMIT = 48 * 1024 * 1024


def _dot(a, b):
    return jnp.dot(a, b, preferred_element_type=F32)


def _dot_nt(a, b):
    return lax.dot_general(a, b, (((1,), (1,)), ((), ())), preferred_element_type=F32)


def _split2(a):
    hi = a.astype(BF16)
    lo = (a - hi.astype(F32)).astype(BF16)
    return hi, lo


def _split3(a):
    p1 = a.astype(BF16)
    r1 = a - p1.astype(F32)
    p2 = r1.astype(BF16)
    p3 = (r1 - p2.astype(F32)).astype(BF16)
    return p1, p2, p3


def _dot3(a, b):
    ah, al = _split2(a)
    bh, bl = _split2(b)
    return _dot(ah, bh) + _dot(ah, bl) + _dot(al, bh)


def _sigmoid(x):
    return 1.0 / (1.0 + jnp.exp(-x))


def _layer_norm(y, g, b):
    mu = jnp.mean(y, axis=-1, keepdims=True)
    d = y - mu
    var = jnp.mean(d * d, axis=-1, keepdims=True)
    return d * lax.rsqrt(var + LN_EPS) * g + b


def _params(sem):
    return pltpu.CompilerParams(dimension_semantics=sem, vmem_limit_bytes=VMEM_LIMIT)


def _matmul_kernel(x_ref, w_ref, o_ref):
    o_ref[...] = _dot(x_ref[...].astype(BF16), w_ref[...]).astype(o_ref.dtype)


def _matmul(x, w, tm):
    m, k = x.shape
    n = w.shape[1]
    return pl.pallas_call(
        _matmul_kernel,
        grid=(m // tm,),
        in_specs=[pl.BlockSpec((tm, k), lambda i: (i, 0)),
                  pl.BlockSpec((k, n), lambda i: (0, 0))],
        out_specs=pl.BlockSpec((tm, n), lambda i: (i, 0)),
        out_shape=jax.ShapeDtypeStruct((m, n), F32),
        compiler_params=_params(("parallel",)),
        name="proj",
    )(x, w)


def _inproj_kernel(x_ref, w_ref, h_ref, ks_ref, kw_ref, vsT_ref, vwT_ref, *, tiles_per_seq):
    h = _dot(x_ref[...].astype(BF16), w_ref[...])
    h_ref[...] = h
    tm = h.shape[0]
    dh = NSA_HEAD_DIM
    k_sel0 = NSA_Q_COLS + 2 * NSA_KV_COLS
    v_sel0, k_win0, v_win0 = k_sel0 + NSA_KV_COLS, k_sel0 + 2 * NSA_KV_COLS, k_sel0 + 3 * NSA_KV_COLS
    pos = lax.rem(pl.program_id(0), tiles_per_seq) * tm + lax.broadcasted_iota(jnp.int32, (tm, 1), 0)
    blk = lax.shift_right_logical(pos & (SEL_KV_TILE - 1), SEL_BLOCK.bit_length() - 1)
    onehot = jnp.where(blk == lax.broadcasted_iota(jnp.int32, (1, LANES - dh), 1), 1.0, 0.0)
    ones_row = jnp.where(lax.broadcasted_iota(jnp.int32, (NSA_V_ROWS - dh, tm), 0) == 0, 1.0, 0.0)
    vsT = h[:, v_sel0:v_sel0 + NSA_KV_COLS].T
    vwT = h[:, v_win0:v_win0 + NSA_KV_COLS].T
    for g in range(NSA_KV_GROUPS):
        cols = slice(g * dh, (g + 1) * dh)
        ks_ref[0, g] = jnp.concatenate([h[:, k_sel0 + g * dh:k_sel0 + (g + 1) * dh], onehot], axis=1).astype(BF16)
        kw_ref[0, g] = h[:, k_win0 + g * dh:k_win0 + (g + 1) * dh].astype(BF16)
        vsT_ref[0, g] = jnp.concatenate([vsT[cols], ones_row], axis=0).astype(BF16)
        vwT_ref[0, g] = jnp.concatenate([vwT[cols], ones_row], axis=0).astype(BF16)


def _inproj(x, w, bsz, seq, tm):
    t, k = x.shape
    n = w.shape[1]
    g, dh = NSA_KV_GROUPS, NSA_HEAD_DIM
    tps = seq // tm
    keys = lambda width: pl.BlockSpec((1, g, tm, width), lambda i: (i // tps, 0, i % tps, 0))
    vals = pl.BlockSpec((1, g, NSA_V_ROWS, tm), lambda i: (i // tps, 0, 0, i % tps))
    return pl.pallas_call(
        functools.partial(_inproj_kernel, tiles_per_seq=tps),
        grid=(t // tm,),
        in_specs=[pl.BlockSpec((tm, k), lambda i: (i, 0)),
                  pl.BlockSpec((k, n), lambda i: (0, 0))],
        out_specs=[pl.BlockSpec((tm, n), lambda i: (i, 0)), keys(LANES), keys(dh), vals, vals],
        out_shape=[jax.ShapeDtypeStruct((t, n), F32),
                   jax.ShapeDtypeStruct((bsz, g, seq, LANES), BF16),
                   jax.ShapeDtypeStruct((bsz, g, seq, dh), BF16),
                   jax.ShapeDtypeStruct((bsz, g, NSA_V_ROWS, seq), BF16),
                   jax.ShapeDtypeStruct((bsz, g, NSA_V_ROWS, seq), BF16)],
        compiler_params=_params(("parallel",)),
        name="inproj",
    )(x, w)


def _compress_kernel(ch_ref, pe_ref, w1_ref, w2_ref, o_ref):
    ch = ch_ref[0, 0, 0]
    half = ch.shape[1]
    nch = ch.shape[0]
    pe = pe_ref[0]
    w1 = w1_ref[0]
    top = _dot3(ch + pe[:, :half], w1[:half])
    bot = _dot3(ch + pe[:, half:], w1[half:])
    hid = top + pltpu.roll(bot, nch - 1, 0)
    c = 0.7978845608028654
    act = 0.5 * hid * (1.0 + jnp.tanh(c * (hid + 0.044715 * hid * hid * hid)))
    out = _dot3(act, w2_ref[0])
    row = lax.broadcasted_iota(jnp.int32, out.shape, 0)
    o_ref[0, 0, 0] = jnp.where(row < nch - 1, out, 0.0)


def _compress(ch, pe, w1, w2):
    _, b, g, nch, width = ch.shape
    hidden = w1.shape[-1]
    d = w2.shape[-1]
    return pl.pallas_call(
        _compress_kernel,
        grid=(2, b, g),
        in_specs=[pl.BlockSpec((1, 1, 1, nch, width), lambda a, i, j: (a, i, j, 0, 0)),
                  pl.BlockSpec((1, 1, 2 * width), lambda a, i, j: (a, 0, 0)),
                  pl.BlockSpec((1, 2 * width, hidden), lambda a, i, j: (a, 0, 0)),
                  pl.BlockSpec((1, hidden, d), lambda a, i, j: (a, 0, 0))],
        out_specs=pl.BlockSpec((1, 1, 1, nch, d), lambda a, i, j: (a, i, j, 0, 0)),
        out_shape=jax.ShapeDtypeStruct((2, b, g, nch, d), F32),
        compiler_params=_params(("parallel", "parallel", "parallel")),
        name="compress",
    )(ch, pe, w1, w2)


def _tile_heads(a):
    return jnp.concatenate([a] * NSA_HPG, axis=1)


def _nsa_kernel(q_ref, kc_ref, vcT_ref, ks_ref, vsT_ref, kw_ref, vwT_ref, gate_ref,
                o_ref, sel_ref, qx_ref, psum_ref, gt_ref, *, seq):
    s0 = pl.program_id(2) * Q_TILE
    nsel = seq // SEL_BLOCK
    ncmp_pad = seq // CMP_STRIDE
    dh = NSA_HEAD_DIM

    q_t = q_ref[...].T
    q = jnp.concatenate([q_t[r * dh:(r + 1) * dh] for r in range(NSA_HPG)], axis=1)
    q_hi = (q * (dh ** -0.5 * LOG2_E)).astype(BF16)
    t_q = s0 + lax.broadcasted_iota(jnp.int32, (1, Q_TILE), 1)
    t_all = _tile_heads(t_q)

    def window_branch():
        span = WINDOW + Q_TILE
        lo = pl.multiple_of(jnp.maximum(s0 - WINDOW, 0), Q_TILE)
        kpos = lo + lax.broadcasted_iota(jnp.int32, (span, 1), 0)
        sw = _dot(kw_ref[0, 0, pl.ds(lo, span), :], q_hi)
        sw = sw + _tile_heads(jnp.where((kpos <= t_q) & (kpos > t_q - WINDOW), 0.0, NEG_INF))
        e_w = jnp.exp2(sw - jnp.max(sw, axis=0, keepdims=True)).astype(BF16)
        acc_w = _dot(vwT_ref[0, 0, :, pl.ds(lo, span)], e_w)
        return acc_w[0:dh] * (1.0 / acc_w[dh:dh + 1])

    def compressed_and_select(parts):
        ncmp = parts * (ncmp_pad // NSA_PARTS)
        rows = parts * (nsel // NSA_PARTS)
        o_w = window_branch()
        sc = _dot(kc_ref[0, 0, 0:ncmp, :], q_hi)
        n_end = lax.broadcasted_iota(jnp.int32, (ncmp, 1), 0) * CMP_STRIDE + (CMP_BLOCK - 1)
        sc = sc + _tile_heads(jnp.where(n_end <= t_q, 0.0, NEG_INF))
        e_c = jnp.exp2(sc - jnp.max(sc, axis=0, keepdims=True))
        l_c = jnp.sum(e_c, axis=0, keepdims=True)
        p_c = e_c * jnp.where(t_all >= CMP_BLOCK - 1, 1.0 / l_c, 0.0)
        o_c = _dot(vcT_ref[0, 0, :, 0:ncmp], p_c.astype(BF16))

        p_sum = p_c[:, 0:Q_TILE]
        for r in range(1, NSA_HPG):
            p_sum = p_sum + p_c[:, r * Q_TILE:(r + 1) * Q_TILE]
        ratio = SEL_BLOCK // CMP_STRIDE
        first_row = 8 - (CMP_BLOCK // CMP_STRIDE - 1)
        psum_ref[0:8, :] = jnp.zeros((8, Q_TILE), F32)
        psum_ref[8:8 + ncmp, :] = p_sum
        imp = psum_ref[pl.ds(first_row, rows, stride=ratio), :]
        for m in range(1, ratio + CMP_BLOCK // CMP_STRIDE - 1):
            imp = imp + psum_ref[pl.ds(first_row + m, rows, stride=ratio), :]
        j_idx = lax.broadcasted_iota(jnp.int32, (rows, Q_TILE), 0)
        cur = lax.shift_right_logical(t_q, SEL_BLOCK.bit_length() - 1)
        forced = (j_idx == 0) | (j_idx == cur) | (j_idx == cur - 1)
        score = jnp.where(j_idx <= cur, imp + jnp.where(forced, FORCE_BONUS, 0.0), -1.0)
        j_f = j_idx.astype(F32)
        sel = jnp.zeros((rows, Q_TILE), F32)
        for _ in range(min(SEL_TOP_N, rows)):
            best = jnp.max(score, axis=0, keepdims=True)
            first = jnp.min(jnp.where(score == best, j_f, float(rows)), axis=0, keepdims=True)
            pick = j_f == first
            sel = jnp.where(pick, 1.0, sel)
            score = jnp.where(pick, -3.0e38, score)
        sel_ref[0:rows, :] = sel
        if parts < NSA_PARTS:
            sel_ref[rows:, :] = jnp.zeros((nsel - rows, Q_TILE), F32)
        return o_c, o_w

    part_len = seq // NSA_PARTS
    parts_needed = lax.div(s0 + (Q_TILE + part_len - 1), part_len)
    o_c, o_w = lax.switch(parts_needed - 1,
                     [functools.partial(compressed_and_select, n) for n in range(1, NSA_PARTS + 1)])

    blocks_per_tile = SEL_KV_TILE // SEL_BLOCK
    for slot in range(SEL_GROUP):
        qx_ref[slot, 0:dh, :] = q_hi
        qx_ref[slot, dh:, :] = jnp.zeros((qx_ref.shape[1] - dh, NSA_ROWS), BF16)
    piece = SEL_KV_TILE // SEL_SPLIT
    k_off = lax.broadcasted_iota(jnp.int32, (piece, 1), 0)

    def sel_scores(j, slot, causal):
        member = sel_ref[pl.ds(pl.multiple_of(j * blocks_per_tile, blocks_per_tile), blocks_per_tile), :]
        bias = _tile_heads(jnp.where(member > 0.5, 0.0, NEG_INF))
        qx_ref[slot, dh:dh + 2 * blocks_per_tile, :] = jnp.concatenate(
            [bias, jnp.zeros_like(bias)], axis=0).astype(BF16)
        parts = []
        for u in range(SEL_SPLIT):
            base = pl.multiple_of(j * SEL_KV_TILE + u * piece, piece)
            s = _dot(ks_ref[0, 0, pl.ds(base, piece), :], qx_ref[slot])
            if causal:
                s = s + _tile_heads(jnp.where((base + k_off) <= t_q, 0.0, NEG_INF))
            parts.append((base, s))
        return parts

    def sel_update(base, s, m_i, acc):
        m_new = jnp.maximum(m_i, jnp.max(s, axis=0, keepdims=True))
        p = jnp.exp2(s - m_new).astype(BF16)
        pv = _dot(vsT_ref[0, 0, :, pl.ds(base, piece)], p)
        return m_new, jnp.exp2(m_i - m_new) * acc + pv

    def sel_group(first_tile, carry, n_tiles, causal_last):
        scores = []
        for u in range(n_tiles):
            scores += sel_scores(first_tile + u, u, causal_last and u == n_tiles - 1)
        for base, s in scores:
            carry = sel_update(base, s, *carry)
        return carry

    n_tiles = lax.div(s0 + (Q_TILE + SEL_KV_TILE - 1), SEL_KV_TILE)
    n_main = lax.div(n_tiles - 1, SEL_GROUP)
    init = (jnp.full((1, NSA_ROWS), NEG_INF, F32), jnp.zeros((vsT_ref.shape[2], NSA_ROWS), F32))
    carry = lax.fori_loop(0, n_main, lambda i, c: sel_group(SEL_GROUP * i, c, SEL_GROUP, False), init)
    _, acc_s = lax.switch(
        n_tiles - SEL_GROUP * n_main - 1,
        [functools.partial(sel_group, n_tiles=n, causal_last=True) for n in range(1, SEL_GROUP + 1)],
        SEL_GROUP * n_main, carry)
    o_s = acc_s[0:dh] * (1.0 / acc_s[dh:dh + 1])

    gt_ref[...] = gate_ref[...].T
    g_row0 = pl.program_id(1) * (NSA_HPG * 3)

    def gate(branch):
        return _sigmoid(jnp.concatenate(
            [gt_ref[pl.ds(g_row0 + 3 * r + branch, 1), :] for r in range(NSA_HPG)], axis=1))

    o = gate(0) * o_c + gate(1) * o_s + gate(2) * o_w
    o_rows = jnp.concatenate([o[:, r * Q_TILE:(r + 1) * Q_TILE] for r in range(NSA_HPG)], axis=0)
    o_ref[...] = o_rows.T


def _nsa(h, kc, vcT, ks, vsT, kw, vwT, seq, gate_tile):
    b, g = ks.shape[:2]
    nqb = seq // Q_TILE
    d = NSA_HEAD_DIM
    ncp = seq // CMP_STRIDE
    nsel = seq // SEL_BLOCK
    dk = ks.shape[-1]
    dv = vsT.shape[2]
    per_bg = lambda i, j, k: (i, j, 0, 0)
    q_rows = lambda i, j, k: (i * nqb + k, j)
    return pl.pallas_call(
        functools.partial(_nsa_kernel, seq=seq),
        grid=(b, g, nqb),
        in_specs=[pl.BlockSpec((Q_TILE, NSA_HPG * d), q_rows),
                  pl.BlockSpec((1, 1, ncp, d), per_bg),
                  pl.BlockSpec((1, 1, d, ncp), per_bg),
                  pl.BlockSpec((1, 1, seq, dk), per_bg),
                  pl.BlockSpec((1, 1, dv, seq), per_bg),
                  pl.BlockSpec((1, 1, seq, d), per_bg),
                  pl.BlockSpec((1, 1, dv, seq), per_bg),
                  pl.BlockSpec((Q_TILE, LANES), lambda i, j, k: (i * nqb + k, gate_tile))],
        out_specs=pl.BlockSpec((Q_TILE, NSA_HPG * d), q_rows),
        out_shape=jax.ShapeDtypeStruct((b * seq, NSA_Q_COLS), F32),
        scratch_shapes=[pltpu.VMEM((nsel, Q_TILE), F32), pltpu.VMEM((SEL_GROUP, dk, NSA_ROWS), BF16),
                        pltpu.VMEM((ncp + 8, Q_TILE), F32), pltpu.VMEM((LANES, Q_TILE), F32)],
        compiler_params=_params(("parallel", "parallel", "arbitrary")),
        name="nsa",
    )(h, kc, vcT, ks, vsT, kw, vwT, h)


def _hgrn_chunk(q, f, v, lb, state_t, shift_ref, lower, gstart, dmat, off_mask, factored):
    c = HG_CHUNK
    log_f = jnp.log(lb + (1.0 - lb) * _sigmoid(f))
    kk = (1.0 - lb) * _sigmoid(-f)
    l1, l2, l3 = _split3(log_f)
    bt = _dot(lower, l1) + _dot(lower, l2) + _dot(lower, l3)
    bs = _dot(gstart, l1) + _dot(gstart, l2) + _dot(gstart, l3)

    qh = (q * jnp.exp(bt - bs)).astype(BF16)

    def same_block_direct():
        shift_ref[0, HG_SUB:, :] = kk
        shift_ref[1, HG_SUB:, :] = bt
        ones = jnp.ones((kk.shape[1], c), BF16)
        acc = jnp.zeros((c, c), F32)
        for d in range(HG_SUB):
            if d == 0:
                prod = q * kk
            else:
                rows = pl.ds(HG_SUB - d, c)
                prod = (q * shift_ref[0, rows, :]) * jnp.exp(bt - shift_ref[1, rows, :])
            band = _dot(prod.astype(BF16), ones)
            acc = jnp.where(dmat == d, band, acc)
        return acc

    def same_block_factored():
        return _dot_nt(qh, (kk * jnp.exp(bs - bt)).astype(BF16))

    att = same_block_factored() if factored else same_block_direct()
    att = jnp.where(dmat >= 0, att, 0.0)

    blocks = [jnp.zeros((HG_SUB, c), F32)]
    for i in range(1, c // HG_SUB):
        b_i = bs[i * HG_SUB:i * HG_SUB + 1, :]
        kh = kk * jnp.exp(jnp.minimum(b_i - bt, 0.0))
        blocks.append(_dot_nt(qh[i * HG_SUB:(i + 1) * HG_SUB], kh.astype(BF16)))
    att = jnp.where(off_mask, jnp.concatenate(blocks, axis=0), att)

    vb = v.astype(BF16)
    o = _dot_nt((q * jnp.exp(bt)).astype(BF16), state_t.astype(BF16)) + _dot(att.astype(BF16), vb)
    b_last = bt[c - 1:c, :]
    k_dec = kk * jnp.exp(b_last - bt)
    return o, state_t * jnp.exp(b_last) + _dot(v.T.astype(BF16), k_dec.astype(BF16))


def _hgrn_kernel(q_ref, f_ref, v_ref, gate_ref, lb_ref, ng_ref, o_ref, state_ref, shift_ref):
    c = HG_CHUNK

    @pl.when(pl.program_id(2) == 0)
    def _():
        state_ref[...] = jnp.zeros_like(state_ref)
        shift_ref[...] = jnp.zeros_like(shift_ref)

    row = lax.broadcasted_iota(jnp.int32, (c, c), 0)
    col = lax.broadcasted_iota(jnp.int32, (c, c), 1)
    sub = HG_SUB.bit_length() - 1
    row_blk = lax.shift_right_logical(row, sub)
    col_blk = lax.shift_right_logical(col, sub)
    lower = jnp.where(col <= row, 1.0, 0.0).astype(BF16)
    gstart = jnp.where(col_blk < row_blk, 1.0, 0.0).astype(BF16)
    dmat = jnp.where(row_blk == col_blk, row - col, -1)
    off_mask = col_blk < row_blk
    ng = ng_ref[...]

    def step(i, states, factored):
        rows = pl.ds(pl.multiple_of(i * c, c), c)
        new_states = []
        for hd in range(HG_HEADS_PER_STEP):
            lanes = slice(hd * HG_KEY_DIM, (hd + 1) * HG_KEY_DIM)
            o, state = _hgrn_chunk(q_ref[rows, lanes], f_ref[rows, lanes], v_ref[rows, lanes], lb_ref[hd],
                                   states[hd], shift_ref.at[hd], lower, gstart, dmat, off_mask, factored)
            gate = gate_ref[rows, lanes]
            ms = jnp.mean(o * o, axis=-1, keepdims=True)
            o_ref[rows, lanes] = o * lax.rsqrt(ms + RMS_EPS) * ng * (gate * _sigmoid(gate))
            new_states.append(state)
        return tuple(new_states)

    def run(factored):
        init = tuple(state_ref[hd] for hd in range(HG_HEADS_PER_STEP))
        states = lax.fori_loop(0, HG_ROWS // c, functools.partial(step, factored=factored), init)
        for hd, state in enumerate(states):
            state_ref[hd] = state

    can_factor = jnp.min(lb_ref[...]) > math.exp(-HG_FACTOR_LIMIT / HG_SUB)
    lax.cond(can_factor, functools.partial(run, True), functools.partial(run, False))


def _hgrn(h, lb, ng, bsz, seq, col0):
    nblk = seq // HG_ROWS
    hps = HG_HEADS_PER_STEP
    width = hps * HG_KEY_DIM
    tiles = HG_COLS // width
    assert (col0 * LANES) % width == 0 and seq % HG_ROWS == 0
    blk = lambda k: pl.BlockSpec((HG_ROWS, width),
                                 lambda i, j, c: (i * nblk + c, (col0 * LANES) // width + k * tiles + j))
    return pl.pallas_call(
        _hgrn_kernel,
        grid=(bsz, HG_HEADS // hps, nblk),
        in_specs=[blk(0), blk(1), blk(2), blk(3),
                  pl.BlockSpec((hps, 1, HG_KEY_DIM), lambda i, j, c: (j, 0, 0)),
                  pl.BlockSpec((1, HG_VAL_DIM), lambda i, j, c: (0, 0))],
        out_specs=pl.BlockSpec((HG_ROWS, width), lambda i, j, c: (i * nblk + c, j)),
        out_shape=jax.ShapeDtypeStruct((bsz * seq, HG_COLS), F32),
        scratch_shapes=[pltpu.VMEM((hps, HG_VAL_DIM, HG_KEY_DIM), F32),
                        pltpu.VMEM((hps, 2, HG_SUB + HG_CHUNK, HG_KEY_DIM), F32)],
        compiler_params=_params(("parallel", "parallel", "arbitrary")),
        name="hgrn",
    )(h, h, h, h, lb, ng)


def _mix_kernel(nsa_ref, hg_ref, x_ref, w_ref, ng_ref, g_ref, b_ref, o_ref):
    o_n = nsa_ref[...]
    half = o_n.shape[1]
    o_n = o_n * lax.rsqrt(jnp.mean(o_n * o_n, axis=-1, keepdims=True) + RMS_EPS) * ng_ref[...]
    mix = _dot(o_n.astype(BF16), w_ref[:half]) + _dot(hg_ref[...].astype(BF16), w_ref[half:])
    o_ref[...] = _layer_norm(DEEPNORM_ALPHA * x_ref[...] + mix, g_ref[...], b_ref[...])


def _mix(o_nsa, o_hg, x, w_out, ng, g, b, tm):
    t, d = x.shape
    half = o_nsa.shape[1]
    rows = lambda w: pl.BlockSpec((tm, w), lambda i: (i, 0))
    full = lambda r, c: pl.BlockSpec((r, c), lambda i: (0, 0))
    return pl.pallas_call(
        _mix_kernel,
        grid=(t // tm,),
        in_specs=[rows(half), rows(half), rows(d), full(d, d), full(1, half), full(1, d), full(1, d)],
        out_specs=rows(d),
        out_shape=jax.ShapeDtypeStruct((t, d), F32),
        compiler_params=_params(("parallel",)),
        name="mix",
    )(o_nsa, o_hg, x, w_out, ng, g, b)


def _xattn_kernel(x_ref, wq_ref, kT_ref, v_ref, wo_ref, g_ref, b_ref, o_ref):
    x = x_ref[0]
    d = x.shape[1]
    dh = d // XA_HEADS
    q = _dot(x.astype(BF16), wq_ref[...])
    heads = []
    for h in range(XA_HEADS):
        cols = slice(h * dh, (h + 1) * dh)
        s = _dot(q[:, cols].astype(BF16), kT_ref[0, cols, :]) * (dh ** -0.5)
        e = jnp.exp(s - jnp.max(s, axis=-1, keepdims=True))
        p = e * (1.0 / jnp.sum(e, axis=-1, keepdims=True))
        heads.append(_dot(p.astype(BF16), v_ref[0, :, cols]))
    o = jnp.concatenate(heads, axis=1)
    xa = _dot(o.astype(BF16), wo_ref[...])
    o_ref[0] = _layer_norm(DEEPNORM_ALPHA * x + xa, g_ref[...], b_ref[...])


def _xattn(x, wq, kT, v, wo, g, b, tm):
    bsz, seq, d = x.shape
    m = v.shape[1]
    full = lambda r, c: pl.BlockSpec((r, c), lambda i, j: (0, 0))
    return pl.pallas_call(
        _xattn_kernel,
        grid=(bsz, seq // tm),
        in_specs=[pl.BlockSpec((1, tm, d), lambda i, j: (i, j, 0)),
                  full(d, d),
                  pl.BlockSpec((1, d, m), lambda i, j: (i, 0, 0)),
                  pl.BlockSpec((1, m, d), lambda i, j: (i, 0, 0)),
                  full(d, d), full(1, d), full(1, d)],
        out_specs=pl.BlockSpec((1, tm, d), lambda i, j: (i, j, 0)),
        out_shape=jax.ShapeDtypeStruct((bsz, seq, d), F32),
        compiler_params=_params(("parallel", "parallel")),
        name="xattn",
    )(x, wq, kT, v, wo, g, b)


def _route_kernel(x_ref, wh_ref, wl_ref, bias_ref, ids_ref, wts_ref, rnk_ref, counts_ref, cnt_ref):
    xh, xl = _split2(x_ref[...])
    logits = _dot(xh, wh_ref[...]) + _dot(xh, wl_ref[...]) + _dot(xl, wh_ref[...]) + bias_ref[...]
    lane = lax.broadcasted_iota(jnp.int32, logits.shape, 1)
    lane_f = lane.astype(F32)
    big = float(LANES)

    is_g = lane < MOE_GROUPS
    g_max = jnp.max(jnp.where(is_g, logits, NEG_INF), axis=-1, keepdims=True)
    g_sum = jnp.sum(jnp.where(is_g, jnp.exp(logits - g_max), 0.0), axis=-1, keepdims=True)
    g_w = 1.0 / g_sum
    g_idx = jnp.min(jnp.where(is_g & (logits == g_max), lane_f, big), axis=-1, keepdims=True)

    e_lo = MOE_GROUPS + MOE_EXPERTS_PER_GROUP * g_idx
    is_e = (lane_f >= e_lo) & (lane_f < e_lo + MOE_EXPERTS_PER_GROUP)
    e_log = jnp.where(is_e, logits, NEG_INF)
    e_max = jnp.max(e_log, axis=-1, keepdims=True)
    e_exp = jnp.where(is_e, jnp.exp(logits - e_max), 0.0)
    e_sum = jnp.sum(e_exp, axis=-1, keepdims=True)
    i1 = jnp.min(jnp.where(is_e & (e_log == e_max), lane_f, big), axis=-1, keepdims=True)
    rest = jnp.where(lane_f == i1, NEG_INF, e_log)
    r_max = jnp.max(rest, axis=-1, keepdims=True)
    i2 = jnp.min(jnp.where(is_e & (lane_f != i1) & (rest == r_max), lane_f, big), axis=-1, keepdims=True)
    p1 = 1.0 / e_sum
    p2 = jnp.exp(r_max - e_max) / e_sum
    tot = p1 + p2
    w1 = g_w * (p1 / tot)
    w2 = g_w * (p2 / tot)
    ids = jnp.where(lane == 0, i1, i2) - float(MOE_GROUPS)
    ids_ref[...] = ids.astype(jnp.int32)
    wts_ref[...] = jnp.where(lane == 0, w1, w2)

    @pl.when(pl.program_id(0) == 0)
    def _():
        cnt_ref[...] = jnp.zeros_like(cnt_ref)

    tm = logits.shape[0]
    hit1 = lane_f == i1
    hit2 = lane_f == i2
    hits = jnp.where(hit1 | hit2, 1.0, 0.0)
    earlier = lax.broadcasted_iota(jnp.int32, (tm, tm), 1) < lax.broadcasted_iota(jnp.int32, (tm, tm), 0)
    before = _dot(jnp.where(earlier, 1.0, 0.0).astype(BF16), hits.astype(BF16)) + cnt_ref[...]
    rank1 = jnp.sum(jnp.where(hit1, before, 0.0), axis=-1, keepdims=True)
    rank2 = jnp.sum(jnp.where(hit2, before, 0.0), axis=-1, keepdims=True)
    rnk_ref[...] = jnp.where(lane == 0, rank1, rank2).astype(jnp.int32)
    cnt_ref[...] = cnt_ref[...] + jnp.sum(hits, axis=0, keepdims=True)
    counts_ref[...] = cnt_ref[...].astype(jnp.int32)


def _route(x, wh, wl, bias, tm):
    t, d = x.shape
    rows = lambda w: pl.BlockSpec((tm, w), lambda i: (i, 0))
    full = lambda r, c: pl.BlockSpec((r, c), lambda i: (0, 0))
    i32 = jnp.int32
    return pl.pallas_call(
        _route_kernel,
        grid=(t // tm,),
        in_specs=[rows(d), full(d, LANES), full(d, LANES), full(1, LANES)],
        out_specs=[rows(LANES), rows(LANES), rows(LANES), full(1, LANES)],
        out_shape=[jax.ShapeDtypeStruct((t, LANES), i32), jax.ShapeDtypeStruct((t, LANES), F32),
                   jax.ShapeDtypeStruct((t, LANES), i32), jax.ShapeDtypeStruct((1, LANES), i32)],
        scratch_shapes=[pltpu.VMEM((1, LANES), F32)],
        compiler_params=_params(("arbitrary",)),
        name="route",
    )(x, wh, wl, bias)


def _row_copy_wait(src, dst, sem, rows):
    pltpu.make_async_copy(src.at[pl.ds(0, rows)], dst.at[pl.ds(0, rows)], sem).wait()


def _dispatch_kernel(dest_ref, zfill_ref, x_ref, xs_hbm, zbuf, zsem, sem):
    i = pl.program_id(0)
    tm = x_ref.shape[0]

    @pl.when(i == 0)
    def _():
        zbuf[...] = jnp.zeros_like(zbuf)
        tail = lambda e: xs_hbm.at[pl.ds(pl.multiple_of(zfill_ref[e], MOE_BLOCK), MOE_BLOCK)]
        for e in range(zfill_ref.shape[0]):
            @pl.when(zfill_ref[e] >= 0)
            def _():
                pltpu.make_async_copy(zbuf, tail(e), zsem).start()
        for e in range(zfill_ref.shape[0]):
            @pl.when(zfill_ref[e] >= 0)
            def _():
                pltpu.make_async_copy(zbuf, tail(e), zsem).wait()

    def send(j, c):
        for k in range(MOE_TOP_K):
            slot = dest_ref[(i * tm + j) * MOE_TOP_K + k]
            pltpu.make_async_copy(x_ref.at[pl.ds(j, 1)], xs_hbm.at[pl.ds(slot, 1)], sem).start()
        return c

    lax.fori_loop(0, tm, send, 0, unroll=8)
    for _ in range(MOE_TOP_K):
        _row_copy_wait(x_ref, xs_hbm, sem, tm)


def _dispatch(dest, zfill, x, cap, tm):
    t, d = x.shape
    grid_spec = pltpu.PrefetchScalarGridSpec(
        num_scalar_prefetch=2,
        grid=(t // tm,),
        in_specs=[pl.BlockSpec((tm, d), lambda i, dest, zfill: (i, 0))],
        out_specs=pl.BlockSpec(memory_space=pl.ANY),
        scratch_shapes=[pltpu.VMEM((MOE_BLOCK, d), F32), pltpu.SemaphoreType.DMA, pltpu.SemaphoreType.DMA])
    return pl.pallas_call(
        _dispatch_kernel,
        grid_spec=grid_spec,
        out_shape=jax.ShapeDtypeStruct((cap, d), F32),
        compiler_params=_params(("arbitrary",)),
        name="dispatch",
    )(dest, zfill, x)


def _experts_kernel(bexp_ref, nact_ref, xs_ref, wg_ref, wu_ref, wd_ref, y_ref):
    @pl.when(pl.program_id(0) < nact_ref[0])
    def _():
        xb = xs_ref[...].astype(BF16)
        gate = _dot(xb, wg_ref[0])
        hid = (gate * _sigmoid(gate)) * _dot(xb, wu_ref[0])
        y_ref[...] = _dot(hid.astype(BF16), wd_ref[0])

    @pl.when(pl.program_id(0) >= nact_ref[0])
    def _():
        y_ref[...] = jnp.zeros_like(y_ref)


def _experts(bexp, nact, xs, wg, wu, wd):
    cap, d = xs.shape
    dff = wg.shape[-1]
    by_expert = lambda i, bexp, nact: (bexp[i], 0, 0)
    active = lambda i, bexp, nact: (jnp.minimum(i, nact[0] - 1), 0)
    grid_spec = pltpu.PrefetchScalarGridSpec(
        num_scalar_prefetch=2,
        grid=(cap // MOE_BLOCK,),
        in_specs=[pl.BlockSpec((MOE_BLOCK, d), active),
                  pl.BlockSpec((1, d, dff), by_expert),
                  pl.BlockSpec((1, d, dff), by_expert),
                  pl.BlockSpec((1, dff, d), by_expert)],
        out_specs=pl.BlockSpec((MOE_BLOCK, d), lambda i, bexp, nact: (i, 0)))
    return pl.pallas_call(
        _experts_kernel,
        grid_spec=grid_spec,
        out_shape=jax.ShapeDtypeStruct((cap, d), F32),
        compiler_params=_params(("arbitrary",)),
        name="experts",
    )(bexp, nact, xs, wg, wu, wd)


def _combine_kernel(dest_ref, y_hbm, x_ref, wts_ref, g_ref, b_ref, o_ref, ybuf, sem):
    i = pl.program_id(0)
    n = pl.num_programs(0)
    tm = x_ref.shape[0]
    cur = lax.rem(i, 2)

    def fetch(tile, buf):
        def one(j, c):
            for k in range(MOE_TOP_K):
                slot = dest_ref[(tile * tm + j) * MOE_TOP_K + k]
                pltpu.make_async_copy(y_hbm.at[pl.ds(slot, 1)], ybuf.at[buf, k, pl.ds(j, 1)], sem.at[buf]).start()
            return c
        lax.fori_loop(0, tm, one, 0, unroll=8)

    @pl.when(i == 0)
    def _():
        fetch(0, 0)

    @pl.when(i + 1 < n)
    def _():
        fetch(i + 1, 1 - cur)

    for k in range(MOE_TOP_K):
        _row_copy_wait(y_hbm, ybuf.at[cur, k], sem.at[cur], tm)
    w = wts_ref[...]
    ff = w[:, 0:1] * ybuf[cur, 0]
    for k in range(1, MOE_TOP_K):
        ff = ff + w[:, k:k + 1] * ybuf[cur, k]
    o_ref[...] = _layer_norm(DEEPNORM_ALPHA * x_ref[...] + ff, g_ref[...], b_ref[...])


def _combine(dest, y, x, wts, g, b, tm):
    t, d = x.shape
    rows = lambda w: pl.BlockSpec((tm, w), lambda i, dest: (i, 0))
    full = lambda r, c: pl.BlockSpec((r, c), lambda i, dest: (0, 0))
    grid_spec = pltpu.PrefetchScalarGridSpec(
        num_scalar_prefetch=1,
        grid=(t // tm,),
        in_specs=[pl.BlockSpec(memory_space=pl.ANY), rows(d), rows(LANES), full(1, d), full(1, d)],
        out_specs=rows(d),
        scratch_shapes=[pltpu.VMEM((2, MOE_TOP_K, tm, d), F32), pltpu.SemaphoreType.DMA((2,))])
    return pl.pallas_call(
        _combine_kernel,
        grid_spec=grid_spec,
        out_shape=jax.ShapeDtypeStruct((t, d), F32),
        compiler_params=_params(("arbitrary",)),
        name="combine",
    )(dest, y, x, wts, g, b)


def _dispatch_plan(ids, rnk, counts, n_tok):
    padded = ((counts + MOE_BLOCK - 1) // MOE_BLOCK) * MOE_BLOCK
    pend = jnp.cumsum(padded)
    pstart = pend - padded
    dest = (pstart[ids] + rnk).reshape(-1).astype(jnp.int32)
    cap = n_tok * MOE_TOP_K + MOE_N_EXPERTS * MOE_BLOCK
    n_blocks = cap // MOE_BLOCK
    block_start = jnp.arange(n_blocks, dtype=jnp.int32) * MOE_BLOCK
    bexp = jnp.minimum(jnp.sum(block_start[:, None] >= pend[None, :], axis=1), MOE_N_EXPERTS - 1).astype(jnp.int32)
    nact = (pend[-1] // MOE_BLOCK).astype(jnp.int32).reshape(1)
    tails = jnp.where(padded > 0, pend - MOE_BLOCK, -1)
    idle = pend[-1] + block_start[:MOE_N_EXPERTS]
    zfill = jnp.concatenate([tails, jnp.where(idle < cap, idle, -1)]).astype(jnp.int32)
    return dest, bexp, nact, zfill, cap


def kernel(x, mem, w_in, cmp_pe_k, cmp_pe_v, cmp_w1_k, cmp_w2_k, cmp_w1_v, cmp_w2_v, nsa_norm_g,
           hg_lb_logits, hg_norm_g, w_out, ln1_g, ln1_b, xa_wq, xa_wk, xa_wv, xa_wo, ln2_g, ln2_b,
           moe_w_group, moe_b_group, moe_w_expert, moe_b_expert, moe_w_gate, moe_w_up, moe_w_down,
           ln3_g, ln3_b):
    b, s, d = x.shape
    t = b * s
    g, dh = NSA_KV_GROUPS, NSA_HEAD_DIM
    nch = s // CMP_STRIDE
    row = lambda a: a.reshape(1, -1)
    lb_all = jnp.cumsum(jax.nn.softmax(hg_lb_logits.astype(F32), axis=0), axis=0)
    xt = x.reshape(t, d)
    for l in range(DEPTH):
        nsa_cols = NSA_Q_COLS + 6 * NSA_KV_COLS
        w_l = w_in[l]
        w_perm = jnp.concatenate(
            [w_l[:, :nsa_cols], w_l[:, nsa_cols + NSA_GATE_COLS:], w_l[:, nsa_cols:nsa_cols + NSA_GATE_COLS],
             jnp.zeros((d, LANES - NSA_GATE_COLS), w_l.dtype)], axis=1)
        h, ks, kw, vsT, vwT = _inproj(xt, w_perm.astype(BF16), b, s, 256)
        nkc = h[:, NSA_Q_COLS:NSA_Q_COLS + NSA_KV_COLS]
        nvc = h[:, NSA_Q_COLS + NSA_KV_COLS:NSA_Q_COLS + 2 * NSA_KV_COLS]
        hg_col0 = nsa_cols // LANES
        gate_tile = (nsa_cols + 4 * HG_COLS) // LANES

        chunks = lambda a: jnp.transpose(a.reshape(b, nch, CMP_STRIDE, g, dh), (0, 3, 1, 2, 4)).reshape(
            b, g, nch, CMP_STRIDE * dh)
        comp = _compress(jnp.stack([chunks(nkc), chunks(nvc)]),
                         jnp.stack([cmp_pe_k[l].reshape(1, -1), cmp_pe_v[l].reshape(1, -1)]),
                         jnp.stack([cmp_w1_k[l], cmp_w1_v[l]]),
                         jnp.stack([cmp_w2_k[l], cmp_w2_v[l]]))
        kc = comp[0].astype(BF16)
        vcT = jnp.swapaxes(comp[1], -1, -2).astype(BF16)

        o_nsa = _nsa(h, kc, vcT, ks, vsT, kw, vwT, s, gate_tile)

        o_hg = _hgrn(h, lb_all[l].reshape(HG_HEADS, 1, HG_KEY_DIM), row(hg_norm_g[l]), b, s, hg_col0)

        x1 = _mix(o_nsa, o_hg, xt, w_out[l].astype(BF16), row(nsa_norm_g[l]), row(ln1_g[l]), row(ln1_b[l]), 512)

        n_mem = mem.shape[1]
        kv = _matmul(mem.reshape(b * n_mem, d),
                     jnp.concatenate([xa_wk[l], xa_wv[l]], axis=1).astype(BF16), n_mem)
        kT = jnp.swapaxes(kv[:, :d].reshape(b, n_mem, d), 1, 2).astype(BF16)
        v = kv[:, d:].reshape(b, n_mem, d).astype(BF16)
        x2 = _xattn(x1.reshape(b, s, d), xa_wq[l].astype(BF16), kT, v, xa_wo[l].astype(BF16),
                    row(ln2_g[l]), row(ln2_b[l]), 512).reshape(t, d)

        w_r = jnp.concatenate([moe_w_group[l], moe_w_expert[l]], axis=1)
        w_r = jnp.pad(w_r, ((0, 0), (0, LANES - w_r.shape[1])))
        b_r = jnp.pad(jnp.concatenate([moe_b_group[l], moe_b_expert[l]]), (0, LANES - MOE_GROUPS - MOE_N_EXPERTS))
        w_rh, w_rl = _split2(w_r)
        ids, wts, rnk, counts = _route(x2, w_rh, w_rl, row(b_r), 512)
        dest, bexp, nact, zfill, cap = _dispatch_plan(
            ids[:, :MOE_TOP_K], rnk[:, :MOE_TOP_K], counts[0, MOE_GROUPS:MOE_GROUPS + MOE_N_EXPERTS], t)
        xs = _dispatch(dest, zfill, x2, cap, 256)
        ys = _experts(bexp, nact, xs, moe_w_gate[l].astype(BF16), moe_w_up[l].astype(BF16),
                      moe_w_down[l].astype(BF16))
        xt = _combine(dest, ys, x2, wts, row(ln3_g[l]), row(ln3_b[l]), 256)
    return xt.reshape(b, s, d)
```

```python
import functools
import math

import jax
import jax.numpy as jnp
from jax import lax
from jax.experimental import pallas as pl
from jax.experimental.pallas import tpu as pltpu

F32 = jnp.float32
BF16 = jnp.bfloat16

NSA_HEAD_DIM = 64
NSA_HEADS = 8
NSA_KV_GROUPS = 2
NSA_HPG = NSA_HEADS // NSA_KV_GROUPS
CMP_BLOCK = 32
CMP_STRIDE = 16
CMP_HIDDEN = 256
SEL_BLOCK = 64
SEL_TOP_N = 16
WINDOW = 512
FORCE_BONUS = 1.0e4
HG_KEY_DIM = 128
HG_VAL_DIM = 128
HG_HEADS = 4
XA_HEADS = 4
MOE_GROUPS = 4
MOE_EXPERTS_PER_GROUP = 8
MOE_N_EXPERTS = MOE_GROUPS * MOE_EXPERTS_PER_GROUP
MOE_TOP_K = 2
DEPTH = 1
DEEPNORM_ALPHA = (2.0 * DEPTH) ** 0.25
LN_EPS = 1e-5
RMS_EPS = 1e-6
NEG_INF = -1e30

NSA_Q_COLS = NSA_HEADS * NSA_HEAD_DIM
NSA_KV_COLS = NSA_KV_GROUPS * NSA_HEAD_DIM
NSA_GATE_COLS = NSA_HEADS * 3
HG_COLS = HG_HEADS * HG_KEY_DIM
NSA_KV0 = NSA_Q_COLS + 4 * HG_COLS

LANES = 128
Q_TILE = 128
NSA_ROWS = NSA_HPG * Q_TILE
SEL_KV_TILE = 512
SEL_GROUP = 6
SEL_SPLIT = 2
NSA_PARTS = 4
NSA_V_ROWS = NSA_HEAD_DIM + 16
LOG2_E = 1.4426950408889634
HG_CHUNK = 128
HG_SUB = 16
HG_ROWS = 512
HG_HEADS_PER_STEP = 4
HG_FACTOR_LIMIT = 60.0
MOE_BLOCK = 256
VMEM_LIMIT = 48 * 1024 * 1024


def _dot(a, b):
    return jnp.dot(a, b, preferred_element_type=F32)


def _dot_nt(a, b):
    return lax.dot_general(a, b, (((1,), (1,)), ((), ())), preferred_element_type=F32)


def _split2(a):
    hi = a.astype(BF16)
    lo = (a - hi.astype(F32)).astype(BF16)
    return hi, lo


def _split3(a):
    p1 = a.astype(BF16)
    r1 = a - p1.astype(F32)
    p2 = r1.astype(BF16)
    p3 = (r1 - p2.astype(F32)).astype(BF16)
    return p1, p2, p3


def _dot3(a, b):
    ah, al = _split2(a)
    bh, bl = _split2(b)
    return _dot(ah, bh) + _dot(ah, bl) + _dot(al, bh)


def _sigmoid(x):
    return 1.0 / (1.0 + jnp.exp(-x))


def _layer_norm(y, g, b):
    mu = jnp.mean(y, axis=-1, keepdims=True)
    d = y - mu
    var = jnp.mean(d * d, axis=-1, keepdims=True)
    return d * lax.rsqrt(var + LN_EPS) * g + b


def _params(sem):
    return pltpu.CompilerParams(dimension_semantics=sem, vmem_limit_bytes=VMEM_LIMIT)


def _matmul_kernel(x_ref, w_ref, o_ref):
    o_ref[...] = _dot(x_ref[...].astype(BF16), w_ref[...]).astype(o_ref.dtype)


def _matmul(x, w, tm):
    m, k = x.shape
    n = w.shape[1]
    return pl.pallas_call(
        _matmul_kernel,
        grid=(m // tm,),
        in_specs=[pl.BlockSpec((tm, k), lambda i: (i, 0)),
                  pl.BlockSpec((k, n), lambda i: (0, 0))],
        out_specs=pl.BlockSpec((tm, n), lambda i: (i, 0)),
        out_shape=jax.ShapeDtypeStruct((m, n), F32),
        compiler_params=_params(("parallel",)),
        name="proj",
    )(x, w)


def _inproj_kernel(x_ref, w_ref, h_ref, ks_ref, kw_ref, vsT_ref, vwT_ref, *, tiles_per_seq):
    h = _dot(x_ref[...].astype(BF16), w_ref[...])
    h_ref[...] = h
    tm = h.shape[0]
    dh = NSA_HEAD_DIM
    k_sel0 = NSA_KV0 + 2 * NSA_KV_COLS
    v_sel0, k_win0, v_win0 = k_sel0 + NSA_KV_COLS, k_sel0 + 2 * NSA_KV_COLS, k_sel0 + 3 * NSA_KV_COLS
    pos = lax.rem(pl.program_id(0), tiles_per_seq) * tm + lax.broadcasted_iota(jnp.int32, (tm, 1), 0)
    blk = lax.shift_right_logical(pos & (SEL_KV_TILE - 1), SEL_BLOCK.bit_length() - 1)
    onehot = jnp.where(blk == lax.broadcasted_iota(jnp.int32, (1, LANES - dh), 1), 1.0, 0.0)
    ones_row = jnp.where(lax.broadcasted_iota(jnp.int32, (NSA_V_ROWS - dh, tm), 0) == 0, 1.0, 0.0)
    vsT = h[:, v_sel0:v_sel0 + NSA_KV_COLS].T
    vwT = h[:, v_win0:v_win0 + NSA_KV_COLS].T
    for g in range(NSA_KV_GROUPS):
        cols = slice(g * dh, (g + 1) * dh)
        ks_ref[0, g] = jnp.concatenate([h[:, k_sel0 + g * dh:k_sel0 + (g + 1) * dh], onehot], axis=1).astype(BF16)
        kw_ref[0, g] = h[:, k_win0 + g * dh:k_win0 + (g + 1) * dh].astype(BF16)
        vsT_ref[0, g] = jnp.concatenate([vsT[cols], ones_row], axis=0).astype(BF16)
        vwT_ref[0, g] = jnp.concatenate([vwT[cols], ones_row], axis=0).astype(BF16)


def _inproj(x, w, bsz, seq, tm):
    t, k = x.shape
    n = w.shape[1]
    g, dh = NSA_KV_GROUPS, NSA_HEAD_DIM
    tps = seq // tm
    keys = lambda width: pl.BlockSpec((1, g, tm, width), lambda i: (i // tps, 0, i % tps, 0))
    vals = pl.BlockSpec((1, g, NSA_V_ROWS, tm), lambda i: (i // tps, 0, 0, i % tps))
    return pl.pallas_call(
        functools.partial(_inproj_kernel, tiles_per_seq=tps),
        grid=(t // tm,),
        in_specs=[pl.BlockSpec((tm, k), lambda i: (i, 0)),
                  pl.BlockSpec((k, n), lambda i: (0, 0))],
        out_specs=[pl.BlockSpec((tm, n), lambda i: (i, 0)), keys(LANES), keys(dh), vals, vals],
        out_shape=[jax.ShapeDtypeStruct((t, n), F32),
                   jax.ShapeDtypeStruct((bsz, g, seq, LANES), BF16),
                   jax.ShapeDtypeStruct((bsz, g, seq, dh), BF16),
                   jax.ShapeDtypeStruct((bsz, g, NSA_V_ROWS, seq), BF16),
                   jax.ShapeDtypeStruct((bsz, g, NSA_V_ROWS, seq), BF16)],
        compiler_params=_params(("parallel",)),
        name="inproj",
    )(x, w)


def _compress_kernel(ch_ref, pe_ref, w1_ref, w2_ref, o_ref):
    ch = ch_ref[0, 0, 0]
    half = ch.shape[1]
    nch = ch.shape[0]
    pe = pe_ref[0]
    w1 = w1_ref[0]
    top = _dot3(ch + pe[:, :half], w1[:half])
    bot = _dot3(ch + pe[:, half:], w1[half:])
    hid = top + pltpu.roll(bot, nch - 1, 0)
    c = 0.7978845608028654
    act = 0.5 * hid * (1.0 + jnp.tanh(c * (hid + 0.044715 * hid * hid * hid)))
    out = _dot3(act, w2_ref[0])
    row = lax.broadcasted_iota(jnp.int32, out.shape, 0)
    o_ref[0, 0, 0] = jnp.where(row < nch - 1, out, 0.0)


def _compress(ch, pe, w1, w2):
    _, b, g, nch, width = ch.shape
    hidden = w1.shape[-1]
    d = w2.shape[-1]
    return pl.pallas_call(
        _compress_kernel,
        grid=(2, b, g),
        in_specs=[pl.BlockSpec((1, 1, 1, nch, width), lambda a, i, j: (a, i, j, 0, 0)),
                  pl.BlockSpec((1, 1, 2 * width), lambda a, i, j: (a, 0, 0)),
                  pl.BlockSpec((1, 2 * width, hidden), lambda a, i, j: (a, 0, 0)),
                  pl.BlockSpec((1, hidden, d), lambda a, i, j: (a, 0, 0))],
        out_specs=pl.BlockSpec((1, 1, 1, nch, d), lambda a, i, j: (a, i, j, 0, 0)),
        out_shape=jax.ShapeDtypeStruct((2, b, g, nch, d), F32),
        compiler_params=_params(("parallel", "parallel", "parallel")),
        name="compress",
    )(ch, pe, w1, w2)


def _tile_heads(a):
    return jnp.concatenate([a] * NSA_HPG, axis=1)


def _nsa_kernel(q_ref, kc_ref, vcT_ref, ks_ref, vsT_ref, kw_ref, vwT_ref, gate_ref,
                o_ref, sel_ref, qx_ref, psum_ref, gt_ref, *, seq):
    s0 = pl.program_id(2) * Q_TILE
    nsel = seq // SEL_BLOCK
    ncmp_pad = seq // CMP_STRIDE
    dh = NSA_HEAD_DIM

    q_t = q_ref[...].T
    q = jnp.concatenate([q_t[r * dh:(r + 1) * dh] for r in range(NSA_HPG)], axis=1)
    q_hi = (q * (dh ** -0.5 * LOG2_E)).astype(BF16)
    t_q = s0 + lax.broadcasted_iota(jnp.int32, (1, Q_TILE), 1)
    t_all = _tile_heads(t_q)

    def window_branch():
        span = WINDOW + Q_TILE
        lo = pl.multiple_of(jnp.maximum(s0 - WINDOW, 0), Q_TILE)
        kpos = lo + lax.broadcasted_iota(jnp.int32, (span, 1), 0)
        sw = _dot(kw_ref[0, 0, pl.ds(lo, span), :], q_hi)
        sw = sw + _tile_heads(jnp.where((kpos <= t_q) & (kpos > t_q - WINDOW), 0.0, NEG_INF))
        e_w = jnp.exp2(sw - jnp.max(sw, axis=0, keepdims=True)).astype(BF16)
        acc_w = _dot(vwT_ref[0, 0, :, pl.ds(lo, span)], e_w)
        return acc_w[0:dh] * (1.0 / acc_w[dh:dh + 1])

    def compressed_and_select(parts):
        ncmp = parts * (ncmp_pad // NSA_PARTS)
        rows = parts * (nsel // NSA_PARTS)
        o_w = window_branch()
        sc = _dot(kc_ref[0, 0, 0:ncmp, :], q_hi)
        n_end = lax.broadcasted_iota(jnp.int32, (ncmp, 1), 0) * CMP_STRIDE + (CMP_BLOCK - 1)
        sc = sc + _tile_heads(jnp.where(n_end <= t_q, 0.0, NEG_INF))
        e_c = jnp.exp2(sc - jnp.max(sc, axis=0, keepdims=True))
        l_c = jnp.sum(e_c, axis=0, keepdims=True)
        p_c = e_c * jnp.where(t_all >= CMP_BLOCK - 1, 1.0 / l_c, 0.0)
        o_c = _dot(vcT_ref[0, 0, :, 0:ncmp], p_c.astype(BF16))

        p_sum = p_c[:, 0:Q_TILE]
        for r in range(1, NSA_HPG):
            p_sum = p_sum + p_c[:, r * Q_TILE:(r + 1) * Q_TILE]
        ratio = SEL_BLOCK // CMP_STRIDE
        first_row = 8 - (CMP_BLOCK // CMP_STRIDE - 1)
        psum_ref[0:8, :] = jnp.zeros((8, Q_TILE), F32)
        psum_ref[8:8 + ncmp, :] = p_sum
        imp = psum_ref[pl.ds(first_row, rows, stride=ratio), :]
        for m in range(1, ratio + CMP_BLOCK // CMP_STRIDE - 1):
            imp = imp + psum_ref[pl.ds(first_row + m, rows, stride=ratio), :]
        j_idx = lax.broadcasted_iota(jnp.int32, (rows, Q_TILE), 0)
        cur = lax.shift_right_logical(t_q, SEL_BLOCK.bit_length() - 1)
        forced = (j_idx == 0) | (j_idx == cur) | (j_idx == cur - 1)
        score = jnp.where(j_idx <= cur, imp + jnp.where(forced, FORCE_BONUS, 0.0), -1.0)
        j_f = j_idx.astype(F32)
        n_pick = min(SEL_TOP_N, rows)

        def top_n(break_ties):
            left = score
            sel = jnp.zeros((rows, Q_TILE), F32)
            for _ in range(n_pick):
                best = jnp.max(left, axis=0, keepdims=True)
                pick = left == best
                if break_ties:
                    first = jnp.min(jnp.where(pick, j_f, float(rows)), axis=0, keepdims=True)
                    pick = j_f == first
                sel = jnp.where(pick, 1.0, sel)
                left = jnp.where(pick, -3.0e38, left)
            return sel

        sel = top_n(False)
        marked = jnp.sum(sel, axis=0, keepdims=True)
        sel = lax.cond(jnp.max(marked) == float(n_pick), lambda: sel, lambda: top_n(True))
        sel_ref[0:rows, :] = sel
        if parts < NSA_PARTS:
            sel_ref[rows:, :] = jnp.zeros((nsel - rows, Q_TILE), F32)
        return o_c, o_w

    part_len = seq // NSA_PARTS
    parts_needed = lax.div(s0 + (Q_TILE + part_len - 1), part_len)
    o_c, o_w = lax.switch(parts_needed - 1,
                     [functools.partial(compressed_and_select, n) for n in range(1, NSA_PARTS + 1)])

    blocks_per_tile = SEL_KV_TILE // SEL_BLOCK
    for slot in range(SEL_GROUP):
        qx_ref[slot, 0:dh, :] = q_hi
        qx_ref[slot, dh:, :] = jnp.zeros((qx_ref.shape[1] - dh, NSA_ROWS), BF16)
    piece = SEL_KV_TILE // SEL_SPLIT
    k_off = lax.broadcasted_iota(jnp.int32, (piece, 1), 0)

    def sel_scores(j, slot, causal):
        member = sel_ref[pl.ds(pl.multiple_of(j * blocks_per_tile, blocks_per_tile), blocks_per_tile), :]
        bias = _tile_heads(jnp.where(member > 0.5, 0.0, NEG_INF))
        qx_ref[slot, dh:dh + 2 * blocks_per_tile, :] = jnp.concatenate(
            [bias, jnp.zeros_like(bias)], axis=0).astype(BF16)
        parts = []
        for u in range(SEL_SPLIT):
            base = pl.multiple_of(j * SEL_KV_TILE + u * piece, piece)
            s = _dot(ks_ref[0, 0, pl.ds(base, piece), :], qx_ref[slot])
            if causal:
                s = s + _tile_heads(jnp.where((base + k_off) <= t_q, 0.0, NEG_INF))
            parts.append((base, s))
        return parts

    def sel_update(base, s, m_i, acc):
        m_new = jnp.maximum(m_i, jnp.max(s, axis=0, keepdims=True))
        p = jnp.exp2(s - m_new).astype(BF16)
        pv = _dot(vsT_ref[0, 0, :, pl.ds(base, piece)], p)
        return m_new, jnp.exp2(m_i - m_new) * acc + pv

    def sel_group(first_tile, carry, n_tiles, causal_last):
        scores = []
        for u in range(n_tiles):
            scores += sel_scores(first_tile + u, u, causal_last and u == n_tiles - 1)
        for base, s in scores:
            carry = sel_update(base, s, *carry)
        return carry

    n_tiles = lax.div(s0 + (Q_TILE + SEL_KV_TILE - 1), SEL_KV_TILE)
    n_main = lax.div(n_tiles - 1, SEL_GROUP)
    init = (jnp.full((1, NSA_ROWS), NEG_INF, F32), jnp.zeros((vsT_ref.shape[2], NSA_ROWS), F32))
    carry = lax.fori_loop(0, n_main, lambda i, c: sel_group(SEL_GROUP * i, c, SEL_GROUP, False), init)
    _, acc_s = lax.switch(
        n_tiles - SEL_GROUP * n_main - 1,
        [functools.partial(sel_group, n_tiles=n, causal_last=True) for n in range(1, SEL_GROUP + 1)],
        SEL_GROUP * n_main, carry)
    o_s = acc_s[0:dh] * (1.0 / acc_s[dh:dh + 1])

    gt_ref[...] = gate_ref[...].T
    g_row0 = pl.program_id(1) * (NSA_HPG * 3)

    def gate(branch):
        return _sigmoid(jnp.concatenate(
            [gt_ref[pl.ds(g_row0 + 3 * r + branch, 1), :] for r in range(NSA_HPG)], axis=1))

    o = gate(0) * o_c + gate(1) * o_s + gate(2) * o_w
    o_rows = jnp.concatenate([o[:, r * Q_TILE:(r + 1) * Q_TILE] for r in range(NSA_HPG)], axis=0)
    o_ref[...] = o_rows.T


def _nsa(h, kc, vcT, ks, vsT, kw, vwT, seq, gate_tile):
    b, g = ks.shape[:2]
    nqb = seq // Q_TILE
    d = NSA_HEAD_DIM
    ncp = seq // CMP_STRIDE
    nsel = seq // SEL_BLOCK
    dk = ks.shape[-1]
    dv = vsT.shape[2]
    per_bg = lambda i, j, k: (i, j, 0, 0)
    q_rows = lambda i, j, k: (i * nqb + k, j)
    return pl.pallas_call(
        functools.partial(_nsa_kernel, seq=seq),
        grid=(b, g, nqb),
        in_specs=[pl.BlockSpec((Q_TILE, NSA_HPG * d), q_rows),
                  pl.BlockSpec((1, 1, ncp, d), per_bg),
                  pl.BlockSpec((1, 1, d, ncp), per_bg),
                  pl.BlockSpec((1, 1, seq, dk), per_bg),
                  pl.BlockSpec((1, 1, dv, seq), per_bg),
                  pl.BlockSpec((1, 1, seq, d), per_bg),
                  pl.BlockSpec((1, 1, dv, seq), per_bg),
                  pl.BlockSpec((Q_TILE, LANES), lambda i, j, k: (i * nqb + k, gate_tile))],
        out_specs=pl.BlockSpec((Q_TILE, NSA_HPG * d), q_rows),
        out_shape=jax.ShapeDtypeStruct((b * seq, NSA_Q_COLS), F32),
        scratch_shapes=[pltpu.VMEM((nsel, Q_TILE), F32), pltpu.VMEM((SEL_GROUP, dk, NSA_ROWS), BF16),
                        pltpu.VMEM((ncp + 8, Q_TILE), F32), pltpu.VMEM((LANES, Q_TILE), F32)],
        compiler_params=_params(("parallel", "parallel", "arbitrary")),
        name="nsa",
    )(h, kc, vcT, ks, vsT, kw, vwT, h)


def _hgrn_chunk(q, f, v, lb, state_t, shift_ref, lower, gstart, dmat, off_mask, factored):
    c = HG_CHUNK
    log_f = jnp.log(lb + (1.0 - lb) * _sigmoid(f))
    kk = (1.0 - lb) * _sigmoid(-f)
    l1, l2, l3 = _split3(log_f)
    bt = _dot(lower, l1) + _dot(lower, l2) + _dot(lower, l3)
    bs = _dot(gstart, l1) + _dot(gstart, l2) + _dot(gstart, l3)

    qh = (q * jnp.exp(bt - bs)).astype(BF16)

    def same_block_direct():
        shift_ref[0, HG_SUB:, :] = kk
        shift_ref[1, HG_SUB:, :] = bt
        ones = jnp.ones((kk.shape[1], c), BF16)
        acc = jnp.zeros((c, c), F32)
        for d in range(HG_SUB):
            if d == 0:
                prod = q * kk
            else:
                rows = pl.ds(HG_SUB - d, c)
                prod = (q * shift_ref[0, rows, :]) * jnp.exp(bt - shift_ref[1, rows, :])
            band = _dot(prod.astype(BF16), ones)
            acc = jnp.where(dmat == d, band, acc)
        return acc

    def same_block_factored():
        return _dot_nt(qh, (kk * jnp.exp(bs - bt)).astype(BF16))

    att = same_block_factored() if factored else same_block_direct()
    att = jnp.where(dmat >= 0, att, 0.0)

    blocks = [jnp.zeros((HG_SUB, c), F32)]
    for i in range(1, c // HG_SUB):
        b_i = bs[i * HG_SUB:i * HG_SUB + 1, :]
        kh = kk * jnp.exp(jnp.minimum(b_i - bt, 0.0))
        blocks.append(_dot_nt(qh[i * HG_SUB:(i + 1) * HG_SUB], kh.astype(BF16)))
    att = jnp.where(off_mask, jnp.concatenate(blocks, axis=0), att)

    vb = v.astype(BF16)
    o = _dot_nt((q * jnp.exp(bt)).astype(BF16), state_t.astype(BF16)) + _dot(att.astype(BF16), vb)
    b_last = bt[c - 1:c, :]
    k_dec = kk * jnp.exp(b_last - bt)
    return o, state_t * jnp.exp(b_last) + _dot(v.T.astype(BF16), k_dec.astype(BF16))


def _hgrn_kernel(q_ref, f_ref, v_ref, gate_ref, lb_ref, ng_ref, o_ref, state_ref, shift_ref):
    c = HG_CHUNK

    @pl.when(pl.program_id(2) == 0)
    def _():
        state_ref[...] = jnp.zeros_like(state_ref)
        shift_ref[...] = jnp.zeros_like(shift_ref)

    row = lax.broadcasted_iota(jnp.int32, (c, c), 0)
    col = lax.broadcasted_iota(jnp.int32, (c, c), 1)
    sub = HG_SUB.bit_length() - 1
    row_blk = lax.shift_right_logical(row, sub)
    col_blk = lax.shift_right_logical(col, sub)
    lower = jnp.where(col <= row, 1.0, 0.0).astype(BF16)
    gstart = jnp.where(col_blk < row_blk, 1.0, 0.0).astype(BF16)
    dmat = jnp.where(row_blk == col_blk, row - col, -1)
    off_mask = col_blk < row_blk
    ng = ng_ref[...]

    def step(i, states, factored):
        rows = pl.ds(pl.multiple_of(i * c, c), c)
        new_states = []
        for hd in range(HG_HEADS_PER_STEP):
            lanes = slice(hd * HG_KEY_DIM, (hd + 1) * HG_KEY_DIM)
            o, state = _hgrn_chunk(q_ref[rows, lanes], f_ref[rows, lanes], v_ref[rows, lanes], lb_ref[hd],
                                   states[hd], shift_ref.at[hd], lower, gstart, dmat, off_mask, factored)
            gate = gate_ref[rows, lanes]
            ms = jnp.mean(o * o, axis=-1, keepdims=True)
            o_ref[rows, lanes] = o * lax.rsqrt(ms + RMS_EPS) * ng * (gate * _sigmoid(gate))
            new_states.append(state)
        return tuple(new_states)

    def run(factored):
        init = tuple(state_ref[hd] for hd in range(HG_HEADS_PER_STEP))
        states = lax.fori_loop(0, HG_ROWS // c, functools.partial(step, factored=factored), init)
        for hd, state in enumerate(states):
            state_ref[hd] = state

    can_factor = jnp.min(lb_ref[...]) > math.exp(-HG_FACTOR_LIMIT / HG_SUB)
    lax.cond(can_factor, functools.partial(run, True), functools.partial(run, False))


def _hgrn(h, lb, ng, bsz, seq, col0):
    nblk = seq // HG_ROWS
    hps = HG_HEADS_PER_STEP
    width = hps * HG_KEY_DIM
    tiles = HG_COLS // width
    assert (col0 * LANES) % width == 0 and seq % HG_ROWS == 0
    blk = lambda k: pl.BlockSpec((HG_ROWS, width),
                                 lambda i, j, c: (i * nblk + c, (col0 * LANES) // width + k * tiles + j))
    return pl.pallas_call(
        _hgrn_kernel,
        grid=(bsz, HG_HEADS // hps, nblk),
        in_specs=[blk(0), blk(1), blk(2), blk(3),
                  pl.BlockSpec((hps, 1, HG_KEY_DIM), lambda i, j, c: (j, 0, 0)),
                  pl.BlockSpec((1, HG_VAL_DIM), lambda i, j, c: (0, 0))],
        out_specs=pl.BlockSpec((HG_ROWS, width), lambda i, j, c: (i * nblk + c, j)),
        out_shape=jax.ShapeDtypeStruct((bsz * seq, HG_COLS), F32),
        scratch_shapes=[pltpu.VMEM((hps, HG_VAL_DIM, HG_KEY_DIM), F32),
                        pltpu.VMEM((hps, 2, HG_SUB + HG_CHUNK, HG_KEY_DIM), F32)],
        compiler_params=_params(("parallel", "parallel", "arbitrary")),
        name="hgrn",
    )(h, h, h, h, lb, ng)


def _mix_kernel(nsa_ref, hg_ref, x_ref, w_ref, ng_ref, g_ref, b_ref, o_ref):
    o_n = nsa_ref[...]
    half = o_n.shape[1]
    o_n = o_n * lax.rsqrt(jnp.mean(o_n * o_n, axis=-1, keepdims=True) + RMS_EPS) * ng_ref[...]
    mix = _dot(o_n.astype(BF16), w_ref[:half]) + _dot(hg_ref[...].astype(BF16), w_ref[half:])
    o_ref[...] = _layer_norm(DEEPNORM_ALPHA * x_ref[...] + mix, g_ref[...], b_ref[...])


def _mix(o_nsa, o_hg, x, w_out, ng, g, b, tm):
    t, d = x.shape
    half = o_nsa.shape[1]
    rows = lambda w: pl.BlockSpec((tm, w), lambda i: (i, 0))
    full = lambda r, c: pl.BlockSpec((r, c), lambda i: (0, 0))
    return pl.pallas_call(
        _mix_kernel,
        grid=(t // tm,),
        in_specs=[rows(half), rows(half), rows(d), full(d, d), full(1, half), full(1, d), full(1, d)],
        out_specs=rows(d),
        out_shape=jax.ShapeDtypeStruct((t, d), F32),
        compiler_params=_params(("parallel",)),
        name="mix",
    )(o_nsa, o_hg, x, w_out, ng, g, b)


def _xattn_kernel(x_ref, wq_ref, kT_ref, v_ref, wo_ref, g_ref, b_ref, o_ref):
    x = x_ref[0]
    d = x.shape[1]
    dh = d // XA_HEADS
    q = _dot(x.astype(BF16), wq_ref[...])
    heads = []
    for h in range(XA_HEADS):
        cols = slice(h * dh, (h + 1) * dh)
        s = _dot(q[:, cols].astype(BF16), kT_ref[0, cols, :]) * (dh ** -0.5)
        e = jnp.exp(s - jnp.max(s, axis=-1, keepdims=True))
        p = e * (1.0 / jnp.sum(e, axis=-1, keepdims=True))
        heads.append(_dot(p.astype(BF16), v_ref[0, :, cols]))
    o = jnp.concatenate(heads, axis=1)
    xa = _dot(o.astype(BF16), wo_ref[...])
    o_ref[0] = _layer_norm(DEEPNORM_ALPHA * x + xa, g_ref[...], b_ref[...])


def _xattn(x, wq, kT, v, wo, g, b, tm):
    bsz, seq, d = x.shape
    m = v.shape[1]
    full = lambda r, c: pl.BlockSpec((r, c), lambda i, j: (0, 0))
    return pl.pallas_call(
        _xattn_kernel,
        grid=(bsz, seq // tm),
        in_specs=[pl.BlockSpec((1, tm, d), lambda i, j: (i, j, 0)),
                  full(d, d),
                  pl.BlockSpec((1, d, m), lambda i, j: (i, 0, 0)),
                  pl.BlockSpec((1, m, d), lambda i, j: (i, 0, 0)),
                  full(d, d), full(1, d), full(1, d)],
        out_specs=pl.BlockSpec((1, tm, d), lambda i, j: (i, j, 0)),
        out_shape=jax.ShapeDtypeStruct((bsz, seq, d), F32),
        compiler_params=_params(("parallel", "parallel")),
        name="xattn",
    )(x, wq, kT, v, wo, g, b)


def _route_kernel(x_ref, wh_ref, wl_ref, bias_ref, ids_ref, wts_ref, rnk_ref, counts_ref, cnt_ref):
    xh, xl = _split2(x_ref[...])
    logits = _dot(xh, wh_ref[...]) + _dot(xh, wl_ref[...]) + _dot(xl, wh_ref[...]) + bias_ref[...]
    lane = lax.broadcasted_iota(jnp.int32, logits.shape, 1)
    lane_f = lane.astype(F32)
    big = float(LANES)

    is_g = lane < MOE_GROUPS
    g_max = jnp.max(jnp.where(is_g, logits, NEG_INF), axis=-1, keepdims=True)
    g_sum = jnp.sum(jnp.where(is_g, jnp.exp(logits - g_max), 0.0), axis=-1, keepdims=True)
    g_w = 1.0 / g_sum
    g_idx = jnp.min(jnp.where(is_g & (logits == g_max), lane_f, big), axis=-1, keepdims=True)

    e_lo = MOE_GROUPS + MOE_EXPERTS_PER_GROUP * g_idx
    is_e = (lane_f >= e_lo) & (lane_f < e_lo + MOE_EXPERTS_PER_GROUP)
    e_log = jnp.where(is_e, logits, NEG_INF)
    e_max = jnp.max(e_log, axis=-1, keepdims=True)
    e_exp = jnp.where(is_e, jnp.exp(logits - e_max), 0.0)
    e_sum = jnp.sum(e_exp, axis=-1, keepdims=True)
    i1 = jnp.min(jnp.where(is_e & (e_log == e_max), lane_f, big), axis=-1, keepdims=True)
    rest = jnp.where(lane_f == i1, NEG_INF, e_log)
    r_max = jnp.max(rest, axis=-1, keepdims=True)
    i2 = jnp.min(jnp.where(is_e & (lane_f != i1) & (rest == r_max), lane_f, big), axis=-1, keepdims=True)
    p1 = 1.0 / e_sum
    p2 = jnp.exp(r_max - e_max) / e_sum
    tot = p1 + p2
    w1 = g_w * (p1 / tot)
    w2 = g_w * (p2 / tot)
    ids = jnp.where(lane == 0, i1, i2) - float(MOE_GROUPS)
    ids_ref[...] = ids.astype(jnp.int32)
    wts_ref[...] = jnp.where(lane == 0, w1, w2)

    @pl.when(pl.program_id(0) == 0)
    def _():
        cnt_ref[...] = jnp.zeros_like(cnt_ref)

    tm = logits.shape[0]
    hit1 = lane_f == i1
    hit2 = lane_f == i2
    hits = jnp.where(hit1 | hit2, 1.0, 0.0)
    earlier = lax.broadcasted_iota(jnp.int32, (tm, tm), 1) < lax.broadcasted_iota(jnp.int32, (tm, tm), 0)
    before = _dot(jnp.where(earlier, 1.0, 0.0).astype(BF16), hits.astype(BF16)) + cnt_ref[...]
    rank1 = jnp.sum(jnp.where(hit1, before, 0.0), axis=-1, keepdims=True)
    rank2 = jnp.sum(jnp.where(hit2, before, 0.0), axis=-1, keepdims=True)
    rnk_ref[...] = jnp.where(lane == 0, rank1, rank2).astype(jnp.int32)
    cnt_ref[...] = cnt_ref[...] + jnp.sum(hits, axis=0, keepdims=True)
    counts_ref[...] = cnt_ref[...].astype(jnp.int32)


def _route(x, wh, wl, bias, tm):
    t, d = x.shape
    rows = lambda w: pl.BlockSpec((tm, w), lambda i: (i, 0))
    full = lambda r, c: pl.BlockSpec((r, c), lambda i: (0, 0))
    i32 = jnp.int32
    return pl.pallas_call(
        _route_kernel,
        grid=(t // tm,),
        in_specs=[rows(d), full(d, LANES), full(d, LANES), full(1, LANES)],
        out_specs=[rows(LANES), rows(LANES), rows(LANES), full(1, LANES)],
        out_shape=[jax.ShapeDtypeStruct((t, LANES), i32), jax.ShapeDtypeStruct((t, LANES), F32),
                   jax.ShapeDtypeStruct((t, LANES), i32), jax.ShapeDtypeStruct((1, LANES), i32)],
        scratch_shapes=[pltpu.VMEM((1, LANES), F32)],
        compiler_params=_params(("arbitrary",)),
        name="route",
    )(x, wh, wl, bias)


def _row_copy_wait(src, dst, sem, rows):
    pltpu.make_async_copy(src.at[pl.ds(0, rows)], dst.at[pl.ds(0, rows)], sem).wait()


def _dispatch_kernel(dest_ref, zfill_ref, x_ref, xs_hbm, zbuf, zsem, sem):
    i = pl.program_id(0)
    tm = x_ref.shape[0]

    @pl.when(i == 0)
    def _():
        zbuf[...] = jnp.zeros_like(zbuf)
        tail = lambda e: xs_hbm.at[pl.ds(pl.multiple_of(zfill_ref[e], MOE_BLOCK), MOE_BLOCK)]
        for e in range(zfill_ref.shape[0]):
            @pl.when(zfill_ref[e] >= 0)
            def _():
                pltpu.make_async_copy(zbuf, tail(e), zsem).start()
        for e in range(zfill_ref.shape[0]):
            @pl.when(zfill_ref[e] >= 0)
            def _():
                pltpu.make_async_copy(zbuf, tail(e), zsem).wait()

    def send(j, c):
        for k in range(MOE_TOP_K):
            slot = dest_ref[(i * tm + j) * MOE_TOP_K + k]
            pltpu.make_async_copy(x_ref.at[pl.ds(j, 1)], xs_hbm.at[pl.ds(slot, 1)], sem).start()
        return c

    lax.fori_loop(0, tm, send, 0, unroll=8)
    for _ in range(MOE_TOP_K):
        _row_copy_wait(x_ref, xs_hbm, sem, tm)


def _dispatch(dest, zfill, x, cap, tm):
    t, d = x.shape
    grid_spec = pltpu.PrefetchScalarGridSpec(
        num_scalar_prefetch=2,
        grid=(t // tm,),
        in_specs=[pl.BlockSpec((tm, d), lambda i, dest, zfill: (i, 0))],
        out_specs=pl.BlockSpec(memory_space=pl.ANY),
        scratch_shapes=[pltpu.VMEM((MOE_BLOCK, d), F32), pltpu.SemaphoreType.DMA, pltpu.SemaphoreType.DMA])
    return pl.pallas_call(
        _dispatch_kernel,
        grid_spec=grid_spec,
        out_shape=jax.ShapeDtypeStruct((cap, d), F32),
        compiler_params=_params(("arbitrary",)),
        name="dispatch",
    )(dest, zfill, x)


def _experts_kernel(bexp_ref, nact_ref, xs_ref, wg_ref, wu_ref, wd_ref, y_ref):
    @pl.when(pl.program_id(0) < nact_ref[0])
    def _():
        xb = xs_ref[...].astype(BF16)
        gate = _dot(xb, wg_ref[0])
        hid = (gate * _sigmoid(gate)) * _dot(xb, wu_ref[0])
        y_ref[...] = _dot(hid.astype(BF16), wd_ref[0])

    @pl.when(pl.program_id(0) >= nact_ref[0])
    def _():
        y_ref[...] = jnp.zeros_like(y_ref)


def _experts(bexp, nact, xs, wg, wu, wd):
    cap, d = xs.shape
    dff = wg.shape[-1]
    by_expert = lambda i, bexp, nact: (bexp[i], 0, 0)
    active = lambda i, bexp, nact: (jnp.minimum(i, nact[0] - 1), 0)
    grid_spec = pltpu.PrefetchScalarGridSpec(
        num_scalar_prefetch=2,
        grid=(cap // MOE_BLOCK,),
        in_specs=[pl.BlockSpec((MOE_BLOCK, d), active),
                  pl.BlockSpec((1, d, dff), by_expert),
                  pl.BlockSpec((1, d, dff), by_expert),
                  pl.BlockSpec((1, dff, d), by_expert)],
        out_specs=pl.BlockSpec((MOE_BLOCK, d), lambda i, bexp, nact: (i, 0)))
    return pl.pallas_call(
        _experts_kernel,
        grid_spec=grid_spec,
        out_shape=jax.ShapeDtypeStruct((cap, d), F32),
        compiler_params=_params(("arbitrary",)),
        name="experts",
    )(bexp, nact, xs, wg, wu, wd)


def _combine_kernel(dest_ref, y_hbm, x_ref, wts_ref, g_ref, b_ref, o_ref, ybuf, sem):
    i = pl.program_id(0)
    n = pl.num_programs(0)
    tm = x_ref.shape[0]
    cur = lax.rem(i, 2)

    def fetch(tile, buf):
        def one(j, c):
            for k in range(MOE_TOP_K):
                slot = dest_ref[(tile * tm + j) * MOE_TOP_K + k]
                pltpu.make_async_copy(y_hbm.at[pl.ds(slot, 1)], ybuf.at[buf, k, pl.ds(j, 1)], sem.at[buf]).start()
            return c
        lax.fori_loop(0, tm, one, 0, unroll=8)

    @pl.when(i == 0)
    def _():
        fetch(0, 0)

    @pl.when(i + 1 < n)
    def _():
        fetch(i + 1, 1 - cur)

    for k in range(MOE_TOP_K):
        _row_copy_wait(y_hbm, ybuf.at[cur, k], sem.at[cur], tm)
    w = wts_ref[...]
    ff = w[:, 0:1] * ybuf[cur, 0]
    for k in range(1, MOE_TOP_K):
        ff = ff + w[:, k:k + 1] * ybuf[cur, k]
    o_ref[...] = _layer_norm(DEEPNORM_ALPHA * x_ref[...] + ff, g_ref[...], b_ref[...])


def _combine(dest, y, x, wts, g, b, tm):
    t, d = x.shape
    rows = lambda w: pl.BlockSpec((tm, w), lambda i, dest: (i, 0))
    full = lambda r, c: pl.BlockSpec((r, c), lambda i, dest: (0, 0))
    grid_spec = pltpu.PrefetchScalarGridSpec(
        num_scalar_prefetch=1,
        grid=(t // tm,),
        in_specs=[pl.BlockSpec(memory_space=pl.ANY), rows(d), rows(LANES), full(1, d), full(1, d)],
        out_specs=rows(d),
        scratch_shapes=[pltpu.VMEM((2, MOE_TOP_K, tm, d), F32), pltpu.SemaphoreType.DMA((2,))])
    return pl.pallas_call(
        _combine_kernel,
        grid_spec=grid_spec,
        out_shape=jax.ShapeDtypeStruct((t, d), F32),
        compiler_params=_params(("arbitrary",)),
        name="combine",
    )(dest, y, x, wts, g, b)


def _dispatch_plan(ids, rnk, counts, n_tok):
    padded = ((counts + MOE_BLOCK - 1) // MOE_BLOCK) * MOE_BLOCK
    pend = jnp.cumsum(padded)
    pstart = pend - padded
    dest = (pstart[ids] + rnk).reshape(-1).astype(jnp.int32)
    cap = n_tok * MOE_TOP_K + MOE_N_EXPERTS * MOE_BLOCK
    n_blocks = cap // MOE_BLOCK
    block_start = jnp.arange(n_blocks, dtype=jnp.int32) * MOE_BLOCK
    bexp = jnp.minimum(jnp.sum(block_start[:, None] >= pend[None, :], axis=1), MOE_N_EXPERTS - 1).astype(jnp.int32)
    nact = (pend[-1] // MOE_BLOCK).astype(jnp.int32).reshape(1)
    tails = jnp.where(padded > 0, pend - MOE_BLOCK, -1)
    idle = pend[-1] + block_start[:MOE_N_EXPERTS]
    zfill = jnp.concatenate([tails, jnp.where(idle < cap, idle, -1)]).astype(jnp.int32)
    return dest, bexp, nact, zfill, cap


def kernel(x, mem, w_in, cmp_pe_k, cmp_pe_v, cmp_w1_k, cmp_w2_k, cmp_w1_v, cmp_w2_v, nsa_norm_g,
           hg_lb_logits, hg_norm_g, w_out, ln1_g, ln1_b, xa_wq, xa_wk, xa_wv, xa_wo, ln2_g, ln2_b,
           moe_w_group, moe_b_group, moe_w_expert, moe_b_expert, moe_w_gate, moe_w_up, moe_w_down,
           ln3_g, ln3_b):
    b, s, d = x.shape
    t = b * s
    g, dh = NSA_KV_GROUPS, NSA_HEAD_DIM
    nch = s // CMP_STRIDE
    row = lambda a: a.reshape(1, -1)
    lb_all = jnp.cumsum(jax.nn.softmax(hg_lb_logits.astype(F32), axis=0), axis=0)
    xt = x.reshape(t, d)
    for l in range(DEPTH):
        nsa_cols = NSA_Q_COLS + 6 * NSA_KV_COLS
        w_l = w_in[l]
        w_perm = jnp.concatenate(
            [w_l[:, :NSA_Q_COLS], w_l[:, nsa_cols + NSA_GATE_COLS:], w_l[:, NSA_Q_COLS:nsa_cols],
             w_l[:, nsa_cols:nsa_cols + NSA_GATE_COLS],
             jnp.zeros((d, LANES - NSA_GATE_COLS), w_l.dtype)], axis=1)
        h, ks, kw, vsT, vwT = _inproj(xt, w_perm.astype(BF16), b, s, 256)
        nkc = h[:, NSA_KV0:NSA_KV0 + NSA_KV_COLS]
        nvc = h[:, NSA_KV0 + NSA_KV_COLS:NSA_KV0 + 2 * NSA_KV_COLS]
        hg_col0 = NSA_Q_COLS // LANES
        gate_tile = (nsa_cols + 4 * HG_COLS) // LANES

        chunks = lambda a: jnp.transpose(a.reshape(b, nch, CMP_STRIDE, g, dh), (0, 3, 1, 2, 4)).reshape(
            b, g, nch, CMP_STRIDE * dh)
        comp = _compress(jnp.stack([chunks(nkc), chunks(nvc)]),
                         jnp.stack([cmp_pe_k[l].reshape(1, -1), cmp_pe_v[l].reshape(1, -1)]),
                         jnp.stack([cmp_w1_k[l], cmp_w1_v[l]]),
                         jnp.stack([cmp_w2_k[l], cmp_w2_v[l]]))
        kc = comp[0].astype(BF16)
        vcT = jnp.swapaxes(comp[1], -1, -2).astype(BF16)

        o_nsa = _nsa(h, kc, vcT, ks, vsT, kw, vwT, s, gate_tile)

        o_hg = _hgrn(h, lb_all[l].reshape(HG_HEADS, 1, HG_KEY_DIM), row(hg_norm_g[l]), b, s, hg_col0)

        x1 = _mix(o_nsa, o_hg, xt, w_out[l].astype(BF16), row(nsa_norm_g[l]), row(ln1_g[l]), row(ln1_b[l]), 512)

        n_mem = mem.shape[1]
        kv = _matmul(mem.reshape(b * n_mem, d),
                     jnp.concatenate([xa_wk[l], xa_wv[l]], axis=1).astype(BF16), n_mem)
        kT = jnp.swapaxes(kv[:, :d].reshape(b, n_mem, d), 1, 2).astype(BF16)
        v = kv[:, d:].reshape(b, n_mem, d).astype(BF16)
        x2 = _xattn(x1.reshape(b, s, d), xa_wq[l].astype(BF16), kT, v, xa_wo[l].astype(BF16),
                    row(ln2_g[l]), row(ln2_b[l]), 512).reshape(t, d)

        w_r = jnp.concatenate([moe_w_group[l], moe_w_expert[l]], axis=1)
        w_r = jnp.pad(w_r, ((0, 0), (0, LANES - w_r.shape[1])))
        b_r = jnp.pad(jnp.concatenate([moe_b_group[l], moe_b_expert[l]]), (0, LANES - MOE_GROUPS - MOE_N_EXPERTS))
        w_rh, w_rl = _split2(w_r)
        ids, wts, rnk, counts = _route(x2, w_rh, w_rl, row(b_r), 512)
        dest, bexp, nact, zfill, cap = _dispatch_plan(
            ids[:, :MOE_TOP_K], rnk[:, :MOE_TOP_K], counts[0, MOE_GROUPS:MOE_GROUPS + MOE_N_EXPERTS], t)
        xs = _dispatch(dest, zfill, x2, cap, 256)
        ys = _experts(bexp, nact, xs, moe_w_gate[l].astype(BF16), moe_w_up[l].astype(BF16),
                      moe_w_down[l].astype(BF16))
        xt = _combine(dest, ys, x2, wts, row(ln3_g[l]), row(ln3_b[l]), 256)
    return xt.reshape(b, s, d)
```

```python
import functools
import math

import jax
import jax.numpy as jnp
from jax import lax
from jax.experimental import pallas as pl
from jax.experimental.pallas import tpu as pltpu

F32 = jnp.float32
BF16 = jnp.bfloat16

NSA_HEAD_DIM = 64
NSA_HEADS = 8
NSA_KV_GROUPS = 2
NSA_HPG = NSA_HEADS // NSA_KV_GROUPS
CMP_BLOCK = 32
CMP_STRIDE = 16
CMP_HIDDEN = 256
SEL_BLOCK = 64
SEL_TOP_N = 16
WINDOW = 512
FORCE_BONUS = 1.0e4
HG_KEY_DIM = 128
HG_VAL_DIM = 128
HG_HEADS = 4
XA_HEADS = 4
MOE_GROUPS = 4
MOE_EXPERTS_PER_GROUP = 8
MOE_N_EXPERTS = MOE_GROUPS * MOE_EXPERTS_PER_GROUP
MOE_TOP_K = 2
DEPTH = 1
DEEPNORM_ALPHA = (2.0 * DEPTH) ** 0.25
LN_EPS = 1e-5
RMS_EPS = 1e-6
NEG_INF = -1e30

NSA_Q_COLS = NSA_HEADS * NSA_HEAD_DIM
NSA_KV_COLS = NSA_KV_GROUPS * NSA_HEAD_DIM
NSA_GATE_COLS = NSA_HEADS * 3
HG_COLS = HG_HEADS * HG_KEY_DIM
NSA_KV0 = NSA_Q_COLS + 4 * HG_COLS

LANES = 128
Q_TILE = 128
NSA_ROWS = NSA_HPG * Q_TILE
SEL_KV_TILE = 512
SEL_GROUP = 6
SEL_SPLIT = 2
NSA_PARTS = 4
NSA_V_ROWS = NSA_HEAD_DIM + 16
LOG2_E = 1.4426950408889634
HG_CHUNK = 128
HG_SUB = 16
HG_ROWS = 512
HG_HEADS_PER_STEP = 4
HG_FACTOR_LIMIT = 60.0
MOE_BLOCK = 256
VMEM_LIMIT = 48 * 1024 * 1024


def _dot(a, b):
    return jnp.dot(a, b, preferred_element_type=F32)


def _dot_nt(a, b):
    return lax.dot_general(a, b, (((1,), (1,)), ((), ())), preferred_element_type=F32)


def _split2(a):
    hi = a.astype(BF16)
    lo = (a - hi.astype(F32)).astype(BF16)
    return hi, lo


def _split3(a):
    p1 = a.astype(BF16)
    r1 = a - p1.astype(F32)
    p2 = r1.astype(BF16)
    p3 = (r1 - p2.astype(F32)).astype(BF16)
    return p1, p2, p3


def _dot3(a, b):
    ah, al = _split2(a)
    bh, bl = _split2(b)
    return _dot(ah, bh) + _dot(ah, bl) + _dot(al, bh)


def _sigmoid(x):
    return 1.0 / (1.0 + jnp.exp(-x))


def _layer_norm(y, g, b):
    mu = jnp.mean(y, axis=-1, keepdims=True)
    d = y - mu
    var = jnp.mean(d * d, axis=-1, keepdims=True)
    return d * lax.rsqrt(var + LN_EPS) * g + b


def _params(sem):
    return pltpu.CompilerParams(dimension_semantics=sem, vmem_limit_bytes=VMEM_LIMIT)


def _matmul_kernel(x_ref, w_ref, o_ref):
    o_ref[...] = _dot(x_ref[...].astype(BF16), w_ref[...]).astype(o_ref.dtype)


def _matmul(x, w, tm):
    m, k = x.shape
    n = w.shape[1]
    return pl.pallas_call(
        _matmul_kernel,
        grid=(m // tm,),
        in_specs=[pl.BlockSpec((tm, k), lambda i: (i, 0)),
                  pl.BlockSpec((k, n), lambda i: (0, 0))],
        out_specs=pl.BlockSpec((tm, n), lambda i: (i, 0)),
        out_shape=jax.ShapeDtypeStruct((m, n), F32),
        compiler_params=_params(("parallel",)),
        name="proj",
    )(x, w)


def _inproj_kernel(x_ref, w_ref, h_ref, ks_ref, kw_ref, vsT_ref, vwT_ref, *, tiles_per_seq):
    h = _dot(x_ref[...].astype(BF16), w_ref[...])
    h_ref[...] = h
    tm = h.shape[0]
    dh = NSA_HEAD_DIM
    k_sel0 = NSA_KV0 + 2 * NSA_KV_COLS
    v_sel0, k_win0, v_win0 = k_sel0 + NSA_KV_COLS, k_sel0 + 2 * NSA_KV_COLS, k_sel0 + 3 * NSA_KV_COLS
    pos = lax.rem(pl.program_id(0), tiles_per_seq) * tm + lax.broadcasted_iota(jnp.int32, (tm, 1), 0)
    blk = lax.shift_right_logical(pos & (SEL_KV_TILE - 1), SEL_BLOCK.bit_length() - 1)
    onehot = jnp.where(blk == lax.broadcasted_iota(jnp.int32, (1, LANES - dh), 1), 1.0, 0.0)
    ones_row = jnp.where(lax.broadcasted_iota(jnp.int32, (NSA_V_ROWS - dh, tm), 0) == 0, 1.0, 0.0)
    vsT = h[:, v_sel0:v_sel0 + NSA_KV_COLS].T
    vwT = h[:, v_win0:v_win0 + NSA_KV_COLS].T
    for g in range(NSA_KV_GROUPS):
        cols = slice(g * dh, (g + 1) * dh)
        ks_ref[0, g] = jnp.concatenate([h[:, k_sel0 + g * dh:k_sel0 + (g + 1) * dh], onehot], axis=1).astype(BF16)
        kw_ref[0, g] = h[:, k_win0 + g * dh:k_win0 + (g + 1) * dh].astype(BF16)
        vsT_ref[0, g] = jnp.concatenate([vsT[cols], ones_row], axis=0).astype(BF16)
        vwT_ref[0, g] = jnp.concatenate([vwT[cols], ones_row], axis=0).astype(BF16)


def _inproj(x, w, bsz, seq, tm):
    t, k = x.shape
    n = w.shape[1]
    g, dh = NSA_KV_GROUPS, NSA_HEAD_DIM
    tps = seq // tm
    keys = lambda width: pl.BlockSpec((1, g, tm, width), lambda i: (i // tps, 0, i % tps, 0))
    vals = pl.BlockSpec((1, g, NSA_V_ROWS, tm), lambda i: (i // tps, 0, 0, i % tps))
    return pl.pallas_call(
        functools.partial(_inproj_kernel, tiles_per_seq=tps),
        grid=(t // tm,),
        in_specs=[pl.BlockSpec((tm, k), lambda i: (i, 0)),
                  pl.BlockSpec((k, n), lambda i: (0, 0))],
        out_specs=[pl.BlockSpec((tm, n), lambda i: (i, 0)), keys(LANES), keys(dh), vals, vals],
        out_shape=[jax.ShapeDtypeStruct((t, n), F32),
                   jax.ShapeDtypeStruct((bsz, g, seq, LANES), BF16),
                   jax.ShapeDtypeStruct((bsz, g, seq, dh), BF16),
                   jax.ShapeDtypeStruct((bsz, g, NSA_V_ROWS, seq), BF16),
                   jax.ShapeDtypeStruct((bsz, g, NSA_V_ROWS, seq), BF16)],
        compiler_params=_params(("parallel",)),
        name="inproj",
    )(x, w)


def _compress_kernel(ch_ref, pe_ref, w1_ref, w2_ref, o_ref):
    ch = ch_ref[0, 0, 0]
    half = ch.shape[1]
    nch = ch.shape[0]
    pe = pe_ref[0]
    w1 = w1_ref[0]
    top = _dot3(ch + pe[:, :half], w1[:half])
    bot = _dot3(ch + pe[:, half:], w1[half:])
    hid = top + pltpu.roll(bot, nch - 1, 0)
    c = 0.7978845608028654
    act = 0.5 * hid * (1.0 + jnp.tanh(c * (hid + 0.044715 * hid * hid * hid)))
    out = _dot3(act, w2_ref[0])
    row = lax.broadcasted_iota(jnp.int32, out.shape, 0)
    o_ref[0, 0, 0] = jnp.where(row < nch - 1, out, 0.0)


def _compress(ch, pe, w1, w2):
    _, b, g, nch, width = ch.shape
    hidden = w1.shape[-1]
    d = w2.shape[-1]
    return pl.pallas_call(
        _compress_kernel,
        grid=(2, b, g),
        in_specs=[pl.BlockSpec((1, 1, 1, nch, width), lambda a, i, j: (a, i, j, 0, 0)),
                  pl.BlockSpec((1, 1, 2 * width), lambda a, i, j: (a, 0, 0)),
                  pl.BlockSpec((1, 2 * width, hidden), lambda a, i, j: (a, 0, 0)),
                  pl.BlockSpec((1, hidden, d), lambda a, i, j: (a, 0, 0))],
        out_specs=pl.BlockSpec((1, 1, 1, nch, d), lambda a, i, j: (a, i, j, 0, 0)),
        out_shape=jax.ShapeDtypeStruct((2, b, g, nch, d), F32),
        compiler_params=_params(("parallel", "parallel", "parallel")),
        name="compress",
    )(ch, pe, w1, w2)


def _tile_heads(a):
    return jnp.concatenate([a] * NSA_HPG, axis=1)


def _nsa_kernel(q_ref, kc_ref, vcT_ref, ks_ref, vsT_ref, kw_ref, vwT_ref, gate_ref,
                o_ref, sel_ref, qx_ref, psum_ref, gt_ref, *, seq):
    s0 = pl.program_id(2) * Q_TILE
    nsel = seq // SEL_BLOCK
    ncmp_pad = seq // CMP_STRIDE
    dh = NSA_HEAD_DIM

    q_t = q_ref[...].T
    q = jnp.concatenate([q_t[r * dh:(r + 1) * dh] for r in range(NSA_HPG)], axis=1)
    q_hi = (q * (dh ** -0.5 * LOG2_E)).astype(BF16)
    t_q = s0 + lax.broadcasted_iota(jnp.int32, (1, Q_TILE), 1)
    t_all = _tile_heads(t_q)

    def window_branch():
        span = WINDOW + Q_TILE
        lo = pl.multiple_of(jnp.maximum(s0 - WINDOW, 0), Q_TILE)
        kpos = lo + lax.broadcasted_iota(jnp.int32, (span, 1), 0)
        sw = _dot(kw_ref[0, 0, pl.ds(lo, span), :], q_hi)
        sw = sw + _tile_heads(jnp.where((kpos <= t_q) & (kpos > t_q - WINDOW), 0.0, NEG_INF))
        e_w = jnp.exp2(sw - jnp.max(sw, axis=0, keepdims=True)).astype(BF16)
        acc_w = _dot(vwT_ref[0, 0, :, pl.ds(lo, span)], e_w)
        return acc_w[0:dh] * (1.0 / acc_w[dh:dh + 1])

    def compressed_and_select(parts):
        ncmp = parts * (ncmp_pad // NSA_PARTS)
        rows = parts * (nsel // NSA_PARTS)
        o_w = window_branch()
        sc = _dot(kc_ref[0, 0, 0:ncmp, :], q_hi)
        n_end = lax.broadcasted_iota(jnp.int32, (ncmp, 1), 0) * CMP_STRIDE + (CMP_BLOCK - 1)
        sc = sc + _tile_heads(jnp.where(n_end <= t_q, 0.0, NEG_INF))
        e_c = jnp.exp2(sc - jnp.max(sc, axis=0, keepdims=True))
        l_c = jnp.sum(e_c, axis=0, keepdims=True)
        p_c = e_c * jnp.where(t_all >= CMP_BLOCK - 1, 1.0 / l_c, 0.0)
        o_c = _dot(vcT_ref[0, 0, :, 0:ncmp], p_c.astype(BF16))

        p_sum = p_c[:, 0:Q_TILE]
        for r in range(1, NSA_HPG):
            p_sum = p_sum + p_c[:, r * Q_TILE:(r + 1) * Q_TILE]
        ratio = SEL_BLOCK // CMP_STRIDE
        first_row = 8 - (CMP_BLOCK // CMP_STRIDE - 1)
        psum_ref[0:8, :] = jnp.zeros((8, Q_TILE), F32)
        psum_ref[8:8 + ncmp, :] = p_sum
        imp = psum_ref[pl.ds(first_row, rows, stride=ratio), :]
        for m in range(1, ratio + CMP_BLOCK // CMP_STRIDE - 1):
            imp = imp + psum_ref[pl.ds(first_row + m, rows, stride=ratio), :]
        j_idx = lax.broadcasted_iota(jnp.int32, (rows, Q_TILE), 0)
        cur = lax.shift_right_logical(t_q, SEL_BLOCK.bit_length() - 1)
        forced = (j_idx == 0) | (j_idx == cur) | (j_idx == cur - 1)
        score = jnp.where(j_idx <= cur, imp + jnp.where(forced, FORCE_BONUS, 0.0), -1.0)
        j_f = j_idx.astype(F32)
        sel = jnp.zeros((rows, Q_TILE), F32)
        for _ in range(min(SEL_TOP_N, rows)):
            best = jnp.max(score, axis=0, keepdims=True)
            first = jnp.min(jnp.where(score == best, j_f, float(rows)), axis=0, keepdims=True)
            pick = j_f == first
            sel = jnp.where(pick, 1.0, sel)
            score = jnp.where(pick, -3.0e38, score)
        sel_ref[0:rows, :] = sel
        if parts < NSA_PARTS:
            sel_ref[rows:, :] = jnp.zeros((nsel - rows, Q_TILE), F32)
        return o_c, o_w

    part_len = seq // NSA_PARTS
    parts_needed = lax.div(s0 + (Q_TILE + part_len - 1), part_len)
    o_c, o_w = lax.switch(parts_needed - 1,
                     [functools.partial(compressed_and_select, n) for n in range(1, NSA_PARTS + 1)])

    blocks_per_tile = SEL_KV_TILE // SEL_BLOCK
    for slot in range(SEL_GROUP):
        qx_ref[slot, 0:dh, :] = q_hi
        qx_ref[slot, dh:, :] = jnp.zeros((qx_ref.shape[1] - dh, NSA_ROWS), BF16)
    piece = SEL_KV_TILE // SEL_SPLIT
    k_off = lax.broadcasted_iota(jnp.int32, (piece, 1), 0)

    def sel_scores(j, slot, causal):
        member = sel_ref[pl.ds(pl.multiple_of(j * blocks_per_tile, blocks_per_tile), blocks_per_tile), :]
        bias = _tile_heads(jnp.where(member > 0.5, 0.0, NEG_INF))
        qx_ref[slot, dh:dh + 2 * blocks_per_tile, :] = jnp.concatenate(
            [bias, jnp.zeros_like(bias)], axis=0).astype(BF16)
        parts = []
        for u in range(SEL_SPLIT):
            base = pl.multiple_of(j * SEL_KV_TILE + u * piece, piece)
            s = _dot(ks_ref[0, 0, pl.ds(base, piece), :], qx_ref[slot])
            if causal:
                s = s + _tile_heads(jnp.where((base + k_off) <= t_q, 0.0, NEG_INF))
            parts.append((base, s))
        return parts

    def sel_update(base, s, m_i, acc):
        m_new = jnp.maximum(m_i, jnp.max(s, axis=0, keepdims=True))
        p = jnp.exp2(s - m_new).astype(BF16)
        pv = _dot(vsT_ref[0, 0, :, pl.ds(base, piece)], p)
        return m_new, jnp.exp2(m_i - m_new) * acc + pv

    def sel_group(first_tile, carry, n_tiles, causal_last):
        scores = []
        for u in range(n_tiles):
            scores += sel_scores(first_tile + u, u, causal_last and u == n_tiles - 1)
        for base, s in scores:
            carry = sel_update(base, s, *carry)
        return carry

    n_tiles = lax.div(s0 + (Q_TILE + SEL_KV_TILE - 1), SEL_KV_TILE)
    n_main = lax.div(n_tiles - 1, SEL_GROUP)
    init = (jnp.full((1, NSA_ROWS), NEG_INF, F32), jnp.zeros((vsT_ref.shape[2], NSA_ROWS), F32))
    carry = lax.fori_loop(0, n_main, lambda i, c: sel_group(SEL_GROUP * i, c, SEL_GROUP, False), init)
    _, acc_s = lax.switch(
        n_tiles - SEL_GROUP * n_main - 1,
        [functools.partial(sel_group, n_tiles=n, causal_last=True) for n in range(1, SEL_GROUP + 1)],
        SEL_GROUP * n_main, carry)
    o_s = acc_s[0:dh] * (1.0 / acc_s[dh:dh + 1])

    gt_ref[...] = gate_ref[...].T
    g_row0 = pl.program_id(1) * (NSA_HPG * 3)

    def gate(branch):
        return _sigmoid(jnp.concatenate(
            [gt_ref[pl.ds(g_row0 + 3 * r + branch, 1), :] for r in range(NSA_HPG)], axis=1))

    o = gate(0) * o_c + gate(1) * o_s + gate(2) * o_w
    o_rows = jnp.concatenate([o[:, r * Q_TILE:(r + 1) * Q_TILE] for r in range(NSA_HPG)], axis=0)
    o_ref[...] = o_rows.T


def _nsa(h, kc, vcT, ks, vsT, kw, vwT, seq, gate_tile):
    b, g = ks.shape[:2]
    nqb = seq // Q_TILE
    d = NSA_HEAD_DIM
    ncp = seq // CMP_STRIDE
    nsel = seq // SEL_BLOCK
    dk = ks.shape[-1]
    dv = vsT.shape[2]
    per_bg = lambda i, j, k: (i, j, 0, 0)
    q_rows = lambda i, j, k: (i * nqb + k, j)
    return pl.pallas_call(
        functools.partial(_nsa_kernel, seq=seq),
        grid=(b, g, nqb),
        in_specs=[pl.BlockSpec((Q_TILE, NSA_HPG * d), q_rows),
                  pl.BlockSpec((1, 1, ncp, d), per_bg),
                  pl.BlockSpec((1, 1, d, ncp), per_bg),
                  pl.BlockSpec((1, 1, seq, dk), per_bg),
                  pl.BlockSpec((1, 1, dv, seq), per_bg),
                  pl.BlockSpec((1, 1, seq, d), per_bg),
                  pl.BlockSpec((1, 1, dv, seq), per_bg),
                  pl.BlockSpec((Q_TILE, LANES), lambda i, j, k: (i * nqb + k, gate_tile))],
        out_specs=pl.BlockSpec((Q_TILE, NSA_HPG * d), q_rows),
        out_shape=jax.ShapeDtypeStruct((b * seq, NSA_Q_COLS), F32),
        scratch_shapes=[pltpu.VMEM((nsel, Q_TILE), F32), pltpu.VMEM((SEL_GROUP, dk, NSA_ROWS), BF16),
                        pltpu.VMEM((ncp + 8, Q_TILE), F32), pltpu.VMEM((LANES, Q_TILE), F32)],
        compiler_params=_params(("parallel", "parallel", "arbitrary")),
        name="nsa",
    )(h, kc, vcT, ks, vsT, kw, vwT, h)


def _hgrn_chunk(q, f, v, lb, state_t, shift_ref, lower, gstart, dmat, off_mask, factored):
    c = HG_CHUNK
    log_f = jnp.log(lb + (1.0 - lb) * _sigmoid(f))
    kk = (1.0 - lb) * _sigmoid(-f)
    l1, l2, l3 = _split3(log_f)
    bt = _dot(lower, l1) + _dot(lower, l2) + _dot(lower, l3)
    bs = _dot(gstart, l1) + _dot(gstart, l2) + _dot(gstart, l3)

    qh = (q * jnp.exp(bt - bs)).astype(BF16)

    def same_block_direct():
        shift_ref[0, HG_SUB:, :] = kk
        shift_ref[1, HG_SUB:, :] = bt
        ones = jnp.ones((kk.shape[1], c), BF16)
        acc = jnp.zeros((c, c), F32)
        for d in range(HG_SUB):
            if d == 0:
                prod = q * kk
            else:
                rows = pl.ds(HG_SUB - d, c)
                prod = (q * shift_ref[0, rows, :]) * jnp.exp(bt - shift_ref[1, rows, :])
            band = _dot(prod.astype(BF16), ones)
            acc = jnp.where(dmat == d, band, acc)
        return acc

    def same_block_factored():
        return _dot_nt(qh, (kk * jnp.exp(bs - bt)).astype(BF16))

    att = same_block_factored() if factored else same_block_direct()
    att = jnp.where(dmat >= 0, att, 0.0)

    blocks = [jnp.zeros((HG_SUB, c), F32)]
    for i in range(1, c // HG_SUB):
        b_i = bs[i * HG_SUB:i * HG_SUB + 1, :]
        kh = kk * jnp.exp(jnp.minimum(b_i - bt, 0.0))
        blocks.append(_dot_nt(qh[i * HG_SUB:(i + 1) * HG_SUB], kh.astype(BF16)))
    att = jnp.where(off_mask, jnp.concatenate(blocks, axis=0), att)

    vb = v.astype(BF16)
    o = _dot_nt((q * jnp.exp(bt)).astype(BF16), state_t.astype(BF16)) + _dot(att.astype(BF16), vb)
    b_last = bt[c - 1:c, :]
    k_dec = kk * jnp.exp(b_last - bt)
    return o, state_t * jnp.exp(b_last) + _dot(v.T.astype(BF16), k_dec.astype(BF16))


def _hgrn_kernel(q_ref, f_ref, v_ref, gate_ref, lb_ref, ng_ref, o_ref, state_ref, shift_ref):
    c = HG_CHUNK

    @pl.when(pl.program_id(2) == 0)
    def _():
        state_ref[...] = jnp.zeros_like(state_ref)
        shift_ref[...] = jnp.zeros_like(shift_ref)

    row = lax.broadcasted_iota(jnp.int32, (c, c), 0)
    col = lax.broadcasted_iota(jnp.int32, (c, c), 1)
    sub = HG_SUB.bit_length() - 1
    row_blk = lax.shift_right_logical(row, sub)
    col_blk = lax.shift_right_logical(col, sub)
    lower = jnp.where(col <= row, 1.0, 0.0).astype(BF16)
    gstart = jnp.where(col_blk < row_blk, 1.0, 0.0).astype(BF16)
    dmat = jnp.where(row_blk == col_blk, row - col, -1)
    off_mask = col_blk < row_blk
    ng = ng_ref[...]

    def step(i, states, factored):
        rows = pl.ds(pl.multiple_of(i * c, c), c)
        new_states = []
        for hd in range(HG_HEADS_PER_STEP):
            lanes = slice(hd * HG_KEY_DIM, (hd + 1) * HG_KEY_DIM)
            o, state = _hgrn_chunk(q_ref[rows, lanes], f_ref[rows, lanes], v_ref[rows, lanes], lb_ref[hd],
                                   states[hd], shift_ref.at[hd], lower, gstart, dmat, off_mask, factored)
            gate = gate_ref[rows, lanes]
            ms = jnp.mean(o * o, axis=-1, keepdims=True)
            o_ref[rows, lanes] = o * lax.rsqrt(ms + RMS_EPS) * ng * (gate * _sigmoid(gate))
            new_states.append(state)
        return tuple(new_states)

    def run(factored):
        init = tuple(state_ref[hd] for hd in range(HG_HEADS_PER_STEP))
        states = lax.fori_loop(0, HG_ROWS // c, functools.partial(step, factored=factored), init)
        for hd, state in enumerate(states):
            state_ref[hd] = state

    can_factor = jnp.min(lb_ref[...]) > math.exp(-HG_FACTOR_LIMIT / HG_SUB)
    lax.cond(can_factor, functools.partial(run, True), functools.partial(run, False))


def _hgrn(h, lb, ng, bsz, seq, col0):
    nblk = seq // HG_ROWS
    hps = HG_HEADS_PER_STEP
    width = hps * HG_KEY_DIM
    tiles = HG_COLS // width
    assert (col0 * LANES) % width == 0 and seq % HG_ROWS == 0
    blk = lambda k: pl.BlockSpec((HG_ROWS, width),
                                 lambda i, j, c: (i * nblk + c, (col0 * LANES) // width + k * tiles + j))
    return pl.pallas_call(
        _hgrn_kernel,
        grid=(bsz, HG_HEADS // hps, nblk),
        in_specs=[blk(0), blk(1), blk(2), blk(3),
                  pl.BlockSpec((hps, 1, HG_KEY_DIM), lambda i, j, c: (j, 0, 0)),
                  pl.BlockSpec((1, HG_VAL_DIM), lambda i, j, c: (0, 0))],
        out_specs=pl.BlockSpec((HG_ROWS, width), lambda i, j, c: (i * nblk + c, j)),
        out_shape=jax.ShapeDtypeStruct((bsz * seq, HG_COLS), F32),
        scratch_shapes=[pltpu.VMEM((hps, HG_VAL_DIM, HG_KEY_DIM), F32),
                        pltpu.VMEM((hps, 2, HG_SUB + HG_CHUNK, HG_KEY_DIM), F32)],
        compiler_params=_params(("parallel", "parallel", "arbitrary")),
        name="hgrn",
    )(h, h, h, h, lb, ng)


def _mix_kernel(nsa_ref, hg_ref, x_ref, w_ref, ng_ref, g_ref, b_ref, o_ref):
    o_n = nsa_ref[...]
    half = o_n.shape[1]
    o_n = o_n * lax.rsqrt(jnp.mean(o_n * o_n, axis=-1, keepdims=True) + RMS_EPS) * ng_ref[...]
    mix = _dot(o_n.astype(BF16), w_ref[:half]) + _dot(hg_ref[...].astype(BF16), w_ref[half:])
    o_ref[...] = _layer_norm(DEEPNORM_ALPHA * x_ref[...] + mix, g_ref[...], b_ref[...])


def _mix(o_nsa, o_hg, x, w_out, ng, g, b, tm):
    t, d = x.shape
    half = o_nsa.shape[1]
    rows = lambda w: pl.BlockSpec((tm, w), lambda i: (i, 0))
    full = lambda r, c: pl.BlockSpec((r, c), lambda i: (0, 0))
    return pl.pallas_call(
        _mix_kernel,
        grid=(t // tm,),
        in_specs=[rows(half), rows(half), rows(d), full(d, d), full(1, half), full(1, d), full(1, d)],
        out_specs=rows(d),
        out_shape=jax.ShapeDtypeStruct((t, d), F32),
        compiler_params=_params(("parallel",)),
        name="mix",
    )(o_nsa, o_hg, x, w_out, ng, g, b)


def _xattn_kernel(x_ref, wq_ref, kT_ref, v_ref, wo_ref, g_ref, b_ref, o_ref):
    x = x_ref[0]
    d = x.shape[1]
    dh = d // XA_HEADS
    q = _dot(x.astype(BF16), wq_ref[...])
    heads = []
    for h in range(XA_HEADS):
        cols = slice(h * dh, (h + 1) * dh)
        s = _dot(q[:, cols].astype(BF16), kT_ref[0, cols, :]) * (dh ** -0.5)
        e = jnp.exp(s - jnp.max(s, axis=-1, keepdims=True))
        p = e * (1.0 / jnp.sum(e, axis=-1, keepdims=True))
        heads.append(_dot(p.astype(BF16), v_ref[0, :, cols]))
    o = jnp.concatenate(heads, axis=1)
    xa = _dot(o.astype(BF16), wo_ref[...])
    o_ref[0] = _layer_norm(DEEPNORM_ALPHA * x + xa, g_ref[...], b_ref[...])


def _xattn(x, wq, kT, v, wo, g, b, tm):
    bsz, seq, d = x.shape
    m = v.shape[1]
    full = lambda r, c: pl.BlockSpec((r, c), lambda i, j: (0, 0))
    return pl.pallas_call(
        _xattn_kernel,
        grid=(bsz, seq // tm),
        in_specs=[pl.BlockSpec((1, tm, d), lambda i, j: (i, j, 0)),
                  full(d, d),
                  pl.BlockSpec((1, d, m), lambda i, j: (i, 0, 0)),
                  pl.BlockSpec((1, m, d), lambda i, j: (i, 0, 0)),
                  full(d, d), full(1, d), full(1, d)],
        out_specs=pl.BlockSpec((1, tm, d), lambda i, j: (i, j, 0)),
        out_shape=jax.ShapeDtypeStruct((bsz, seq, d), F32),
        compiler_params=_params(("parallel", "parallel")),
        name="xattn",
    )(x, wq, kT, v, wo, g, b)


def _route_kernel(x_ref, wh_ref, wl_ref, bias_ref, ids_ref, wts_ref, rnk_ref, counts_ref, cnt_ref):
    xh, xl = _split2(x_ref[...])
    logits = _dot(xh, wh_ref[...]) + _dot(xh, wl_ref[...]) + _dot(xl, wh_ref[...]) + bias_ref[...]
    lane = lax.broadcasted_iota(jnp.int32, logits.shape, 1)
    lane_f = lane.astype(F32)
    big = float(LANES)

    is_g = lane < MOE_GROUPS
    g_max = jnp.max(jnp.where(is_g, logits, NEG_INF), axis=-1, keepdims=True)
    g_sum = jnp.sum(jnp.where(is_g, jnp.exp(logits - g_max), 0.0), axis=-1, keepdims=True)
    g_w = 1.0 / g_sum
    g_idx = jnp.min(jnp.where(is_g & (logits == g_max), lane_f, big), axis=-1, keepdims=True)

    e_lo = MOE_GROUPS + MOE_EXPERTS_PER_GROUP * g_idx
    is_e = (lane_f >= e_lo) & (lane_f < e_lo + MOE_EXPERTS_PER_GROUP)
    e_log = jnp.where(is_e, logits, NEG_INF)
    e_max = jnp.max(e_log, axis=-1, keepdims=True)
    e_exp = jnp.where(is_e, jnp.exp(logits - e_max), 0.0)
    e_sum = jnp.sum(e_exp, axis=-1, keepdims=True)
    i1 = jnp.min(jnp.where(is_e & (e_log == e_max), lane_f, big), axis=-1, keepdims=True)
    rest = jnp.where(lane_f == i1, NEG_INF, e_log)
    r_max = jnp.max(rest, axis=-1, keepdims=True)
    i2 = jnp.min(jnp.where(is_e & (lane_f != i1) & (rest == r_max), lane_f, big), axis=-1, keepdims=True)
    p1 = 1.0 / e_sum
    p2 = jnp.exp(r_max - e_max) / e_sum
    tot = p1 + p2
    w1 = g_w * (p1 / tot)
    w2 = g_w * (p2 / tot)
    ids = jnp.where(lane == 0, i1, i2) - float(MOE_GROUPS)
    ids_ref[...] = ids.astype(jnp.int32)
    wts_ref[...] = jnp.where(lane == 0, w1, w2)

    @pl.when(pl.program_id(0) == 0)
    def _():
        cnt_ref[...] = jnp.zeros_like(cnt_ref)

    tm = logits.shape[0]
    hit1 = lane_f == i1
    hit2 = lane_f == i2
    hits = jnp.where(hit1 | hit2, 1.0, 0.0)
    earlier = lax.broadcasted_iota(jnp.int32, (tm, tm), 1) < lax.broadcasted_iota(jnp.int32, (tm, tm), 0)
    before = _dot(jnp.where(earlier, 1.0, 0.0).astype(BF16), hits.astype(BF16)) + cnt_ref[...]
    rank1 = jnp.sum(jnp.where(hit1, before, 0.0), axis=-1, keepdims=True)
    rank2 = jnp.sum(jnp.where(hit2, before, 0.0), axis=-1, keepdims=True)
    rnk_ref[...] = jnp.where(lane == 0, rank1, rank2).astype(jnp.int32)
    cnt_ref[...] = cnt_ref[...] + jnp.sum(hits, axis=0, keepdims=True)
    counts_ref[...] = cnt_ref[...].astype(jnp.int32)


def _route(x, wh, wl, bias, tm):
    t, d = x.shape
    rows = lambda w: pl.BlockSpec((tm, w), lambda i: (i, 0))
    full = lambda r, c: pl.BlockSpec((r, c), lambda i: (0, 0))
    i32 = jnp.int32
    return pl.pallas_call(
        _route_kernel,
        grid=(t // tm,),
        in_specs=[rows(d), full(d, LANES), full(d, LANES), full(1, LANES)],
        out_specs=[rows(LANES), rows(LANES), rows(LANES), full(1, LANES)],
        out_shape=[jax.ShapeDtypeStruct((t, LANES), i32), jax.ShapeDtypeStruct((t, LANES), F32),
                   jax.ShapeDtypeStruct((t, LANES), i32), jax.ShapeDtypeStruct((1, LANES), i32)],
        scratch_shapes=[pltpu.VMEM((1, LANES), F32)],
        compiler_params=_params(("arbitrary",)),
        name="route",
    )(x, wh, wl, bias)


def _row_copy_wait(src, dst, sem, rows):
    pltpu.make_async_copy(src.at[pl.ds(0, rows)], dst.at[pl.ds(0, rows)], sem).wait()


def _dispatch_kernel(dest_ref, zfill_ref, x_ref, xs_hbm, zbuf, zsem, sem):
    i = pl.program_id(0)
    tm = x_ref.shape[0]

    @pl.when(i == 0)
    def _():
        zbuf[...] = jnp.zeros_like(zbuf)
        tail = lambda e: xs_hbm.at[pl.ds(pl.multiple_of(zfill_ref[e], MOE_BLOCK), MOE_BLOCK)]
        for e in range(zfill_ref.shape[0]):
            @pl.when(zfill_ref[e] >= 0)
            def _():
                pltpu.make_async_copy(zbuf, tail(e), zsem).start()
        for e in range(zfill_ref.shape[0]):
            @pl.when(zfill_ref[e] >= 0)
            def _():
                pltpu.make_async_copy(zbuf, tail(e), zsem).wait()

    def send(j, c):
        for k in range(MOE_TOP_K):
            slot = dest_ref[(i * tm + j) * MOE_TOP_K + k]
            pltpu.make_async_copy(x_ref.at[pl.ds(j, 1)], xs_hbm.at[pl.ds(slot, 1)], sem).start()
        return c

    lax.fori_loop(0, tm, send, 0, unroll=8)
    for _ in range(MOE_TOP_K):
        _row_copy_wait(x_ref, xs_hbm, sem, tm)


def _dispatch(dest, zfill, x, cap, tm):
    t, d = x.shape
    grid_spec = pltpu.PrefetchScalarGridSpec(
        num_scalar_prefetch=2,
        grid=(t // tm,),
        in_specs=[pl.BlockSpec((tm, d), lambda i, dest, zfill: (i, 0))],
        out_specs=pl.BlockSpec(memory_space=pl.ANY),
        scratch_shapes=[pltpu.VMEM((MOE_BLOCK, d), F32), pltpu.SemaphoreType.DMA, pltpu.SemaphoreType.DMA])
    return pl.pallas_call(
        _dispatch_kernel,
        grid_spec=grid_spec,
        out_shape=jax.ShapeDtypeStruct((cap, d), F32),
        compiler_params=_params(("arbitrary",)),
        name="dispatch",
    )(dest, zfill, x)


def _experts_kernel(bexp_ref, nact_ref, xs_ref, wg_ref, wu_ref, wd_ref, y_ref):
    @pl.when(pl.program_id(0) < nact_ref[0])
    def _():
        xb = xs_ref[...].astype(BF16)
        gate = _dot(xb, wg_ref[0])
        hid = (gate * _sigmoid(gate)) * _dot(xb, wu_ref[0])
        y_ref[...] = _dot(hid.astype(BF16), wd_ref[0])

    @pl.when(pl.program_id(0) >= nact_ref[0])
    def _():
        y_ref[...] = jnp.zeros_like(y_ref)


def _experts(bexp, nact, xs, wg, wu, wd):
    cap, d = xs.shape
    dff = wg.shape[-1]
    by_expert = lambda i, bexp, nact: (bexp[i], 0, 0)
    active = lambda i, bexp, nact: (jnp.minimum(i, nact[0] - 1), 0)
    grid_spec = pltpu.PrefetchScalarGridSpec(
        num_scalar_prefetch=2,
        grid=(cap // MOE_BLOCK,),
        in_specs=[pl.BlockSpec((MOE_BLOCK, d), active),
                  pl.BlockSpec((1, d, dff), by_expert),
                  pl.BlockSpec((1, d, dff), by_expert),
                  pl.BlockSpec((1, dff, d), by_expert)],
        out_specs=pl.BlockSpec((MOE_BLOCK, d), lambda i, bexp, nact: (i, 0)))
    return pl.pallas_call(
        _experts_kernel,
        grid_spec=grid_spec,
        out_shape=jax.ShapeDtypeStruct((cap, d), F32),
        compiler_params=_params(("arbitrary",)),
        name="experts",
    )(bexp, nact, xs, wg, wu, wd)


def _combine_kernel(dest_ref, y_hbm, x_ref, wts_ref, g_ref, b_ref, o_ref, ybuf, sem):
    i = pl.program_id(0)
    n = pl.num_programs(0)
    tm = x_ref.shape[0]
    cur = lax.rem(i, 2)

    def fetch(tile, buf):
        def one(j, c):
            for k in range(MOE_TOP_K):
                slot = dest_ref[(tile * tm + j) * MOE_TOP_K + k]
                pltpu.make_async_copy(y_hbm.at[pl.ds(slot, 1)], ybuf.at[buf, k, pl.ds(j, 1)], sem.at[buf]).start()
            return c
        lax.fori_loop(0, tm, one, 0, unroll=8)

    @pl.when(i == 0)
    def _():
        fetch(0, 0)

    @pl.when(i + 1 < n)
    def _():
        fetch(i + 1, 1 - cur)

    for k in range(MOE_TOP_K):
        _row_copy_wait(y_hbm, ybuf.at[cur, k], sem.at[cur], tm)
    w = wts_ref[...]
    ff = w[:, 0:1] * ybuf[cur, 0]
    for k in range(1, MOE_TOP_K):
        ff = ff + w[:, k:k + 1] * ybuf[cur, k]
    o_ref[...] = _layer_norm(DEEPNORM_ALPHA * x_ref[...] + ff, g_ref[...], b_ref[...])


def _combine(dest, y, x, wts, g, b, tm):
    t, d = x.shape
    rows = lambda w: pl.BlockSpec((tm, w), lambda i, dest: (i, 0))
    full = lambda r, c: pl.BlockSpec((r, c), lambda i, dest: (0, 0))
    grid_spec = pltpu.PrefetchScalarGridSpec(
        num_scalar_prefetch=1,
        grid=(t // tm,),
        in_specs=[pl.BlockSpec(memory_space=pl.ANY), rows(d), rows(LANES), full(1, d), full(1, d)],
        out_specs=rows(d),
        scratch_shapes=[pltpu.VMEM((2, MOE_TOP_K, tm, d), F32), pltpu.SemaphoreType.DMA((2,))])
    return pl.pallas_call(
        _combine_kernel,
        grid_spec=grid_spec,
        out_shape=jax.ShapeDtypeStruct((t, d), F32),
        compiler_params=_params(("arbitrary",)),
        name="combine",
    )(dest, y, x, wts, g, b)


def _dispatch_plan(ids, rnk, counts, n_tok):
    padded = ((counts + MOE_BLOCK - 1) // MOE_BLOCK) * MOE_BLOCK
    pend = jnp.cumsum(padded)
    pstart = pend - padded
    dest = (pstart[ids] + rnk).reshape(-1).astype(jnp.int32)
    cap = n_tok * MOE_TOP_K + MOE_N_EXPERTS * MOE_BLOCK
    n_blocks = cap // MOE_BLOCK
    block_start = jnp.arange(n_blocks, dtype=jnp.int32) * MOE_BLOCK
    bexp = jnp.minimum(jnp.sum(block_start[:, None] >= pend[None, :], axis=1), MOE_N_EXPERTS - 1).astype(jnp.int32)
    nact = (pend[-1] // MOE_BLOCK).astype(jnp.int32).reshape(1)
    tails = jnp.where(padded > 0, pend - MOE_BLOCK, -1)
    idle = pend[-1] + block_start[:MOE_N_EXPERTS]
    zfill = jnp.concatenate([tails, jnp.where(idle < cap, idle, -1)]).astype(jnp.int32)
    return dest, bexp, nact, zfill, cap


def kernel(x, mem, w_in, cmp_pe_k, cmp_pe_v, cmp_w1_k, cmp_w2_k, cmp_w1_v, cmp_w2_v, nsa_norm_g,
           hg_lb_logits, hg_norm_g, w_out, ln1_g, ln1_b, xa_wq, xa_wk, xa_wv, xa_wo, ln2_g, ln2_b,
           moe_w_group, moe_b_group, moe_w_expert, moe_b_expert, moe_w_gate, moe_w_up, moe_w_down,
           ln3_g, ln3_b):
    b, s, d = x.shape
    t = b * s
    g, dh = NSA_KV_GROUPS, NSA_HEAD_DIM
    nch = s // CMP_STRIDE
    row = lambda a: a.reshape(1, -1)
    lb_all = jnp.cumsum(jax.nn.softmax(hg_lb_logits.astype(F32), axis=0), axis=0)
    xt = x.reshape(t, d)
    for l in range(DEPTH):
        nsa_cols = NSA_Q_COLS + 6 * NSA_KV_COLS
        w_l = w_in[l]
        w_perm = jnp.concatenate(
            [w_l[:, :NSA_Q_COLS], w_l[:, nsa_cols + NSA_GATE_COLS:], w_l[:, NSA_Q_COLS:nsa_cols],
             w_l[:, nsa_cols:nsa_cols + NSA_GATE_COLS],
             jnp.zeros((d, LANES - NSA_GATE_COLS), w_l.dtype)], axis=1)
        h, ks, kw, vsT, vwT = _inproj(xt, w_perm.astype(BF16), b, s, 256)
        nkc = h[:, NSA_KV0:NSA_KV0 + NSA_KV_COLS]
        nvc = h[:, NSA_KV0 + NSA_KV_COLS:NSA_KV0 + 2 * NSA_KV_COLS]
        hg_col0 = NSA_Q_COLS // LANES
        gate_tile = (nsa_cols + 4 * HG_COLS) // LANES

        chunks = lambda a: jnp.transpose(a.reshape(b, nch, CMP_STRIDE, g, dh), (0, 3, 1, 2, 4)).reshape(
            b, g, nch, CMP_STRIDE * dh)
        comp = _compress(jnp.stack([chunks(nkc), chunks(nvc)]),
                         jnp.stack([cmp_pe_k[l].reshape(1, -1), cmp_pe_v[l].reshape(1, -1)]),
                         jnp.stack([cmp_w1_k[l], cmp_w1_v[l]]),
                         jnp.stack([cmp_w2_k[l], cmp_w2_v[l]]))
        kc = comp[0].astype(BF16)
        vcT = jnp.swapaxes(comp[1], -1, -2).astype(BF16)

        o_nsa = _nsa(h, kc, vcT, ks, vsT, kw, vwT, s, gate_tile)

        o_hg = _hgrn(h, lb_all[l].reshape(HG_HEADS, 1, HG_KEY_DIM), row(hg_norm_g[l]), b, s, hg_col0)

        x1 = _mix(o_nsa, o_hg, xt, w_out[l].astype(BF16), row(nsa_norm_g[l]), row(ln1_g[l]), row(ln1_b[l]), 512)

        n_mem = mem.shape[1]
        kv = _matmul(mem.reshape(b * n_mem, d),
                     jnp.concatenate([xa_wk[l], xa_wv[l]], axis=1).astype(BF16), n_mem)
        kT = jnp.swapaxes(kv[:, :d].reshape(b, n_mem, d), 1, 2).astype(BF16)
        v = kv[:, d:].reshape(b, n_mem, d).astype(BF16)
        x2 = _xattn(x1.reshape(b, s, d), xa_wq[l].astype(BF16), kT, v, xa_wo[l].astype(BF16),
                    row(ln2_g[l]), row(ln2_b[l]), 512).reshape(t, d)

        w_r = jnp.concatenate([moe_w_group[l], moe_w_expert[l]], axis=1)
        w_r = jnp.pad(w_r, ((0, 0), (0, LANES - w_r.shape[1])))
        b_r = jnp.pad(jnp.concatenate([moe_b_group[l], moe_b_expert[l]]), (0, LANES - MOE_GROUPS - MOE_N_EXPERTS))
        w_rh, w_rl = _split2(w_r)
        ids, wts, rnk, counts = _route(x2, w_rh, w_rl, row(b_r), 512)
        dest, bexp, nact, zfill, cap = _dispatch_plan(
            ids[:, :MOE_TOP_K], rnk[:, :MOE_TOP_K], counts[0, MOE_GROUPS:MOE_GROUPS + MOE_N_EXPERTS], t)
        xs = _dispatch(dest, zfill, x2, cap, 256)
        ys = _experts(bexp, nact, xs, moe_w_gate[l].astype(BF16), moe_w_up[l].astype(BF16),
                      moe_w_down[l].astype(BF16))
        xt = _combine(dest, ys, x2, wts, row(ln3_g[l]), row(ln3_b[l]), 256)
    return xt.reshape(b, s, d)
```

```python
import functools
import math

import jax
import jax.numpy as jnp
from jax import lax
from jax.experimental import pallas as pl
from jax.experimental.pallas import tpu as pltpu

F32 = jnp.float32
BF16 = jnp.bfloat16

NSA_HEAD_DIM = 64
NSA_HEADS = 8
NSA_KV_GROUPS = 2
NSA_HPG = NSA_HEADS // NSA_KV_GROUPS
CMP_BLOCK = 32
CMP_STRIDE = 16
CMP_HIDDEN = 256
SEL_BLOCK = 64
SEL_TOP_N = 16
WINDOW = 512
FORCE_BONUS = 1.0e4
HG_KEY_DIM = 128
HG_VAL_DIM = 128
HG_HEADS = 4
XA_HEADS = 4
MOE_GROUPS = 4
MOE_EXPERTS_PER_GROUP = 8
MOE_N_EXPERTS = MOE_GROUPS * MOE_EXPERTS_PER_GROUP
MOE_TOP_K = 2
DEPTH = 1
DEEPNORM_ALPHA = (2.0 * DEPTH) ** 0.25
LN_EPS = 1e-5
RMS_EPS = 1e-6
NEG_INF = -1e30

NSA_Q_COLS = NSA_HEADS * NSA_HEAD_DIM
NSA_KV_COLS = NSA_KV_GROUPS * NSA_HEAD_DIM
NSA_GATE_COLS = NSA_HEADS * 3
HG_COLS = HG_HEADS * HG_KEY_DIM
NSA_KV0 = NSA_Q_COLS + 4 * HG_COLS

LANES = 128
Q_TILE = 128
NSA_ROWS = NSA_HPG * Q_TILE
SEL_KV_TILE = 512
SEL_GROUP = 6
SEL_SPLIT = 2
NSA_PARTS = 4
NSA_V_ROWS = NSA_HEAD_DIM + 16
LOG2_E = 1.4426950408889634
HG_CHUNK = 128
HG_SUB = 16
HG_ROWS = 512
HG_HEADS_PER_STEP = 4
HG_FACTOR_LIMIT = 60.0
MOE_BLOCK = 256
VMEM_LIMIT = 48 * 1024 * 1024


def _dot(a, b):
    return jnp.dot(a, b, preferred_element_type=F32)


def _dot_nt(a, b):
    return lax.dot_general(a, b, (((1,), (1,)), ((), ())), preferred_element_type=F32)


def _split2(a):
    hi = a.astype(BF16)
    lo = (a - hi.astype(F32)).astype(BF16)
    return hi, lo


def _split3(a):
    p1 = a.astype(BF16)
    r1 = a - p1.astype(F32)
    p2 = r1.astype(BF16)
    p3 = (r1 - p2.astype(F32)).astype(BF16)
    return p1, p2, p3


def _dot3(a, b):
    ah, al = _split2(a)
    bh, bl = _split2(b)
    return _dot(ah, bh) + _dot(ah, bl) + _dot(al, bh)


def _sigmoid(x):
    return 1.0 / (1.0 + jnp.exp(-x))


def _layer_norm(y, g, b):
    mu = jnp.mean(y, axis=-1, keepdims=True)
    d = y - mu
    var = jnp.mean(d * d, axis=-1, keepdims=True)
    return d * lax.rsqrt(var + LN_EPS) * g + b


def _params(sem):
    return pltpu.CompilerParams(dimension_semantics=sem, vmem_limit_bytes=VMEM_LIMIT)


def _matmul_kernel(x_ref, w_ref, o_ref):
    o_ref[...] = _dot(x_ref[...].astype(BF16), w_ref[...]).astype(o_ref.dtype)


def _matmul(x, w, tm):
    m, k = x.shape
    n = w.shape[1]
    return pl.pallas_call(
        _matmul_kernel,
        grid=(m // tm,),
        in_specs=[pl.BlockSpec((tm, k), lambda i: (i, 0)),
                  pl.BlockSpec((k, n), lambda i: (0, 0))],
        out_specs=pl.BlockSpec((tm, n), lambda i: (i, 0)),
        out_shape=jax.ShapeDtypeStruct((m, n), F32),
        compiler_params=_params(("parallel",)),
        name="proj",
    )(x, w)


def _inproj_kernel(x_ref, w_ref, h_ref, ks_ref, kw_ref, vsT_ref, vwT_ref, *, tiles_per_seq):
    h = _dot(x_ref[...].astype(BF16), w_ref[...])
    h_ref[...] = h
    tm = h.shape[0]
    dh = NSA_HEAD_DIM
    k_sel0 = NSA_KV0 + 2 * NSA_KV_COLS
    v_sel0, k_win0, v_win0 = k_sel0 + NSA_KV_COLS, k_sel0 + 2 * NSA_KV_COLS, k_sel0 + 3 * NSA_KV_COLS
    pos = lax.rem(pl.program_id(0), tiles_per_seq) * tm + lax.broadcasted_iota(jnp.int32, (tm, 1), 0)
    blk = lax.shift_right_logical(pos & (SEL_KV_TILE - 1), SEL_BLOCK.bit_length() - 1)
    onehot = jnp.where(blk == lax.broadcasted_iota(jnp.int32, (1, LANES - dh), 1), 1.0, 0.0)
    ones_row = jnp.where(lax.broadcasted_iota(jnp.int32, (NSA_V_ROWS - dh, tm), 0) == 0, 1.0, 0.0)
    vsT = h[:, v_sel0:v_sel0 + NSA_KV_COLS].T
    vwT = h[:, v_win0:v_win0 + NSA_KV_COLS].T
    for g in range(NSA_KV_GROUPS):
        cols = slice(g * dh, (g + 1) * dh)
        ks_ref[0, g] = jnp.concatenate([h[:, k_sel0 + g * dh:k_sel0 + (g + 1) * dh], onehot], axis=1).astype(BF16)
        kw_ref[0, g] = h[:, k_win0 + g * dh:k_win0 + (g + 1) * dh].astype(BF16)
        vsT_ref[0, g] = jnp.concatenate([vsT[cols], ones_row], axis=0).astype(BF16)
        vwT_ref[0, g] = jnp.concatenate([vwT[cols], ones_row], axis=0).astype(BF16)


def _inproj(x, w, bsz, seq, tm):
    t, k = x.shape
    n = w.shape[1]
    g, dh = NSA_KV_GROUPS, NSA_HEAD_DIM
    tps = seq // tm
    keys = lambda width: pl.BlockSpec((1, g, tm, width), lambda i: (i // tps, 0, i % tps, 0))
    vals = pl.BlockSpec((1, g, NSA_V_ROWS, tm), lambda i: (i // tps, 0, 0, i % tps))
    return pl.pallas_call(
        functools.partial(_inproj_kernel, tiles_per_seq=tps),
        grid=(t // tm,),
        in_specs=[pl.BlockSpec((tm, k), lambda i: (i, 0)),
                  pl.BlockSpec((k, n), lambda i: (0, 0))],
        out_specs=[pl.BlockSpec((tm, n), lambda i: (i, 0)), keys(LANES), keys(dh), vals, vals],
        out_shape=[jax.ShapeDtypeStruct((t, n), F32),
                   jax.ShapeDtypeStruct((bsz, g, seq, LANES), BF16),
                   jax.ShapeDtypeStruct((bsz, g, seq, dh), BF16),
                   jax.ShapeDtypeStruct((bsz, g, NSA_V_ROWS, seq), BF16),
                   jax.ShapeDtypeStruct((bsz, g, NSA_V_ROWS, seq), BF16)],
        compiler_params=_params(("parallel",)),
        name="inproj",
    )(x, w)


def _compress_kernel(ch_ref, pe_ref, w1_ref, w2_ref, o_ref):
    ch = ch_ref[0, 0, 0]
    half = ch.shape[1]
    nch = ch.shape[0]
    pe = pe_ref[0]
    w1 = w1_ref[0]
    top = _dot3(ch + pe[:, :half], w1[:half])
    bot = _dot3(ch + pe[:, half:], w1[half:])
    hid = top + pltpu.roll(bot, nch - 1, 0)
    c = 0.7978845608028654
    act = 0.5 * hid * (1.0 + jnp.tanh(c * (hid + 0.044715 * hid * hid * hid)))
    out = _dot3(act, w2_ref[0])
    row = lax.broadcasted_iota(jnp.int32, out.shape, 0)
    o_ref[0, 0, 0] = jnp.where(row < nch - 1, out, 0.0)


def _compress(ch, pe, w1, w2):
    _, b, g, nch, width = ch.shape
    hidden = w1.shape[-1]
    d = w2.shape[-1]
    return pl.pallas_call(
        _compress_kernel,
        grid=(2, b, g),
        in_specs=[pl.BlockSpec((1, 1, 1, nch, width), lambda a, i, j: (a, i, j, 0, 0)),
                  pl.BlockSpec((1, 1, 2 * width), lambda a, i, j: (a, 0, 0)),
                  pl.BlockSpec((1, 2 * width, hidden), lambda a, i, j: (a, 0, 0)),
                  pl.BlockSpec((1, hidden, d), lambda a, i, j: (a, 0, 0))],
        out_specs=pl.BlockSpec((1, 1, 1, nch, d), lambda a, i, j: (a, i, j, 0, 0)),
        out_shape=jax.ShapeDtypeStruct((2, b, g, nch, d), F32),
        compiler_params=_params(("parallel", "parallel", "parallel")),
        name="compress",
    )(ch, pe, w1, w2)


def _tile_heads(a):
    return jnp.concatenate([a] * NSA_HPG, axis=1)


def _nsa_kernel(q_ref, kc_ref, vcT_ref, ks_ref, vsT_ref, kw_ref, vwT_ref, gate_ref,
                o_ref, sel_ref, qx_ref, psum_ref, gt_ref, *, seq):
    s0 = pl.program_id(2) * Q_TILE
    nsel = seq // SEL_BLOCK
    ncmp_pad = seq // CMP_STRIDE
    dh = NSA_HEAD_DIM

    q_t = q_ref[...].T
    q = jnp.concatenate([q_t[r * dh:(r + 1) * dh] for r in range(NSA_HPG)], axis=1)
    q_hi = (q * (dh ** -0.5 * LOG2_E)).astype(BF16)
    t_q = s0 + lax.broadcasted_iota(jnp.int32, (1, Q_TILE), 1)
    t_all = _tile_heads(t_q)

    def window_branch():
        span = WINDOW + Q_TILE
        lo = pl.multiple_of(jnp.maximum(s0 - WINDOW, 0), Q_TILE)
        kpos = lo + lax.broadcasted_iota(jnp.int32, (span, 1), 0)
        sw = _dot(kw_ref[0, 0, pl.ds(lo, span), :], q_hi)
        sw = sw + _tile_heads(jnp.where((kpos <= t_q) & (kpos > t_q - WINDOW), 0.0, NEG_INF))
        e_w = jnp.exp2(sw - jnp.max(sw, axis=0, keepdims=True)).astype(BF16)
        acc_w = _dot(vwT_ref[0, 0, :, pl.ds(lo, span)], e_w)
        return acc_w[0:dh] * (1.0 / acc_w[dh:dh + 1])

    def compressed_and_select(parts):
        ncmp = parts * (ncmp_pad // NSA_PARTS)
        rows = parts * (nsel // NSA_PARTS)
        o_w = window_branch()
        sc = _dot(kc_ref[0, 0, 0:ncmp, :], q_hi)
        n_end = lax.broadcasted_iota(jnp.int32, (ncmp, 1), 0) * CMP_STRIDE + (CMP_BLOCK - 1)
        sc = sc + _tile_heads(jnp.where(n_end <= t_q, 0.0, NEG_INF))
        e_c = jnp.exp2(sc - jnp.max(sc, axis=0, keepdims=True))
        l_c = jnp.sum(e_c, axis=0, keepdims=True)
        p_c = e_c * jnp.where(t_all >= CMP_BLOCK - 1, 1.0 / l_c, 0.0)
        o_c = _dot(vcT_ref[0, 0, :, 0:ncmp], p_c.astype(BF16))

        p_sum = p_c[:, 0:Q_TILE]
        for r in range(1, NSA_HPG):
            p_sum = p_sum + p_c[:, r * Q_TILE:(r + 1) * Q_TILE]
        ratio = SEL_BLOCK // CMP_STRIDE
        first_row = 8 - (CMP_BLOCK // CMP_STRIDE - 1)
        psum_ref[0:8, :] = jnp.zeros((8, Q_TILE), F32)
        psum_ref[8:8 + ncmp, :] = p_sum
        imp = psum_ref[pl.ds(first_row, rows, stride=ratio), :]
        for m in range(1, ratio + CMP_BLOCK // CMP_STRIDE - 1):
            imp = imp + psum_ref[pl.ds(first_row + m, rows, stride=ratio), :]
        j_idx = lax.broadcasted_iota(jnp.int32, (rows, Q_TILE), 0)
        cur = lax.shift_right_logical(t_q, SEL_BLOCK.bit_length() - 1)
        forced = (j_idx == 0) | (j_idx == cur) | (j_idx == cur - 1)
        score = jnp.where(j_idx <= cur, imp + jnp.where(forced, FORCE_BONUS, 0.0), -1.0)
        j_f = j_idx.astype(F32)
        n_forced = 3 if (parts >= 2 and seq // NSA_PARTS >= Q_TILE + 2 * SEL_BLOCK and rows >= SEL_TOP_N) else 0
        sel = jnp.zeros((rows, Q_TILE), F32)
        if n_forced:
            sel = jnp.where(forced, 1.0, sel)
            score = jnp.where(forced, -3.0e38, score)
        for _ in range(min(SEL_TOP_N, rows) - n_forced):
            best = jnp.max(score, axis=0, keepdims=True)
            first = jnp.min(jnp.where(score == best, j_f, float(rows)), axis=0, keepdims=True)
            pick = j_f == first
            sel = jnp.where(pick, 1.0, sel)
            score = jnp.where(pick, -3.0e38, score)
        sel_ref[0:rows, :] = sel
        if parts < NSA_PARTS:
            sel_ref[rows:, :] = jnp.zeros((nsel - rows, Q_TILE), F32)
        return o_c, o_w

    part_len = seq // NSA_PARTS
    parts_needed = lax.div(s0 + (Q_TILE + part_len - 1), part_len)
    o_c, o_w = lax.switch(parts_needed - 1,
                     [functools.partial(compressed_and_select, n) for n in range(1, NSA_PARTS + 1)])

    blocks_per_tile = SEL_KV_TILE // SEL_BLOCK
    for slot in range(SEL_GROUP):
        qx_ref[slot, 0:dh, :] = q_hi
        qx_ref[slot, dh:, :] = jnp.zeros((qx_ref.shape[1] - dh, NSA_ROWS), BF16)
    piece = SEL_KV_TILE // SEL_SPLIT
    k_off = lax.broadcasted_iota(jnp.int32, (piece, 1), 0)

    def sel_scores(j, slot, causal):
        member = sel_ref[pl.ds(pl.multiple_of(j * blocks_per_tile, blocks_per_tile), blocks_per_tile), :]
        bias = _tile_heads(jnp.where(member > 0.5, 0.0, NEG_INF))
        qx_ref[slot, dh:dh + 2 * blocks_per_tile, :] = jnp.concatenate(
            [bias, jnp.zeros_like(bias)], axis=0).astype(BF16)
        parts = []
        for u in range(SEL_SPLIT):
            base = pl.multiple_of(j * SEL_KV_TILE + u * piece, piece)
            s = _dot(ks_ref[0, 0, pl.ds(base, piece), :], qx_ref[slot])
            if causal:
                s = s + _tile_heads(jnp.where((base + k_off) <= t_q, 0.0, NEG_INF))
            parts.append((base, s))
        return parts

    def sel_update(base, s, m_i, acc):
        m_new = jnp.maximum(m_i, jnp.max(s, axis=0, keepdims=True))
        p = jnp.exp2(s - m_new).astype(BF16)
        pv = _dot(vsT_ref[0, 0, :, pl.ds(base, piece)], p)
        return m_new, jnp.exp2(m_i - m_new) * acc + pv

    def sel_group(first_tile, carry, n_tiles, causal_last):
        scores = []
        for u in range(n_tiles):
            scores += sel_scores(first_tile + u, u, causal_last and u == n_tiles - 1)
        for base, s in scores:
            carry = sel_update(base, s, *carry)
        return carry

    n_tiles = lax.div(s0 + (Q_TILE + SEL_KV_TILE - 1), SEL_KV_TILE)
    n_main = lax.div(n_tiles - 1, SEL_GROUP)
    init = (jnp.full((1, NSA_ROWS), NEG_INF, F32), jnp.zeros((vsT_ref.shape[2], NSA_ROWS), F32))
    carry = lax.fori_loop(0, n_main, lambda i, c: sel_group(SEL_GROUP * i, c, SEL_GROUP, False), init)
    _, acc_s = lax.switch(
        n_tiles - SEL_GROUP * n_main - 1,
        [functools.partial(sel_group, n_tiles=n, causal_last=True) for n in range(1, SEL_GROUP + 1)],
        SEL_GROUP * n_main, carry)
    o_s = acc_s[0:dh] * (1.0 / acc_s[dh:dh + 1])

    gt_ref[...] = gate_ref[...].T
    g_row0 = pl.program_id(1) * (NSA_HPG * 3)

    def gate(branch):
        return _sigmoid(jnp.concatenate(
            [gt_ref[pl.ds(g_row0 + 3 * r + branch, 1), :] for r in range(NSA_HPG)], axis=1))

    o = gate(0) * o_c + gate(1) * o_s + gate(2) * o_w
    o_rows = jnp.concatenate([o[:, r * Q_TILE:(r + 1) * Q_TILE] for r in range(NSA_HPG)], axis=0)
    o_ref[...] = o_rows.T


def _nsa(h, kc, vcT, ks, vsT, kw, vwT, seq, gate_tile):
    b, g = ks.shape[:2]
    nqb = seq // Q_TILE
    d = NSA_HEAD_DIM
    ncp = seq // CMP_STRIDE
    nsel = seq // SEL_BLOCK
    dk = ks.shape[-1]
    dv = vsT.shape[2]
    per_bg = lambda i, j, k: (i, j, 0, 0)
    q_rows = lambda i, j, k: (i * nqb + k, j)
    return pl.pallas_call(
        functools.partial(_nsa_kernel, seq=seq),
        grid=(b, g, nqb),
        in_specs=[pl.BlockSpec((Q_TILE, NSA_HPG * d), q_rows),
                  pl.BlockSpec((1, 1, ncp, d), per_bg),
                  pl.BlockSpec((1, 1, d, ncp), per_bg),
                  pl.BlockSpec((1, 1, seq, dk), per_bg),
                  pl.BlockSpec((1, 1, dv, seq), per_bg),
                  pl.BlockSpec((1, 1, seq, d), per_bg),
                  pl.BlockSpec((1, 1, dv, seq), per_bg),
                  pl.BlockSpec((Q_TILE, LANES), lambda i, j, k: (i * nqb + k, gate_tile))],
        out_specs=pl.BlockSpec((Q_TILE, NSA_HPG * d), q_rows),
        out_shape=jax.ShapeDtypeStruct((b * seq, NSA_Q_COLS), F32),
        scratch_shapes=[pltpu.VMEM((nsel, Q_TILE), F32), pltpu.VMEM((SEL_GROUP, dk, NSA_ROWS), BF16),
                        pltpu.VMEM((ncp + 8, Q_TILE), F32), pltpu.VMEM((LANES, Q_TILE), F32)],
        compiler_params=_params(("parallel", "parallel", "arbitrary")),
        name="nsa",
    )(h, kc, vcT, ks, vsT, kw, vwT, h)


def _hgrn_chunk(q, f, v, lb, state_t, shift_ref, lower, gstart, dmat, off_mask, factored):
    c = HG_CHUNK
    log_f = jnp.log(lb + (1.0 - lb) * _sigmoid(f))
    kk = (1.0 - lb) * _sigmoid(-f)
    l1, l2, l3 = _split3(log_f)
    bt = _dot(lower, l1) + _dot(lower, l2) + _dot(lower, l3)
    bs = _dot(gstart, l1) + _dot(gstart, l2) + _dot(gstart, l3)

    qh = (q * jnp.exp(bt - bs)).astype(BF16)

    def same_block_direct():
        shift_ref[0, HG_SUB:, :] = kk
        shift_ref[1, HG_SUB:, :] = bt
        ones = jnp.ones((kk.shape[1], c), BF16)
        acc = jnp.zeros((c, c), F32)
        for d in range(HG_SUB):
            if d == 0:
                prod = q * kk
            else:
                rows = pl.ds(HG_SUB - d, c)
                prod = (q * shift_ref[0, rows, :]) * jnp.exp(bt - shift_ref[1, rows, :])
            band = _dot(prod.astype(BF16), ones)
            acc = jnp.where(dmat == d, band, acc)
        return acc

    def same_block_factored():
        return _dot_nt(qh, (kk * jnp.exp(bs - bt)).astype(BF16))

    att = same_block_factored() if factored else same_block_direct()
    att = jnp.where(dmat >= 0, att, 0.0)

    blocks = [jnp.zeros((HG_SUB, c), F32)]
    for i in range(1, c // HG_SUB):
        b_i = bs[i * HG_SUB:i * HG_SUB + 1, :]
        kh = kk * jnp.exp(jnp.minimum(b_i - bt, 0.0))
        blocks.append(_dot_nt(qh[i * HG_SUB:(i + 1) * HG_SUB], kh.astype(BF16)))
    att = jnp.where(off_mask, jnp.concatenate(blocks, axis=0), att)

    vb = v.astype(BF16)
    o = _dot_nt((q * jnp.exp(bt)).astype(BF16), state_t.astype(BF16)) + _dot(att.astype(BF16), vb)
    b_last = bt[c - 1:c, :]
    k_dec = kk * jnp.exp(b_last - bt)
    return o, state_t * jnp.exp(b_last) + _dot(v.T.astype(BF16), k_dec.astype(BF16))


def _hgrn_kernel(q_ref, f_ref, v_ref, gate_ref, lb_ref, ng_ref, o_ref, state_ref, shift_ref):
    c = HG_CHUNK

    @pl.when(pl.program_id(2) == 0)
    def _():
        state_ref[...] = jnp.zeros_like(state_ref)
        shift_ref[...] = jnp.zeros_like(shift_ref)

    row = lax.broadcasted_iota(jnp.int32, (c, c), 0)
    col = lax.broadcasted_iota(jnp.int32, (c, c), 1)
    sub = HG_SUB.bit_length() - 1
    row_blk = lax.shift_right_logical(row, sub)
    col_blk = lax.shift_right_logical(col, sub)
    lower = jnp.where(col <= row, 1.0, 0.0).astype(BF16)
    gstart = jnp.where(col_blk < row_blk, 1.0, 0.0).astype(BF16)
    dmat = jnp.where(row_blk == col_blk, row - col, -1)
    off_mask = col_blk < row_blk
    ng = ng_ref[...]

    def step(i, states, factored):
        rows = pl.ds(pl.multiple_of(i * c, c), c)
        new_states = []
        for hd in range(HG_HEADS_PER_STEP):
            lanes = slice(hd * HG_KEY_DIM, (hd + 1) * HG_KEY_DIM)
            o, state = _hgrn_chunk(q_ref[rows, lanes], f_ref[rows, lanes], v_ref[rows, lanes], lb_ref[hd],
                                   states[hd], shift_ref.at[hd], lower, gstart, dmat, off_mask, factored)
            gate = gate_ref[rows, lanes]
            ms = jnp.mean(o * o, axis=-1, keepdims=True)
            o_ref[rows, lanes] = o * lax.rsqrt(ms + RMS_EPS) * ng * (gate * _sigmoid(gate))
            new_states.append(state)
        return tuple(new_states)

    def run(factored):
        init = tuple(state_ref[hd] for hd in range(HG_HEADS_PER_STEP))
        states = lax.fori_loop(0, HG_ROWS // c, functools.partial(step, factored=factored), init)
        for hd, state in enumerate(states):
            state_ref[hd] = state

    can_factor = jnp.min(lb_ref[...]) > math.exp(-HG_FACTOR_LIMIT / HG_SUB)
    lax.cond(can_factor, functools.partial(run, True), functools.partial(run, False))


def _hgrn(h, lb, ng, bsz, seq, col0):
    nblk = seq // HG_ROWS
    hps = HG_HEADS_PER_STEP
    width = hps * HG_KEY_DIM
    tiles = HG_COLS // width
    assert (col0 * LANES) % width == 0 and seq % HG_ROWS == 0
    blk = lambda k: pl.BlockSpec((HG_ROWS, width),
                                 lambda i, j, c: (i * nblk + c, (col0 * LANES) // width + k * tiles + j))
    return pl.pallas_call(
        _hgrn_kernel,
        grid=(bsz, HG_HEADS // hps, nblk),
        in_specs=[blk(0), blk(1), blk(2), blk(3),
                  pl.BlockSpec((hps, 1, HG_KEY_DIM), lambda i, j, c: (j, 0, 0)),
                  pl.BlockSpec((1, HG_VAL_DIM), lambda i, j, c: (0, 0))],
        out_specs=pl.BlockSpec((HG_ROWS, width), lambda i, j, c: (i * nblk + c, j)),
        out_shape=jax.ShapeDtypeStruct((bsz * seq, HG_COLS), F32),
        scratch_shapes=[pltpu.VMEM((hps, HG_VAL_DIM, HG_KEY_DIM), F32),
                        pltpu.VMEM((hps, 2, HG_SUB + HG_CHUNK, HG_KEY_DIM), F32)],
        compiler_params=_params(("parallel", "parallel", "arbitrary")),
        name="hgrn",
    )(h, h, h, h, lb, ng)


def _mix_kernel(nsa_ref, hg_ref, x_ref, w_ref, ng_ref, g_ref, b_ref, o_ref):
    o_n = nsa_ref[...]
    half = o_n.shape[1]
    o_n = o_n * lax.rsqrt(jnp.mean(o_n * o_n, axis=-1, keepdims=True) + RMS_EPS) * ng_ref[...]
    mix = _dot(o_n.astype(BF16), w_ref[:half]) + _dot(hg_ref[...].astype(BF16), w_ref[half:])
    o_ref[...] = _layer_norm(DEEPNORM_ALPHA * x_ref[...] + mix, g_ref[...], b_ref[...])


def _mix(o_nsa, o_hg, x, w_out, ng, g, b, tm):
    t, d = x.shape
    half = o_nsa.shape[1]
    rows = lambda w: pl.BlockSpec((tm, w), lambda i: (i, 0))
    full = lambda r, c: pl.BlockSpec((r, c), lambda i: (0, 0))
    return pl.pallas_call(
        _mix_kernel,
        grid=(t // tm,),
        in_specs=[rows(half), rows(half), rows(d), full(d, d), full(1, half), full(1, d), full(1, d)],
        out_specs=rows(d),
        out_shape=jax.ShapeDtypeStruct((t, d), F32),
        compiler_params=_params(("parallel",)),
        name="mix",
    )(o_nsa, o_hg, x, w_out, ng, g, b)


def _xattn_kernel(x_ref, wq_ref, kT_ref, v_ref, wo_ref, g_ref, b_ref, o_ref):
    x = x_ref[0]
    d = x.shape[1]
    dh = d // XA_HEADS
    q = _dot(x.astype(BF16), wq_ref[...])
    heads = []
    for h in range(XA_HEADS):
        cols = slice(h * dh, (h + 1) * dh)
        s = _dot(q[:, cols].astype(BF16), kT_ref[0, cols, :]) * (dh ** -0.5)
        e = jnp.exp(s - jnp.max(s, axis=-1, keepdims=True))
        p = e * (1.0 / jnp.sum(e, axis=-1, keepdims=True))
        heads.append(_dot(p.astype(BF16), v_ref[0, :, cols]))
    o = jnp.concatenate(heads, axis=1)
    xa = _dot(o.astype(BF16), wo_ref[...])
    o_ref[0] = _layer_norm(DEEPNORM_ALPHA * x + xa, g_ref[...], b_ref[...])


def _xattn(x, wq, kT, v, wo, g, b, tm):
    bsz, seq, d = x.shape
    m = v.shape[1]
    full = lambda r, c: pl.BlockSpec((r, c), lambda i, j: (0, 0))
    return pl.pallas_call(
        _xattn_kernel,
        grid=(bsz, seq // tm),
        in_specs=[pl.BlockSpec((1, tm, d), lambda i, j: (i, j, 0)),
                  full(d, d),
                  pl.BlockSpec((1, d, m), lambda i, j: (i, 0, 0)),
                  pl.BlockSpec((1, m, d), lambda i, j: (i, 0, 0)),
                  full(d, d), full(1, d), full(1, d)],
        out_specs=pl.BlockSpec((1, tm, d), lambda i, j: (i, j, 0)),
        out_shape=jax.ShapeDtypeStruct((bsz, seq, d), F32),
        compiler_params=_params(("parallel", "parallel")),
        name="xattn",
    )(x, wq, kT, v, wo, g, b)


def _route_kernel(x_ref, wh_ref, wl_ref, bias_ref, ids_ref, wts_ref, rnk_ref, counts_ref, cnt_ref):
    xh, xl = _split2(x_ref[...])
    logits = _dot(xh, wh_ref[...]) + _dot(xh, wl_ref[...]) + _dot(xl, wh_ref[...]) + bias_ref[...]
    lane = lax.broadcasted_iota(jnp.int32, logits.shape, 1)
    lane_f = lane.astype(F32)
    big = float(LANES)

    is_g = lane < MOE_GROUPS
    g_max = jnp.max(jnp.where(is_g, logits, NEG_INF), axis=-1, keepdims=True)
    g_sum = jnp.sum(jnp.where(is_g, jnp.exp(logits - g_max), 0.0), axis=-1, keepdims=True)
    g_w = 1.0 / g_sum
    g_idx = jnp.min(jnp.where(is_g & (logits == g_max), lane_f, big), axis=-1, keepdims=True)

    e_lo = MOE_GROUPS + MOE_EXPERTS_PER_GROUP * g_idx
    is_e = (lane_f >= e_lo) & (lane_f < e_lo + MOE_EXPERTS_PER_GROUP)
    e_log = jnp.where(is_e, logits, NEG_INF)
    e_max = jnp.max(e_log, axis=-1, keepdims=True)
    e_exp = jnp.where(is_e, jnp.exp(logits - e_max), 0.0)
    e_sum = jnp.sum(e_exp, axis=-1, keepdims=True)
    i1 = jnp.min(jnp.where(is_e & (e_log == e_max), lane_f, big), axis=-1, keepdims=True)
    rest = jnp.where(lane_f == i1, NEG_INF, e_log)
    r_max = jnp.max(rest, axis=-1, keepdims=True)
    i2 = jnp.min(jnp.where(is_e & (lane_f != i1) & (rest == r_max), lane_f, big), axis=-1, keepdims=True)
    p1 = 1.0 / e_sum
    p2 = jnp.exp(r_max - e_max) / e_sum
    tot = p1 + p2
    w1 = g_w * (p1 / tot)
    w2 = g_w * (p2 / tot)
    ids = jnp.where(lane == 0, i1, i2) - float(MOE_GROUPS)
    ids_ref[...] = ids.astype(jnp.int32)
    wts_ref[...] = jnp.where(lane == 0, w1, w2)

    @pl.when(pl.program_id(0) == 0)
    def _():
        cnt_ref[...] = jnp.zeros_like(cnt_ref)

    tm = logits.shape[0]
    hit1 = lane_f == i1
    hit2 = lane_f == i2
    hits = jnp.where(hit1 | hit2, 1.0, 0.0)
    earlier = lax.broadcasted_iota(jnp.int32, (tm, tm), 1) < lax.broadcasted_iota(jnp.int32, (tm, tm), 0)
    before = _dot(jnp.where(earlier, 1.0, 0.0).astype(BF16), hits.astype(BF16)) + cnt_ref[...]
    rank1 = jnp.sum(jnp.where(hit1, before, 0.0), axis=-1, keepdims=True)
    rank2 = jnp.sum(jnp.where(hit2, before, 0.0), axis=-1, keepdims=True)
    rnk_ref[...] = jnp.where(lane == 0, rank1, rank2).astype(jnp.int32)
    cnt_ref[...] = cnt_ref[...] + jnp.sum(hits, axis=0, keepdims=True)
    counts_ref[...] = cnt_ref[...].astype(jnp.int32)


def _route(x, wh, wl, bias, tm):
    t, d = x.shape
    rows = lambda w: pl.BlockSpec((tm, w), lambda i: (i, 0))
    full = lambda r, c: pl.BlockSpec((r, c), lambda i: (0, 0))
    i32 = jnp.int32
    return pl.pallas_call(
        _route_kernel,
        grid=(t // tm,),
        in_specs=[rows(d), full(d, LANES), full(d, LANES), full(1, LANES)],
        out_specs=[rows(LANES), rows(LANES), rows(LANES), full(1, LANES)],
        out_shape=[jax.ShapeDtypeStruct((t, LANES), i32), jax.ShapeDtypeStruct((t, LANES), F32),
                   jax.ShapeDtypeStruct((t, LANES), i32), jax.ShapeDtypeStruct((1, LANES), i32)],
        scratch_shapes=[pltpu.VMEM((1, LANES), F32)],
        compiler_params=_params(("arbitrary",)),
        name="route",
    )(x, wh, wl, bias)


def _row_copy_wait(src, dst, sem, rows):
    pltpu.make_async_copy(src.at[pl.ds(0, rows)], dst.at[pl.ds(0, rows)], sem).wait()


def _dispatch_kernel(dest_ref, zfill_ref, x_ref, xs_hbm, zbuf, zsem, sem):
    i = pl.program_id(0)
    tm = x_ref.shape[0]

    @pl.when(i == 0)
    def _():
        zbuf[...] = jnp.zeros_like(zbuf)
        tail = lambda e: xs_hbm.at[pl.ds(pl.multiple_of(zfill_ref[e], MOE_BLOCK), MOE_BLOCK)]
        for e in range(zfill_ref.shape[0]):
            @pl.when(zfill_ref[e] >= 0)
            def _():
                pltpu.make_async_copy(zbuf, tail(e), zsem).start()
        for e in range(zfill_ref.shape[0]):
            @pl.when(zfill_ref[e] >= 0)
            def _():
                pltpu.make_async_copy(zbuf, tail(e), zsem).wait()

    def send(j, c):
        for k in range(MOE_TOP_K):
            slot = dest_ref[(i * tm + j) * MOE_TOP_K + k]
            pltpu.make_async_copy(x_ref.at[pl.ds(j, 1)], xs_hbm.at[pl.ds(slot, 1)], sem).start()
        return c

    lax.fori_loop(0, tm, send, 0, unroll=8)
    for _ in range(MOE_TOP_K):
        _row_copy_wait(x_ref, xs_hbm, sem, tm)


def _dispatch(dest, zfill, x, cap, tm):
    t, d = x.shape
    grid_spec = pltpu.PrefetchScalarGridSpec(
        num_scalar_prefetch=2,
        grid=(t // tm,),
        in_specs=[pl.BlockSpec((tm, d), lambda i, dest, zfill: (i, 0))],
        out_specs=pl.BlockSpec(memory_space=pl.ANY),
        scratch_shapes=[pltpu.VMEM((MOE_BLOCK, d), F32), pltpu.SemaphoreType.DMA, pltpu.SemaphoreType.DMA])
    return pl.pallas_call(
        _dispatch_kernel,
        grid_spec=grid_spec,
        out_shape=jax.ShapeDtypeStruct((cap, d), F32),
        compiler_params=_params(("arbitrary",)),
        name="dispatch",
    )(dest, zfill, x)


def _experts_kernel(bexp_ref, nact_ref, xs_ref, wg_ref, wu_ref, wd_ref, y_ref):
    @pl.when(pl.program_id(0) < nact_ref[0])
    def _():
        xb = xs_ref[...].astype(BF16)
        gate = _dot(xb, wg_ref[0])
        hid = (gate * _sigmoid(gate)) * _dot(xb, wu_ref[0])
        y_ref[...] = _dot(hid.astype(BF16), wd_ref[0])

    @pl.when(pl.program_id(0) >= nact_ref[0])
    def _():
        y_ref[...] = jnp.zeros_like(y_ref)


def _experts(bexp, nact, xs, wg, wu, wd):
    cap, d = xs.shape
    dff = wg.shape[-1]
    by_expert = lambda i, bexp, nact: (bexp[i], 0, 0)
    active = lambda i, bexp, nact: (jnp.minimum(i, nact[0] - 1), 0)
    grid_spec = pltpu.PrefetchScalarGridSpec(
        num_scalar_prefetch=2,
        grid=(cap // MOE_BLOCK,),
        in_specs=[pl.BlockSpec((MOE_BLOCK, d), active),
                  pl.BlockSpec((1, d, dff), by_expert),
                  pl.BlockSpec((1, d, dff), by_expert),
                  pl.BlockSpec((1, dff, d), by_expert)],
        out_specs=pl.BlockSpec((MOE_BLOCK, d), lambda i, bexp, nact: (i, 0)))
    return pl.pallas_call(
        _experts_kernel,
        grid_spec=grid_spec,
        out_shape=jax.ShapeDtypeStruct((cap, d), F32),
        compiler_params=_params(("arbitrary",)),
        name="experts",
    )(bexp, nact, xs, wg, wu, wd)


def _combine_kernel(dest_ref, y_hbm, x_ref, wts_ref, g_ref, b_ref, o_ref, ybuf, sem):
    i = pl.program_id(0)
    n = pl.num_programs(0)
    tm = x_ref.shape[0]
    cur = lax.rem(i, 2)

    def fetch(tile, buf):
        def one(j, c):
            for k in range(MOE_TOP_K):
                slot = dest_ref[(tile * tm + j) * MOE_TOP_K + k]
                pltpu.make_async_copy(y_hbm.at[pl.ds(slot, 1)], ybuf.at[buf, k, pl.ds(j, 1)], sem.at[buf]).start()
            return c
        lax.fori_loop(0, tm, one, 0, unroll=8)

    @pl.when(i == 0)
    def _():
        fetch(0, 0)

    @pl.when(i + 1 < n)
    def _():
        fetch(i + 1, 1 - cur)

    for k in range(MOE_TOP_K):
        _row_copy_wait(y_hbm, ybuf.at[cur, k], sem.at[cur], tm)
    w = wts_ref[...]
    ff = w[:, 0:1] * ybuf[cur, 0]
    for k in range(1, MOE_TOP_K):
        ff = ff + w[:, k:k + 1] * ybuf[cur, k]
    o_ref[...] = _layer_norm(DEEPNORM_ALPHA * x_ref[...] + ff, g_ref[...], b_ref[...])


def _combine(dest, y, x, wts, g, b, tm):
    t, d = x.shape
    rows = lambda w: pl.BlockSpec((tm, w), lambda i, dest: (i, 0))
    full = lambda r, c: pl.BlockSpec((r, c), lambda i, dest: (0, 0))
    grid_spec = pltpu.PrefetchScalarGridSpec(
        num_scalar_prefetch=1,
        grid=(t // tm,),
        in_specs=[pl.BlockSpec(memory_space=pl.ANY), rows(d), rows(LANES), full(1, d), full(1, d)],
        out_specs=rows(d),
        scratch_shapes=[pltpu.VMEM((2, MOE_TOP_K, tm, d), F32), pltpu.SemaphoreType.DMA((2,))])
    return pl.pallas_call(
        _combine_kernel,
        grid_spec=grid_spec,
        out_shape=jax.ShapeDtypeStruct((t, d), F32),
        compiler_params=_params(("arbitrary",)),
        name="combine",
    )(dest, y, x, wts, g, b)


def _dispatch_plan(ids, rnk, counts, n_tok):
    padded = ((counts + MOE_BLOCK - 1) // MOE_BLOCK) * MOE_BLOCK
    pend = jnp.cumsum(padded)
    pstart = pend - padded
    dest = (pstart[ids] + rnk).reshape(-1).astype(jnp.int32)
    cap = n_tok * MOE_TOP_K + MOE_N_EXPERTS * MOE_BLOCK
    n_blocks = cap // MOE_BLOCK
    block_start = jnp.arange(n_blocks, dtype=jnp.int32) * MOE_BLOCK
    bexp = jnp.minimum(jnp.sum(block_start[:, None] >= pend[None, :], axis=1), MOE_N_EXPERTS - 1).astype(jnp.int32)
    nact = (pend[-1] // MOE_BLOCK).astype(jnp.int32).reshape(1)
    tails = jnp.where(padded > 0, pend - MOE_BLOCK, -1)
    idle = pend[-1] + block_start[:MOE_N_EXPERTS]
    zfill = jnp.concatenate([tails, jnp.where(idle < cap, idle, -1)]).astype(jnp.int32)
    return dest, bexp, nact, zfill, cap


def kernel(x, mem, w_in, cmp_pe_k, cmp_pe_v, cmp_w1_k, cmp_w2_k, cmp_w1_v, cmp_w2_v, nsa_norm_g,
           hg_lb_logits, hg_norm_g, w_out, ln1_g, ln1_b, xa_wq, xa_wk, xa_wv, xa_wo, ln2_g, ln2_b,
           moe_w_group, moe_b_group, moe_w_expert, moe_b_expert, moe_w_gate, moe_w_up, moe_w_down,
           ln3_g, ln3_b):
    b, s, d = x.shape
    t = b * s
    g, dh = NSA_KV_GROUPS, NSA_HEAD_DIM
    nch = s // CMP_STRIDE
    row = lambda a: a.reshape(1, -1)
    lb_all = jnp.cumsum(jax.nn.softmax(hg_lb_logits.astype(F32), axis=0), axis=0)
    xt = x.reshape(t, d)
    for l in range(DEPTH):
        nsa_cols = NSA_Q_COLS + 6 * NSA_KV_COLS
        w_l = w_in[l]
        w_perm = jnp.concatenate(
            [w_l[:, :NSA_Q_COLS], w_l[:, nsa_cols + NSA_GATE_COLS:], w_l[:, NSA_Q_COLS:nsa_cols],
             w_l[:, nsa_cols:nsa_cols + NSA_GATE_COLS],
             jnp.zeros((d, LANES - NSA_GATE_COLS), w_l.dtype)], axis=1)
        h, ks, kw, vsT, vwT = _inproj(xt, w_perm.astype(BF16), b, s, 256)
        nkc = h[:, NSA_KV0:NSA_KV0 + NSA_KV_COLS]
        nvc = h[:, NSA_KV0 + NSA_KV_COLS:NSA_KV0 + 2 * NSA_KV_COLS]
        hg_col0 = NSA_Q_COLS // LANES
        gate_tile = (nsa_cols + 4 * HG_COLS) // LANES

        chunks = lambda a: jnp.transpose(a.reshape(b, nch, CMP_STRIDE, g, dh), (0, 3, 1, 2, 4)).reshape(
            b, g, nch, CMP_STRIDE * dh)
        comp = _compress(jnp.stack([chunks(nkc), chunks(nvc)]),
                         jnp.stack([cmp_pe_k[l].reshape(1, -1), cmp_pe_v[l].reshape(1, -1)]),
                         jnp.stack([cmp_w1_k[l], cmp_w1_v[l]]),
                         jnp.stack([cmp_w2_k[l], cmp_w2_v[l]]))
        kc = comp[0].astype(BF16)
        vcT = jnp.swapaxes(comp[1], -1, -2).astype(BF16)

        o_nsa = _nsa(h, kc, vcT, ks, vsT, kw, vwT, s, gate_tile)

        o_hg = _hgrn(h, lb_all[l].reshape(HG_HEADS, 1, HG_KEY_DIM), row(hg_norm_g[l]), b, s, hg_col0)

        x1 = _mix(o_nsa, o_hg, xt, w_out[l].astype(BF16), row(nsa_norm_g[l]), row(ln1_g[l]), row(ln1_b[l]), 512)

        n_mem = mem.shape[1]
        kv = _matmul(mem.reshape(b * n_mem, d),
                     jnp.concatenate([xa_wk[l], xa_wv[l]], axis=1).astype(BF16), n_mem)
        kT = jnp.swapaxes(kv[:, :d].reshape(b, n_mem, d), 1, 2).astype(BF16)
        v = kv[:, d:].reshape(b, n_mem, d).astype(BF16)
        x2 = _xattn(x1.reshape(b, s, d), xa_wq[l].astype(BF16), kT, v, xa_wo[l].astype(BF16),
                    row(ln2_g[l]), row(ln2_b[l]), 512).reshape(t, d)

        w_r = jnp.concatenate([moe_w_group[l], moe_w_expert[l]], axis=1)
        w_r = jnp.pad(w_r, ((0, 0), (0, LANES - w_r.shape[1])))
        b_r = jnp.pad(jnp.concatenate([moe_b_group[l], moe_b_expert[l]]), (0, LANES - MOE_GROUPS - MOE_N_EXPERTS))
        w_rh, w_rl = _split2(w_r)
        ids, wts, rnk, counts = _route(x2, w_rh, w_rl, row(b_r), 512)
        dest, bexp, nact, zfill, cap = _dispatch_plan(
            ids[:, :MOE_TOP_K], rnk[:, :MOE_TOP_K], counts[0, MOE_GROUPS:MOE_GROUPS + MOE_N_EXPERTS], t)
        xs = _dispatch(dest, zfill, x2, cap, 256)
        ys = _experts(bexp, nact, xs, moe_w_gate[l].astype(BF16), moe_w_up[l].astype(BF16),
                      moe_w_down[l].astype(BF16))
        xt = _combine(dest, ys, x2, wts, row(ln3_g[l]), row(ln3_b[l]), 256)
    return xt.reshape(b, s, d)
```

```python
import functools
import math

import jax
import jax.numpy as jnp
from jax import lax
from jax.experimental import pallas as pl
from jax.experimental.pallas import tpu as pltpu

F32 = jnp.float32
BF16 = jnp.bfloat16

NSA_HEAD_DIM = 64
NSA_HEADS = 8
NSA_KV_GROUPS = 2
NSA_HPG = NSA_HEADS // NSA_KV_GROUPS
CMP_BLOCK = 32
CMP_STRIDE = 16
CMP_HIDDEN = 256
SEL_BLOCK = 64
SEL_TOP_N = 16
WINDOW = 512
FORCE_BONUS = 1.0e4
HG_KEY_DIM = 128
HG_VAL_DIM = 128
HG_HEADS = 4
XA_HEADS = 4
MOE_GROUPS = 4
MOE_EXPERTS_PER_GROUP = 8
MOE_N_EXPERTS = MOE_GROUPS * MOE_EXPERTS_PER_GROUP
MOE_TOP_K = 2
DEPTH = 1
DEEPNORM_ALPHA = (2.0 * DEPTH) ** 0.25
LN_EPS = 1e-5
RMS_EPS = 1e-6
NEG_INF = -1e30

NSA_Q_COLS = NSA_HEADS * NSA_HEAD_DIM
NSA_KV_COLS = NSA_KV_GROUPS * NSA_HEAD_DIM
NSA_GATE_COLS = NSA_HEADS * 3
HG_COLS = HG_HEADS * HG_KEY_DIM
NSA_KV0 = NSA_Q_COLS + 4 * HG_COLS

LANES = 128
Q_TILE = 128
NSA_ROWS = NSA_HPG * Q_TILE
SEL_KV_TILE = 512
SEL_GROUP = 6
SEL_SPLIT = 2
NSA_PARTS = 4
NSA_V_ROWS = NSA_HEAD_DIM + 16
LOG2_E = 1.4426950408889634
HG_CHUNK = 128
HG_SUB = 16
HG_ROWS = 512
HG_HEADS_PER_STEP = 4
HG_FACTOR_LIMIT = 60.0
MOE_BLOCK = 256
VMEM_LIMIT = 48 * 1024 * 1024


def _dot(a, b):
    return jnp.dot(a, b, preferred_element_type=F32)


def _dot_nt(a, b):
    return lax.dot_general(a, b, (((1,), (1,)), ((), ())), preferred_element_type=F32)


def _split2(a):
    hi = a.astype(BF16)
    lo = (a - hi.astype(F32)).astype(BF16)
    return hi, lo


def _split3(a):
    p1 = a.astype(BF16)
    r1 = a - p1.astype(F32)
    p2 = r1.astype(BF16)
    p3 = (r1 - p2.astype(F32)).astype(BF16)
    return p1, p2, p3


def _dot3(a, b):
    ah, al = _split2(a)
    bh, bl = _split2(b)
    return _dot(ah, bh) + _dot(ah, bl) + _dot(al, bh)


def _sigmoid(x):
    return 1.0 / (1.0 + jnp.exp(-x))


def _layer_norm(y, g, b):
    mu = jnp.mean(y, axis=-1, keepdims=True)
    d = y - mu
    var = jnp.mean(d * d, axis=-1, keepdims=True)
    return d * lax.rsqrt(var + LN_EPS) * g + b


def _params(sem):
    return pltpu.CompilerParams(dimension_semantics=sem, vmem_limit_bytes=VMEM_LIMIT)


def _matmul_kernel(x_ref, w_ref, o_ref):
    o_ref[...] = _dot(x_ref[...].astype(BF16), w_ref[...]).astype(o_ref.dtype)


def _matmul(x, w, tm):
    m, k = x.shape
    n = w.shape[1]
    return pl.pallas_call(
        _matmul_kernel,
        grid=(m // tm,),
        in_specs=[pl.BlockSpec((tm, k), lambda i: (i, 0)),
                  pl.BlockSpec((k, n), lambda i: (0, 0))],
        out_specs=pl.BlockSpec((tm, n), lambda i: (i, 0)),
        out_shape=jax.ShapeDtypeStruct((m, n), F32),
        compiler_params=_params(("parallel",)),
        name="proj",
    )(x, w)


def _inproj_kernel(x_ref, w_ref, h_ref, ks_ref, kw_ref, vsT_ref, vwT_ref, *, tiles_per_seq):
    h = _dot(x_ref[...].astype(BF16), w_ref[...])
    h_ref[...] = h
    tm = h.shape[0]
    dh = NSA_HEAD_DIM
    k_sel0 = NSA_KV0 + 2 * NSA_KV_COLS
    v_sel0, k_win0, v_win0 = k_sel0 + NSA_KV_COLS, k_sel0 + 2 * NSA_KV_COLS, k_sel0 + 3 * NSA_KV_COLS
    pos = lax.rem(pl.program_id(0), tiles_per_seq) * tm + lax.broadcasted_iota(jnp.int32, (tm, 1), 0)
    blk = lax.shift_right_logical(pos & (SEL_KV_TILE - 1), SEL_BLOCK.bit_length() - 1)
    onehot = jnp.where(blk == lax.broadcasted_iota(jnp.int32, (1, LANES - dh), 1), 1.0, 0.0)
    ones_row = jnp.where(lax.broadcasted_iota(jnp.int32, (NSA_V_ROWS - dh, tm), 0) == 0, 1.0, 0.0)
    vsT = h[:, v_sel0:v_sel0 + NSA_KV_COLS].T
    vwT = h[:, v_win0:v_win0 + NSA_KV_COLS].T
    for g in range(NSA_KV_GROUPS):
        cols = slice(g * dh, (g + 1) * dh)
        ks_ref[0, g] = jnp.concatenate([h[:, k_sel0 + g * dh:k_sel0 + (g + 1) * dh], onehot], axis=1).astype(BF16)
        kw_ref[0, g] = h[:, k_win0 + g * dh:k_win0 + (g + 1) * dh].astype(BF16)
        vsT_ref[0, g] = jnp.concatenate([vsT[cols], ones_row], axis=0).astype(BF16)
        vwT_ref[0, g] = jnp.concatenate([vwT[cols], ones_row], axis=0).astype(BF16)


def _inproj(x, w, bsz, seq, tm):
    t, k = x.shape
    n = w.shape[1]
    g, dh = NSA_KV_GROUPS, NSA_HEAD_DIM
    tps = seq // tm
    keys = lambda width: pl.BlockSpec((1, g, tm, width), lambda i: (i // tps, 0, i % tps, 0))
    vals = pl.BlockSpec((1, g, NSA_V_ROWS, tm), lambda i: (i // tps, 0, 0, i % tps))
    return pl.pallas_call(
        functools.partial(_inproj_kernel, tiles_per_seq=tps),
        grid=(t // tm,),
        in_specs=[pl.BlockSpec((tm, k), lambda i: (i, 0)),
                  pl.BlockSpec((k, n), lambda i: (0, 0))],
        out_specs=[pl.BlockSpec((tm, n), lambda i: (i, 0)), keys(LANES), keys(dh), vals, vals],
        out_shape=[jax.ShapeDtypeStruct((t, n), F32),
                   jax.ShapeDtypeStruct((bsz, g, seq, LANES), BF16),
                   jax.ShapeDtypeStruct((bsz, g, seq, dh), BF16),
                   jax.ShapeDtypeStruct((bsz, g, NSA_V_ROWS, seq), BF16),
                   jax.ShapeDtypeStruct((bsz, g, NSA_V_ROWS, seq), BF16)],
        compiler_params=_params(("parallel",)),
        name="inproj",
    )(x, w)


def _compress_kernel(ch_ref, pe_ref, w1_ref, w2_ref, o_ref):
    ch = ch_ref[0, 0, 0]
    half = ch.shape[1]
    nch = ch.shape[0]
    pe = pe_ref[0]
    w1 = w1_ref[0]
    top = _dot3(ch + pe[:, :half], w1[:half])
    bot = _dot3(ch + pe[:, half:], w1[half:])
    hid = top + pltpu.roll(bot, nch - 1, 0)
    c = 0.7978845608028654
    act = 0.5 * hid * (1.0 + jnp.tanh(c * (hid + 0.044715 * hid * hid * hid)))
    out = _dot3(act, w2_ref[0])
    row = lax.broadcasted_iota(jnp.int32, out.shape, 0)
    o_ref[0, 0, 0] = jnp.where(row < nch - 1, out, 0.0)


def _compress(ch, pe, w1, w2):
    _, b, g, nch, width = ch.shape
    hidden = w1.shape[-1]
    d = w2.shape[-1]
    return pl.pallas_call(
        _compress_kernel,
        grid=(2, b, g),
        in_specs=[pl.BlockSpec((1, 1, 1, nch, width), lambda a, i, j: (a, i, j, 0, 0)),
                  pl.BlockSpec((1, 1, 2 * width), lambda a, i, j: (a, 0, 0)),
                  pl.BlockSpec((1, 2 * width, hidden), lambda a, i, j: (a, 0, 0)),
                  pl.BlockSpec((1, hidden, d), lambda a, i, j: (a, 0, 0))],
        out_specs=pl.BlockSpec((1, 1, 1, nch, d), lambda a, i, j: (a, i, j, 0, 0)),
        out_shape=jax.ShapeDtypeStruct((2, b, g, nch, d), F32),
        compiler_params=_params(("parallel", "parallel", "parallel")),
        name="compress",
    )(ch, pe, w1, w2)


def _tile_heads(a):
    return jnp.concatenate([a] * NSA_HPG, axis=1)


def _nsa_kernel(q_ref, kc_ref, vcT_ref, ks_ref, vsT_ref, kw_ref, vwT_ref, gate_ref,
                o_ref, sel_ref, qx_ref, psum_ref, gt_ref, *, seq):
    s0 = pl.program_id(2) * Q_TILE
    nsel = seq // SEL_BLOCK
    ncmp_pad = seq // CMP_STRIDE
    dh = NSA_HEAD_DIM

    q_t = q_ref[...].T
    q = jnp.concatenate([q_t[r * dh:(r + 1) * dh] for r in range(NSA_HPG)], axis=1)
    q_hi = (q * (dh ** -0.5 * LOG2_E)).astype(BF16)
    t_q = s0 + lax.broadcasted_iota(jnp.int32, (1, Q_TILE), 1)
    t_all = _tile_heads(t_q)

    def window_branch():
        span = WINDOW + Q_TILE
        lo = pl.multiple_of(jnp.maximum(s0 - WINDOW, 0), Q_TILE)
        kpos = lo + lax.broadcasted_iota(jnp.int32, (span, 1), 0)
        sw = _dot(kw_ref[0, 0, pl.ds(lo, span), :], q_hi)
        sw = sw + _tile_heads(jnp.where((kpos <= t_q) & (kpos > t_q - WINDOW), 0.0, NEG_INF))
        e_w = jnp.exp2(sw - jnp.max(sw, axis=0, keepdims=True)).astype(BF16)
        acc_w = _dot(vwT_ref[0, 0, :, pl.ds(lo, span)], e_w)
        return acc_w[0:dh] * (1.0 / acc_w[dh:dh + 1])

    def compressed_and_select(parts):
        ncmp = parts * (ncmp_pad // NSA_PARTS)
        rows = parts * (nsel // NSA_PARTS)
        o_w = window_branch()
        sc = _dot(kc_ref[0, 0, 0:ncmp, :], q_hi)
        n_end = lax.broadcasted_iota(jnp.int32, (ncmp, 1), 0) * CMP_STRIDE + (CMP_BLOCK - 1)
        sc = sc + _tile_heads(jnp.where(n_end <= t_q, 0.0, NEG_INF))
        e_c = jnp.exp2(sc - jnp.max(sc, axis=0, keepdims=True))
        l_c = jnp.sum(e_c, axis=0, keepdims=True)
        p_c = e_c * jnp.where(t_all >= CMP_BLOCK - 1, 1.0 / l_c, 0.0)
        o_c = _dot(vcT_ref[0, 0, :, 0:ncmp], p_c.astype(BF16))

        p_sum = p_c[:, 0:Q_TILE]
        for r in range(1, NSA_HPG):
            p_sum = p_sum + p_c[:, r * Q_TILE:(r + 1) * Q_TILE]
        ratio = SEL_BLOCK // CMP_STRIDE
        first_row = 8 - (CMP_BLOCK // CMP_STRIDE - 1)
        psum_ref[0:8, :] = jnp.zeros((8, Q_TILE), F32)
        psum_ref[8:8 + ncmp, :] = p_sum
        imp = psum_ref[pl.ds(first_row, rows, stride=ratio), :]
        for m in range(1, ratio + CMP_BLOCK // CMP_STRIDE - 1):
            imp = imp + psum_ref[pl.ds(first_row + m, rows, stride=ratio), :]
        j_idx = lax.broadcasted_iota(jnp.int32, (rows, Q_TILE), 0)
        cur = lax.shift_right_logical(t_q, SEL_BLOCK.bit_length() - 1)
        forced = (j_idx == 0) | (j_idx == cur) | (j_idx == cur - 1)
        score = jnp.where(j_idx <= cur, imp + jnp.where(forced, FORCE_BONUS, 0.0), -1.0)
        j_f = j_idx.astype(F32)
        n_forced = 3 if (parts >= 2 and seq // NSA_PARTS >= Q_TILE + 2 * SEL_BLOCK and rows >= SEL_TOP_N) else 0
        sel = jnp.zeros((rows, Q_TILE), F32)
        if n_forced:
            sel = jnp.where(forced, 1.0, sel)
            score = jnp.where(forced, -3.0e38, score)
        for _ in range(min(SEL_TOP_N, rows) - n_forced):
            best = jnp.max(score, axis=0, keepdims=True)
            first = jnp.min(jnp.where(score == best, j_f, float(rows)), axis=0, keepdims=True)
            pick = j_f == first
            sel = jnp.where(pick, 1.0, sel)
            score = jnp.where(pick, -3.0e38, score)
        sel_ref[0:rows, :] = sel
        if parts < NSA_PARTS:
            sel_ref[rows:, :] = jnp.zeros((nsel - rows, Q_TILE), F32)
        return o_c, o_w

    part_len = seq // NSA_PARTS
    parts_needed = lax.div(s0 + (Q_TILE + part_len - 1), part_len)
    o_c, o_w = lax.switch(parts_needed - 1,
                     [functools.partial(compressed_and_select, n) for n in range(1, NSA_PARTS + 1)])

    blocks_per_tile = SEL_KV_TILE // SEL_BLOCK
    for slot in range(SEL_GROUP):
        qx_ref[slot, 0:dh, :] = q_hi
        qx_ref[slot, dh:, :] = jnp.zeros((qx_ref.shape[1] - dh, NSA_ROWS), BF16)
    piece = SEL_KV_TILE // SEL_SPLIT
    k_off = lax.broadcasted_iota(jnp.int32, (piece, 1), 0)

    def sel_scores(j, slot, causal):
        member = sel_ref[pl.ds(pl.multiple_of(j * blocks_per_tile, blocks_per_tile), blocks_per_tile), :]
        bias = _tile_heads(jnp.where(member > 0.5, 0.0, NEG_INF))
        qx_ref[slot, dh:dh + 2 * blocks_per_tile, :] = jnp.concatenate(
            [bias, jnp.zeros_like(bias)], axis=0).astype(BF16)
        parts = []
        for u in range(SEL_SPLIT):
            base = pl.multiple_of(j * SEL_KV_TILE + u * piece, piece)
            s = _dot(ks_ref[0, 0, pl.ds(base, piece), :], qx_ref[slot])
            if causal:
                s = s + _tile_heads(jnp.where((base + k_off) <= t_q, 0.0, NEG_INF))
            parts.append((base, s))
        return parts

    def sel_update(base, s, m_i, acc):
        m_new = jnp.maximum(m_i, jnp.max(s, axis=0, keepdims=True))
        p = jnp.exp2(s - m_new).astype(BF16)
        pv = _dot(vsT_ref[0, 0, :, pl.ds(base, piece)], p)
        return m_new, jnp.exp2(m_i - m_new) * acc + pv

    def sel_group(first_tile, carry, n_tiles, causal_last):
        scores = []
        for u in range(n_tiles):
            scores += sel_scores(first_tile + u, u, causal_last and u == n_tiles - 1)
        for base, s in scores:
            carry = sel_update(base, s, *carry)
        return carry

    n_tiles = lax.div(s0 + (Q_TILE + SEL_KV_TILE - 1), SEL_KV_TILE)
    n_main = lax.div(n_tiles - 1, SEL_GROUP)
    init = (jnp.full((1, NSA_ROWS), NEG_INF, F32), jnp.zeros((vsT_ref.shape[2], NSA_ROWS), F32))
    carry = lax.fori_loop(0, n_main, lambda i, c: sel_group(SEL_GROUP * i, c, SEL_GROUP, False), init)
    _, acc_s = lax.switch(
        n_tiles - SEL_GROUP * n_main - 1,
        [functools.partial(sel_group, n_tiles=n, causal_last=True) for n in range(1, SEL_GROUP + 1)],
        SEL_GROUP * n_main, carry)
    o_s = acc_s[0:dh] * (1.0 / acc_s[dh:dh + 1])

    gt_ref[...] = gate_ref[...].T
    g_row0 = pl.program_id(1) * (NSA_HPG * 3)

    def gate(branch):
        return _sigmoid(jnp.concatenate(
            [gt_ref[pl.ds(g_row0 + 3 * r + branch, 1), :] for r in range(NSA_HPG)], axis=1))

    o = gate(0) * o_c + gate(1) * o_s + gate(2) * o_w
    o_rows = jnp.concatenate([o[:, r * Q_TILE:(r + 1) * Q_TILE] for r in range(NSA_HPG)], axis=0)
    o_ref[...] = o_rows.T


def _nsa(h, kc, vcT, ks, vsT, kw, vwT, seq, gate_tile):
    b, g = ks.shape[:2]
    nqb = seq // Q_TILE
    d = NSA_HEAD_DIM
    ncp = seq // CMP_STRIDE
    nsel = seq // SEL_BLOCK
    dk = ks.shape[-1]
    dv = vsT.shape[2]
    per_bg = lambda i, j, k: (i, j, 0, 0)
    q_rows = lambda i, j, k: (i * nqb + k, j)
    return pl.pallas_call(
        functools.partial(_nsa_kernel, seq=seq),
        grid=(b, g, nqb),
        in_specs=[pl.BlockSpec((Q_TILE, NSA_HPG * d), q_rows),
                  pl.BlockSpec((1, 1, ncp, d), per_bg),
                  pl.BlockSpec((1, 1, d, ncp), per_bg),
                  pl.BlockSpec((1, 1, seq, dk), per_bg),
                  pl.BlockSpec((1, 1, dv, seq), per_bg),
                  pl.BlockSpec((1, 1, seq, d), per_bg),
                  pl.BlockSpec((1, 1, dv, seq), per_bg),
                  pl.BlockSpec((Q_TILE, LANES), lambda i, j, k: (i * nqb + k, gate_tile))],
        out_specs=pl.BlockSpec((Q_TILE, NSA_HPG * d), q_rows),
        out_shape=jax.ShapeDtypeStruct((b * seq, NSA_Q_COLS), F32),
        scratch_shapes=[pltpu.VMEM((nsel, Q_TILE), F32), pltpu.VMEM((SEL_GROUP, dk, NSA_ROWS), BF16),
                        pltpu.VMEM((ncp + 8, Q_TILE), F32), pltpu.VMEM((LANES, Q_TILE), F32)],
        compiler_params=_params(("parallel", "parallel", "arbitrary")),
        name="nsa",
    )(h, kc, vcT, ks, vsT, kw, vwT, h)


def _hgrn_chunk(q, f, v, lb, state_t, shift_ref, lower, gstart, dmat, off_mask, factored):
    c = HG_CHUNK
    log_f = jnp.log(lb + (1.0 - lb) * _sigmoid(f))
    kk = (1.0 - lb) * _sigmoid(-f)
    l1, l2, l3 = _split3(log_f)
    bt = _dot(lower, l1) + _dot(lower, l2) + _dot(lower, l3)
    bs = _dot(gstart, l1) + _dot(gstart, l2) + _dot(gstart, l3)

    qh = (q * jnp.exp(bt - bs)).astype(BF16)

    def same_block_direct():
        shift_ref[0, HG_SUB:, :] = kk
        shift_ref[1, HG_SUB:, :] = bt
        ones = jnp.ones((kk.shape[1], c), BF16)
        acc = jnp.zeros((c, c), F32)
        for d in range(HG_SUB):
            if d == 0:
                prod = q * kk
            else:
                rows = pl.ds(HG_SUB - d, c)
                prod = (q * shift_ref[0, rows, :]) * jnp.exp(bt - shift_ref[1, rows, :])
            band = _dot(prod.astype(BF16), ones)
            acc = jnp.where(dmat == d, band, acc)
        return acc

    def same_block_factored():
        return _dot_nt(qh, (kk * jnp.exp(bs - bt)).astype(BF16))

    att = same_block_factored() if factored else same_block_direct()
    att = jnp.where(dmat >= 0, att, 0.0)

    blocks = [jnp.zeros((HG_SUB, c), F32)]
    for i in range(1, c // HG_SUB):
        b_i = bs[i * HG_SUB:i * HG_SUB + 1, :]
        kh = kk * jnp.exp(jnp.minimum(b_i - bt, 0.0))
        blocks.append(_dot_nt(qh[i * HG_SUB:(i + 1) * HG_SUB], kh.astype(BF16)))
    att = jnp.where(off_mask, jnp.concatenate(blocks, axis=0), att)

    vb = v.astype(BF16)
    o = _dot_nt((q * jnp.exp(bt)).astype(BF16), state_t.astype(BF16)) + _dot(att.astype(BF16), vb)
    b_last = bt[c - 1:c, :]
    k_dec = kk * jnp.exp(b_last - bt)
    return o, state_t * jnp.exp(b_last) + _dot(v.T.astype(BF16), k_dec.astype(BF16))


def _hgrn_kernel(q_ref, f_ref, v_ref, gate_ref, lb_ref, ng_ref, o_ref, state_ref, shift_ref):
    c = HG_CHUNK

    @pl.when(pl.program_id(2) == 0)
    def _():
        state_ref[...] = jnp.zeros_like(state_ref)
        shift_ref[...] = jnp.zeros_like(shift_ref)

    row = lax.broadcasted_iota(jnp.int32, (c, c), 0)
    col = lax.broadcasted_iota(jnp.int32, (c, c), 1)
    sub = HG_SUB.bit_length() - 1
    row_blk = lax.shift_right_logical(row, sub)
    col_blk = lax.shift_right_logical(col, sub)
    lower = jnp.where(col <= row, 1.0, 0.0).astype(BF16)
    gstart = jnp.where(col_blk < row_blk, 1.0, 0.0).astype(BF16)
    dmat = jnp.where(row_blk == col_blk, row - col, -1)
    off_mask = col_blk < row_blk
    ng = ng_ref[...]

    def step(i, states, factored):
        rows = pl.ds(pl.multiple_of(i * c, c), c)
        new_states = []
        for hd in range(HG_HEADS_PER_STEP):
            lanes = slice(hd * HG_KEY_DIM, (hd + 1) * HG_KEY_DIM)
            o, state = _hgrn_chunk(q_ref[rows, lanes], f_ref[rows, lanes], v_ref[rows, lanes], lb_ref[hd],
                                   states[hd], shift_ref.at[hd], lower, gstart, dmat, off_mask, factored)
            gate = gate_ref[rows, lanes]
            ms = jnp.mean(o * o, axis=-1, keepdims=True)
            o_ref[rows, lanes] = o * lax.rsqrt(ms + RMS_EPS) * ng * (gate * _sigmoid(gate))
            new_states.append(state)
        return tuple(new_states)

    def run(factored):
        init = tuple(state_ref[hd] for hd in range(HG_HEADS_PER_STEP))
        states = lax.fori_loop(0, HG_ROWS // c, functools.partial(step, factored=factored), init)
        for hd, state in enumerate(states):
            state_ref[hd] = state

    can_factor = jnp.min(lb_ref[...]) > math.exp(-HG_FACTOR_LIMIT / HG_SUB)
    lax.cond(can_factor, functools.partial(run, True), functools.partial(run, False))


def _hgrn(h, lb, ng, bsz, seq, col0):
    nblk = seq // HG_ROWS
    hps = HG_HEADS_PER_STEP
    width = hps * HG_KEY_DIM
    tiles = HG_COLS // width
    assert (col0 * LANES) % width == 0 and seq % HG_ROWS == 0
    blk = lambda k: pl.BlockSpec((HG_ROWS, width),
                                 lambda i, j, c: (i * nblk + c, (col0 * LANES) // width + k * tiles + j))
    return pl.pallas_call(
        _hgrn_kernel,
        grid=(bsz, HG_HEADS // hps, nblk),
        in_specs=[blk(0), blk(1), blk(2), blk(3),
                  pl.BlockSpec((hps, 1, HG_KEY_DIM), lambda i, j, c: (j, 0, 0)),
                  pl.BlockSpec((1, HG_VAL_DIM), lambda i, j, c: (0, 0))],
        out_specs=pl.BlockSpec((HG_ROWS, width), lambda i, j, c: (i * nblk + c, j)),
        out_shape=jax.ShapeDtypeStruct((bsz * seq, HG_COLS), F32),
        scratch_shapes=[pltpu.VMEM((hps, HG_VAL_DIM, HG_KEY_DIM), F32),
                        pltpu.VMEM((hps, 2, HG_SUB + HG_CHUNK, HG_KEY_DIM), F32)],
        compiler_params=_params(("parallel", "parallel", "arbitrary")),
        name="hgrn",
    )(h, h, h, h, lb, ng)


def _mix_kernel(nsa_ref, hg_ref, x_ref, w_ref, ng_ref, g_ref, b_ref, o_ref):
    o_n = nsa_ref[...]
    half = o_n.shape[1]
    o_n = o_n * lax.rsqrt(jnp.mean(o_n * o_n, axis=-1, keepdims=True) + RMS_EPS) * ng_ref[...]
    mix = _dot(o_n.astype(BF16), w_ref[:half]) + _dot(hg_ref[...].astype(BF16), w_ref[half:])
    o_ref[...] = _layer_norm(DEEPNORM_ALPHA * x_ref[...] + mix, g_ref[...], b_ref[...])


def _mix(o_nsa, o_hg, x, w_out, ng, g, b, tm):
    t, d = x.shape
    half = o_nsa.shape[1]
    rows = lambda w: pl.BlockSpec((tm, w), lambda i: (i, 0))
    full = lambda r, c: pl.BlockSpec((r, c), lambda i: (0, 0))
    return pl.pallas_call(
        _mix_kernel,
        grid=(t // tm,),
        in_specs=[rows(half), rows(half), rows(d), full(d, d), full(1, half), full(1, d), full(1, d)],
        out_specs=rows(d),
        out_shape=jax.ShapeDtypeStruct((t, d), F32),
        compiler_params=_params(("parallel",)),
        name="mix",
    )(o_nsa, o_hg, x, w_out, ng, g, b)


def _xattn_kernel(x_ref, wq_ref, kT_ref, v_ref, wo_ref, g_ref, b_ref, o_ref):
    x = x_ref[0]
    d = x.shape[1]
    dh = d // XA_HEADS
    q = _dot(x.astype(BF16), wq_ref[...])
    heads = []
    for h in range(XA_HEADS):
        cols = slice(h * dh, (h + 1) * dh)
        s = _dot(q[:, cols].astype(BF16), kT_ref[0, cols, :]) * (dh ** -0.5)
        e = jnp.exp(s - jnp.max(s, axis=-1, keepdims=True))
        p = e * (1.0 / jnp.sum(e, axis=-1, keepdims=True))
        heads.append(_dot(p.astype(BF16), v_ref[0, :, cols]))
    o = jnp.concatenate(heads, axis=1)
    xa = _dot(o.astype(BF16), wo_ref[...])
    o_ref[0] = _layer_norm(DEEPNORM_ALPHA * x + xa, g_ref[...], b_ref[...])


def _xattn(x, wq, kT, v, wo, g, b, tm):
    bsz, seq, d = x.shape
    m = v.shape[1]
    full = lambda r, c: pl.BlockSpec((r, c), lambda i, j: (0, 0))
    return pl.pallas_call(
        _xattn_kernel,
        grid=(bsz, seq // tm),
        in_specs=[pl.BlockSpec((1, tm, d), lambda i, j: (i, j, 0)),
                  full(d, d),
                  pl.BlockSpec((1, d, m), lambda i, j: (i, 0, 0)),
                  pl.BlockSpec((1, m, d), lambda i, j: (i, 0, 0)),
                  full(d, d), full(1, d), full(1, d)],
        out_specs=pl.BlockSpec((1, tm, d), lambda i, j: (i, j, 0)),
        out_shape=jax.ShapeDtypeStruct((bsz, seq, d), F32),
        compiler_params=_params(("parallel", "parallel")),
        name="xattn",
    )(x, wq, kT, v, wo, g, b)


def _route_kernel(x_ref, wh_ref, wl_ref, bias_ref, ids_ref, wts_ref, rnk_ref, counts_ref, cnt_ref):
    xh, xl = _split2(x_ref[...])
    logits = _dot(xh, wh_ref[...]) + _dot(xh, wl_ref[...]) + _dot(xl, wh_ref[...]) + bias_ref[...]
    lane = lax.broadcasted_iota(jnp.int32, logits.shape, 1)
    lane_f = lane.astype(F32)
    big = float(LANES)

    is_g = lane < MOE_GROUPS
    g_max = jnp.max(jnp.where(is_g, logits, NEG_INF), axis=-1, keepdims=True)
    g_sum = jnp.sum(jnp.where(is_g, jnp.exp(logits - g_max), 0.0), axis=-1, keepdims=True)
    g_w = 1.0 / g_sum
    g_idx = jnp.min(jnp.where(is_g & (logits == g_max), lane_f, big), axis=-1, keepdims=True)

    e_lo = MOE_GROUPS + MOE_EXPERTS_PER_GROUP * g_idx
    is_e = (lane_f >= e_lo) & (lane_f < e_lo + MOE_EXPERTS_PER_GROUP)
    e_log = jnp.where(is_e, logits, NEG_INF)
    e_max = jnp.max(e_log, axis=-1, keepdims=True)
    e_exp = jnp.where(is_e, jnp.exp(logits - e_max), 0.0)
    e_sum = jnp.sum(e_exp, axis=-1, keepdims=True)
    i1 = jnp.min(jnp.where(is_e & (e_log == e_max), lane_f, big), axis=-1, keepdims=True)
    rest = jnp.where(lane_f == i1, NEG_INF, e_log)
    r_max = jnp.max(rest, axis=-1, keepdims=True)
    i2 = jnp.min(jnp.where(is_e & (lane_f != i1) & (rest == r_max), lane_f, big), axis=-1, keepdims=True)
    p1 = 1.0 / e_sum
    p2 = jnp.exp(r_max - e_max) / e_sum
    tot = p1 + p2
    w1 = g_w * (p1 / tot)
    w2 = g_w * (p2 / tot)
    ids = jnp.where(lane == 0, i1, i2) - float(MOE_GROUPS)
    ids_ref[...] = ids.astype(jnp.int32)
    wts_ref[...] = jnp.where(lane == 0, w1, w2)

    @pl.when(pl.program_id(0) == 0)
    def _():
        cnt_ref[...] = jnp.zeros_like(cnt_ref)

    tm = logits.shape[0]
    hit1 = lane_f == i1
    hit2 = lane_f == i2
    hits = jnp.where(hit1 | hit2, 1.0, 0.0)
    earlier = lax.broadcasted_iota(jnp.int32, (tm, tm), 1) < lax.broadcasted_iota(jnp.int32, (tm, tm), 0)
    before = _dot(jnp.where(earlier, 1.0, 0.0).astype(BF16), hits.astype(BF16)) + cnt_ref[...]
    rank1 = jnp.sum(jnp.where(hit1, before, 0.0), axis=-1, keepdims=True)
    rank2 = jnp.sum(jnp.where(hit2, before, 0.0), axis=-1, keepdims=True)
    rnk_ref[...] = jnp.where(lane == 0, rank1, rank2).astype(jnp.int32)
    cnt_ref[...] = cnt_ref[...] + jnp.sum(hits, axis=0, keepdims=True)
    counts_ref[...] = cnt_ref[...].astype(jnp.int32)


def _route(x, wh, wl, bias, tm):
    t, d = x.shape
    rows = lambda w: pl.BlockSpec((tm, w), lambda i: (i, 0))
    full = lambda r, c: pl.BlockSpec((r, c), lambda i: (0, 0))
    i32 = jnp.int32
    return pl.pallas_call(
        _route_kernel,
        grid=(t // tm,),
        in_specs=[rows(d), full(d, LANES), full(d, LANES), full(1, LANES)],
        out_specs=[rows(LANES), rows(LANES), rows(LANES), full(1, LANES)],
        out_shape=[jax.ShapeDtypeStruct((t, LANES), i32), jax.ShapeDtypeStruct((t, LANES), F32),
                   jax.ShapeDtypeStruct((t, LANES), i32), jax.ShapeDtypeStruct((1, LANES), i32)],
        scratch_shapes=[pltpu.VMEM((1, LANES), F32)],
        compiler_params=_params(("arbitrary",)),
        name="route",
    )(x, wh, wl, bias)


def _row_copy_wait(src, dst, sem, rows):
    pltpu.make_async_copy(src.at[pl.ds(0, rows)], dst.at[pl.ds(0, rows)], sem).wait()


def _dispatch_kernel(dest_ref, zfill_ref, x_ref, xs_hbm, zbuf, zsem, sem):
    i = pl.program_id(0)
    tm = x_ref.shape[0]

    @pl.when(i == 0)
    def _():
        zbuf[...] = jnp.zeros_like(zbuf)
        tail = lambda e: xs_hbm.at[pl.ds(pl.multiple_of(zfill_ref[e], MOE_BLOCK), MOE_BLOCK)]
        for e in range(zfill_ref.shape[0]):
            @pl.when(zfill_ref[e] >= 0)
            def _():
                pltpu.make_async_copy(zbuf, tail(e), zsem).start()
        for e in range(zfill_ref.shape[0]):
            @pl.when(zfill_ref[e] >= 0)
            def _():
                pltpu.make_async_copy(zbuf, tail(e), zsem).wait()

    def send(j, c):
        for k in range(MOE_TOP_K):
            slot = dest_ref[(i * tm + j) * MOE_TOP_K + k]
            pltpu.make_async_copy(x_ref.at[pl.ds(j, 1)], xs_hbm.at[pl.ds(slot, 1)], sem).start(priority=k % 2)
        return c

    lax.fori_loop(0, tm, send, 0, unroll=8)
    for _ in range(MOE_TOP_K):
        _row_copy_wait(x_ref, xs_hbm, sem, tm)


def _dispatch(dest, zfill, x, cap, tm):
    t, d = x.shape
    grid_spec = pltpu.PrefetchScalarGridSpec(
        num_scalar_prefetch=2,
        grid=(t // tm,),
        in_specs=[pl.BlockSpec((tm, d), lambda i, dest, zfill: (i, 0))],
        out_specs=pl.BlockSpec(memory_space=pl.ANY),
        scratch_shapes=[pltpu.VMEM((MOE_BLOCK, d), F32), pltpu.SemaphoreType.DMA, pltpu.SemaphoreType.DMA])
    return pl.pallas_call(
        _dispatch_kernel,
        grid_spec=grid_spec,
        out_shape=jax.ShapeDtypeStruct((cap, d), F32),
        compiler_params=_params(("arbitrary",)),
        name="dispatch",
    )(dest, zfill, x)


def _experts_kernel(bexp_ref, nact_ref, xs_ref, wg_ref, wu_ref, wd_ref, y_ref):
    @pl.when(pl.program_id(0) < nact_ref[0])
    def _():
        xb = xs_ref[...].astype(BF16)
        gate = _dot(xb, wg_ref[0])
        hid = (gate * _sigmoid(gate)) * _dot(xb, wu_ref[0])
        y_ref[...] = _dot(hid.astype(BF16), wd_ref[0])

    @pl.when(pl.program_id(0) >= nact_ref[0])
    def _():
        y_ref[...] = jnp.zeros_like(y_ref)


def _experts(bexp, nact, xs, wg, wu, wd):
    cap, d = xs.shape
    dff = wg.shape[-1]
    by_expert = lambda i, bexp, nact: (bexp[i], 0, 0)
    active = lambda i, bexp, nact: (jnp.minimum(i, nact[0] - 1), 0)
    grid_spec = pltpu.PrefetchScalarGridSpec(
        num_scalar_prefetch=2,
        grid=(cap // MOE_BLOCK,),
        in_specs=[pl.BlockSpec((MOE_BLOCK, d), active),
                  pl.BlockSpec((1, d, dff), by_expert),
                  pl.BlockSpec((1, d, dff), by_expert),
                  pl.BlockSpec((1, dff, d), by_expert)],
        out_specs=pl.BlockSpec((MOE_BLOCK, d), lambda i, bexp, nact: (i, 0)))
    return pl.pallas_call(
        _experts_kernel,
        grid_spec=grid_spec,
        out_shape=jax.ShapeDtypeStruct((cap, d), F32),
        compiler_params=_params(("arbitrary",)),
        name="experts",
    )(bexp, nact, xs, wg, wu, wd)


def _combine_kernel(dest_ref, y_hbm, x_ref, wts_ref, g_ref, b_ref, o_ref, ybuf, sem):
    i = pl.program_id(0)
    n = pl.num_programs(0)
    tm = x_ref.shape[0]
    cur = lax.rem(i, 2)

    def fetch(tile, buf):
        def one(j, c):
            for k in range(MOE_TOP_K):
                slot = dest_ref[(tile * tm + j) * MOE_TOP_K + k]
                pltpu.make_async_copy(y_hbm.at[pl.ds(slot, 1)], ybuf.at[buf, k, pl.ds(j, 1)],
                                      sem.at[buf]).start(priority=k % 2)
            return c
        lax.fori_loop(0, tm, one, 0, unroll=8)

    @pl.when(i == 0)
    def _():
        fetch(0, 0)

    @pl.when(i + 1 < n)
    def _():
        fetch(i + 1, 1 - cur)

    for k in range(MOE_TOP_K):
        _row_copy_wait(y_hbm, ybuf.at[cur, k], sem.at[cur], tm)
    w = wts_ref[...]
    ff = w[:, 0:1] * ybuf[cur, 0]
    for k in range(1, MOE_TOP_K):
        ff = ff + w[:, k:k + 1] * ybuf[cur, k]
    o_ref[...] = _layer_norm(DEEPNORM_ALPHA * x_ref[...] + ff, g_ref[...], b_ref[...])


def _combine(dest, y, x, wts, g, b, tm):
    t, d = x.shape
    rows = lambda w: pl.BlockSpec((tm, w), lambda i, dest: (i, 0))
    full = lambda r, c: pl.BlockSpec((r, c), lambda i, dest: (0, 0))
    grid_spec = pltpu.PrefetchScalarGridSpec(
        num_scalar_prefetch=1,
        grid=(t // tm,),
        in_specs=[pl.BlockSpec(memory_space=pl.ANY), rows(d), rows(LANES), full(1, d), full(1, d)],
        out_specs=rows(d),
        scratch_shapes=[pltpu.VMEM((2, MOE_TOP_K, tm, d), F32), pltpu.SemaphoreType.DMA((2,))])
    return pl.pallas_call(
        _combine_kernel,
        grid_spec=grid_spec,
        out_shape=jax.ShapeDtypeStruct((t, d), F32),
        compiler_params=_params(("arbitrary",)),
        name="combine",
    )(dest, y, x, wts, g, b)


def _dispatch_plan(ids, rnk, counts, n_tok):
    padded = ((counts + MOE_BLOCK - 1) // MOE_BLOCK) * MOE_BLOCK
    pend = jnp.cumsum(padded)
    pstart = pend - padded
    dest = (pstart[ids] + rnk).reshape(-1).astype(jnp.int32)
    cap = n_tok * MOE_TOP_K + MOE_N_EXPERTS * MOE_BLOCK
    n_blocks = cap // MOE_BLOCK
    block_start = jnp.arange(n_blocks, dtype=jnp.int32) * MOE_BLOCK
    bexp = jnp.minimum(jnp.sum(block_start[:, None] >= pend[None, :], axis=1), MOE_N_EXPERTS - 1).astype(jnp.int32)
    nact = (pend[-1] // MOE_BLOCK).astype(jnp.int32).reshape(1)
    tails = jnp.where(padded > 0, pend - MOE_BLOCK, -1)
    idle = pend[-1] + block_start[:MOE_N_EXPERTS]
    zfill = jnp.concatenate([tails, jnp.where(idle < cap, idle, -1)]).astype(jnp.int32)
    return dest, bexp, nact, zfill, cap


def kernel(x, mem, w_in, cmp_pe_k, cmp_pe_v, cmp_w1_k, cmp_w2_k, cmp_w1_v, cmp_w2_v, nsa_norm_g,
           hg_lb_logits, hg_norm_g, w_out, ln1_g, ln1_b, xa_wq, xa_wk, xa_wv, xa_wo, ln2_g, ln2_b,
           moe_w_group, moe_b_group, moe_w_expert, moe_b_expert, moe_w_gate, moe_w_up, moe_w_down,
           ln3_g, ln3_b):
    b, s, d = x.shape
    t = b * s
    g, dh = NSA_KV_GROUPS, NSA_HEAD_DIM
    nch = s // CMP_STRIDE
    row = lambda a: a.reshape(1, -1)
    lb_all = jnp.cumsum(jax.nn.softmax(hg_lb_logits.astype(F32), axis=0), axis=0)
    xt = x.reshape(t, d)
    for l in range(DEPTH):
        nsa_cols = NSA_Q_COLS + 6 * NSA_KV_COLS
        w_l = w_in[l]
        w_perm = jnp.concatenate(
            [w_l[:, :NSA_Q_COLS], w_l[:, nsa_cols + NSA_GATE_COLS:], w_l[:, NSA_Q_COLS:nsa_cols],
             w_l[:, nsa_cols:nsa_cols + NSA_GATE_COLS],
             jnp.zeros((d, LANES - NSA_GATE_COLS), w_l.dtype)], axis=1)
        h, ks, kw, vsT, vwT = _inproj(xt, w_perm.astype(BF16), b, s, 256)
        nkc = h[:, NSA_KV0:NSA_KV0 + NSA_KV_COLS]
        nvc = h[:, NSA_KV0 + NSA_KV_COLS:NSA_KV0 + 2 * NSA_KV_COLS]
        hg_col0 = NSA_Q_COLS // LANES
        gate_tile = (nsa_cols + 4 * HG_COLS) // LANES

        chunks = lambda a: jnp.transpose(a.reshape(b, nch, CMP_STRIDE, g, dh), (0, 3, 1, 2, 4)).reshape(
            b, g, nch, CMP_STRIDE * dh)
        comp = _compress(jnp.stack([chunks(nkc), chunks(nvc)]),
                         jnp.stack([cmp_pe_k[l].reshape(1, -1), cmp_pe_v[l].reshape(1, -1)]),
                         jnp.stack([cmp_w1_k[l], cmp_w1_v[l]]),
                         jnp.stack([cmp_w2_k[l], cmp_w2_v[l]]))
        kc = comp[0].astype(BF16)
        vcT = jnp.swapaxes(comp[1], -1, -2).astype(BF16)

        o_nsa = _nsa(h, kc, vcT, ks, vsT, kw, vwT, s, gate_tile)

        o_hg = _hgrn(h, lb_all[l].reshape(HG_HEADS, 1, HG_KEY_DIM), row(hg_norm_g[l]), b, s, hg_col0)

        x1 = _mix(o_nsa, o_hg, xt, w_out[l].astype(BF16), row(nsa_norm_g[l]), row(ln1_g[l]), row(ln1_b[l]), 512)

        n_mem = mem.shape[1]
        kv = _matmul(mem.reshape(b * n_mem, d),
                     jnp.concatenate([xa_wk[l], xa_wv[l]], axis=1).astype(BF16), n_mem)
        kT = jnp.swapaxes(kv[:, :d].reshape(b, n_mem, d), 1, 2).astype(BF16)
        v = kv[:, d:].reshape(b, n_mem, d).astype(BF16)
        x2 = _xattn(x1.reshape(b, s, d), xa_wq[l].astype(BF16), kT, v, xa_wo[l].astype(BF16),
                    row(ln2_g[l]), row(ln2_b[l]), 512).reshape(t, d)

        w_r = jnp.concatenate([moe_w_group[l], moe_w_expert[l]], axis=1)
        w_r = jnp.pad(w_r, ((0, 0), (0, LANES - w_r.shape[1])))
        b_r = jnp.pad(jnp.concatenate([moe_b_group[l], moe_b_expert[l]]), (0, LANES - MOE_GROUPS - MOE_N_EXPERTS))
        w_rh, w_rl = _split2(w_r)
        ids, wts, rnk, counts = _route(x2, w_rh, w_rl, row(b_r), 512)
        dest, bexp, nact, zfill, cap = _dispatch_plan(
            ids[:, :MOE_TOP_K], rnk[:, :MOE_TOP_K], counts[0, MOE_GROUPS:MOE_GROUPS + MOE_N_EXPERTS], t)
        xs = _dispatch(dest, zfill, x2, cap, 256)
        ys = _experts(bexp, nact, xs, moe_w_gate[l].astype(BF16), moe_w_up[l].astype(BF16),
                      moe_w_down[l].astype(BF16))
        xt = _combine(dest, ys, x2, wts, row(ln3_g[l]), row(ln3_b[l]), 256)
    return xt.reshape(b, s, d)
```
